```python
import jax, jax.numpy as jnp
from jax import lax
import numpy as np

D_MODEL = 1024
BATCH = 32
SEQ = 2048
DEPTH = 1

D_CONV = 512
CONV_GROUPS = 8
CONV_WIDTH = 3
HG_HEADS = 4
HG_KD = 128
HG_VD = 128
D_HG = HG_HEADS * HG_KD
D_MIX = D_CONV + D_HG
D_IN = 3 * D_CONV + 4 * D_HG
CHUNK = 32
PEER_HEADS = 8
N_KEYS = 128
N_EXPERTS = N_KEYS * N_KEYS
D_QUERY = 256
HALF_Q = D_QUERY // 2
PEER_TOPK = 16
TOKEN_BLOCK = 128
EPS = 1e-6

kernel_name = "hymba_conv_hgrn2_peer_block"


def rmsnorm(x, gain):
    x32 = x.astype(jnp.float32)
    y = x32 * lax.rsqrt(jnp.mean(x32 * x32, axis=-1, keepdims=True) + EPS)
    return (y * gain.astype(jnp.float32)).astype(x.dtype)


def group_rmsnorm(x, gain, groups):
    shp = x.shape
    xg = x.reshape(shp[:-1] + (groups, shp[-1] // groups)).astype(jnp.float32)
    y = xg * lax.rsqrt(jnp.mean(xg * xg, axis=-1, keepdims=True) + EPS)
    return (y.reshape(shp) * gain.astype(jnp.float32)).astype(x.dtype)


def short_conv_mixer(c_gate, h, b_gate, conv_w, gain):
    u = c_gate * h
    s = u.shape[1]
    up = jnp.pad(u, ((0, 0), (CONV_WIDTH - 1, 0), (0, 0)))
    y = conv_w[0] * up[:, 0:s]
    for j in range(1, CONV_WIDTH):
        y = y + conv_w[j] * up[:, j:j + s]
    y = b_gate * y
    return group_rmsnorm(y, gain, CONV_GROUPS)


def hgrn2_mixer(q, f_pre, v, g, lb, gain):
    f32 = jnp.float32
    bsz, s, _ = q.shape
    n_chunks = s // CHUNK

    def heads(t):
        return t.reshape(bsz, n_chunks, CHUNK, HG_HEADS, -1).transpose(0, 3, 1, 2, 4)

    qh = heads(jax.nn.silu(q.astype(f32)) * (HG_KD ** -0.5))
    f = lb + (1.0 - lb) * jax.nn.sigmoid(f_pre.astype(f32))
    kh = heads(1.0 - f)
    logf = heads(jnp.log(f))
    vh = heads(v.astype(f32))
    G = jnp.cumsum(logf, axis=3)
    G_last = G[:, :, :, -1:, :]
    q_dec = qh * jnp.exp(G)
    k_dec = kh * jnp.exp(-G)
    causal = jnp.tril(jnp.ones((CHUNK, CHUNK), dtype=bool))
    A = jnp.einsum('bhnck,bhnsk->bhncs', q_dec, k_dec)
    A = jnp.where(causal, A, 0.0)
    o_intra = jnp.einsum('bhncs,bhnsv->bhncv', A, vh)
    k_end = kh * jnp.exp(G_last - G)
    decay = jnp.exp(G_last[:, :, :, 0, :])

    def step(state, xs):
        dec_n, k_n, v_n, q_n = xs
        o_n = jnp.einsum('bhck,bhkv->bhcv', q_n, state)
        state = dec_n[..., None] * state + jnp.einsum('bhck,bhcv->bhkv', k_n, v_n)
        return state, o_n

    state0 = jnp.zeros((bsz, HG_HEADS, HG_KD, HG_VD), f32)
    _, o_inter = lax.scan(step, state0, (jnp.moveaxis(decay, 2, 0), jnp.moveaxis(k_end, 2, 0),
                                         jnp.moveaxis(vh, 2, 0), jnp.moveaxis(q_dec, 2, 0)))
    o = o_intra + jnp.moveaxis(o_inter, 0, 2)
    o = o * lax.rsqrt(jnp.mean(o * o, axis=-1, keepdims=True) + EPS) * gain.astype(f32)
    o = o * jax.nn.silu(heads(g.astype(f32)))
    o = o.transpose(0, 2, 3, 1, 4).reshape(bsz, s, HG_HEADS * HG_VD)
    return o.astype(q.dtype)


def peer_ffn(x, wq, keys, u_tab, v_tab):
    bsz, s, d = x.shape
    t = bsz * s
    xt = x.reshape(t, d)
    q = (xt @ wq).reshape(t, PEER_HEADS, 2, HALF_Q)
    scores = jnp.einsum('thpd,hpnd->thpn', q, keys).astype(jnp.float32)
    s_top, i_top = lax.top_k(scores, PEER_TOPK)
    cand = (s_top[:, :, 0, :, None] + s_top[:, :, 1, None, :]).reshape(t, PEER_HEADS, PEER_TOPK * PEER_TOPK)
    c_score, c_idx = lax.top_k(cand, PEER_TOPK)
    i1 = jnp.take_along_axis(i_top[:, :, 0], c_idx // PEER_TOPK, axis=-1)
    i2 = jnp.take_along_axis(i_top[:, :, 1], c_idx % PEER_TOPK, axis=-1)
    expert = i1 * N_KEYS + i2
    gates = jax.nn.softmax(c_score, axis=-1)
    n_blocks = t // TOKEN_BLOCK

    def block(args):
        xb, eb, gb = args
        u = jnp.take(u_tab, eb, axis=0)
        a = jax.nn.gelu(jnp.einsum('td,thkd->thk', xb, u).astype(jnp.float32), approximate=False)
        w = (gb * a).astype(xb.dtype)
        vv = jnp.take(v_tab, eb, axis=0)
        return jnp.einsum('thk,thkd->td', w, vv)

    y = lax.map(block, (xt.reshape(n_blocks, TOKEN_BLOCK, d),
                        expert.reshape(n_blocks, TOKEN_BLOCK, PEER_HEADS, PEER_TOPK),
                        gates.reshape(n_blocks, TOKEN_BLOCK, PEER_HEADS, PEER_TOPK)))
    return y.reshape(bsz, s, d)


def setup_inputs(seed: int = 0) -> dict:
    key = jax.random.key(seed)
    ks = jax.random.split(key, 16)
    f32 = jnp.float32
    nrm = lambda k, shp: jax.random.normal(k, shp, f32)
    return {
        "x": nrm(ks[0], (BATCH, SEQ, D_MODEL)),
        "norm_mix": 1.0 + 0.02 * nrm(ks[1], (DEPTH, D_MODEL)),
        "w_in": nrm(ks[2], (DEPTH, D_MODEL, D_IN)) * D_MODEL ** -0.5,
        "conv_w": nrm(ks[3], (DEPTH, CONV_WIDTH, D_CONV)) * CONV_WIDTH ** -0.5,
        "conv_gain": 1.0 + 0.02 * nrm(ks[4], (DEPTH, D_CONV)),
        "hg_lb_logits": 0.1 * nrm(ks[5], (DEPTH + 1, D_HG)),
        "hg_gain": 1.0 + 0.02 * nrm(ks[6], (DEPTH, HG_VD)),
        "w_out": nrm(ks[7], (DEPTH, D_MIX, D_MODEL)) * D_MIX ** -0.5,
        "norm_ffn": 1.0 + 0.02 * nrm(ks[8], (DEPTH, D_MODEL)),
        "peer_wq": nrm(ks[9], (DEPTH, D_MODEL, PEER_HEADS * D_QUERY)) * D_MODEL ** -0.5,
        "peer_keys": nrm(ks[10], (DEPTH, PEER_HEADS, 2, N_KEYS, HALF_Q)) * HALF_Q ** -0.5,
        "peer_u": nrm(ks[11], (DEPTH, N_EXPERTS, D_MODEL)) * D_MODEL ** -0.5,
        "peer_v": nrm(ks[12], (DEPTH, N_EXPERTS, D_MODEL)) * PEER_HEADS ** -0.5,
        "norm_f": 1.0 + 0.02 * nrm(ks[13], (D_MODEL,)),
    }


def reference(x, norm_mix, w_in, conv_w, conv_gain, hg_lb_logits, hg_gain, w_out,
              norm_ffn, peer_wq, peer_keys, peer_u, peer_v, norm_f):
    lb_all = jnp.cumsum(jax.nn.softmax(hg_lb_logits.astype(jnp.float32), axis=0), axis=0)
    splits = [D_CONV, 2 * D_CONV, 3 * D_CONV, 3 * D_CONV + D_HG, 3 * D_CONV + 2 * D_HG, 3 * D_CONV + 3 * D_HG]
    for l in range(DEPTH):
        h = rmsnorm(x, norm_mix[l])
        proj = h @ w_in[l]
        c_gate, hc, b_gate, q, f_pre, v, g = jnp.split(proj, splits, axis=-1)
        conv_out = short_conv_mixer(c_gate, hc, b_gate, conv_w[l], conv_gain[l])
        hg_out = hgrn2_mixer(q, f_pre, v, g, lb_all[l], hg_gain[l])
        mix = jnp.concatenate([conv_out, hg_out], axis=-1)
        x = x + mix @ w_out[l]
        h2 = rmsnorm(x, norm_ffn[l])
        x = x + peer_ffn(h2, peer_wq[l], peer_keys[l], peer_u[l], peer_v[l])
    return rmsnorm(x, norm_f)
```

```python
import functools

import numpy as np
import jax
import jax.numpy as jnp
from jax import lax
from jax.experimental import pallas as pl
from jax.experimental.pallas import tpu as pltpu

F32 = jnp.float32
BF16 = jnp.bfloat16
I32 = jnp.int32

D_MODEL = 1024
D_CONV = 512
CONV_GROUP = 64
CONV_WIDTH = 3
HG_HEADS = 4
HG_KD = 128
HG_VD = 128
D_HG = HG_HEADS * HG_KD
D_IN = 3 * D_CONV + 4 * D_HG
CHUNK = 32
PEER_HEADS = 8
N_KEYS = 128
HALF_Q = 128
PEER_TOPK = 16
N_PAIRS = PEER_HEADS * PEER_TOPK
EPS = 1e-6

LANES = 128
SUBLANES = 8
SEQ_BLOCK = 256
TOK_BLOCK = 128
ROW_WORDS = D_MODEL // (2 * LANES)
VMEM_LIMIT_MIXER = 48 * 1024 * 1024
VMEM_LIMIT_EXPERT = 52 * 1024 * 1024

NT_DIMS = (((1,), (1,)), ((), ()))


def _bdot(a, b):
    return jnp.dot(a.astype(BF16), b.astype(BF16), preferred_element_type=F32)


def _bdot_nt(a, b):
    return lax.dot_general(a.astype(BF16), b.astype(BF16), NT_DIMS, preferred_element_type=F32)


def _split3(a):
    a1 = a.astype(BF16)
    r1 = a - a1.astype(F32)
    a2 = r1.astype(BF16)
    a3 = (r1 - a2.astype(F32)).astype(BF16)
    return a1, a2, a3


def _dot_sel(sel, a):
    out = None
    for piece in _split3(a):
        d = jnp.dot(sel, piece, preferred_element_type=F32)
        out = d if out is None else out + d
    return out


def _dot_sel_right(a, sel):
    out = None
    for piece in _split3(a):
        d = jnp.dot(piece, sel, preferred_element_type=F32)
        out = d if out is None else out + d
    return out


def _silu(x):
    return x * (1.0 / (1.0 + jnp.exp(-x)))


def _sigmoid(x):
    return 1.0 / (1.0 + jnp.exp(-x))


def _rms(x, gain):
    return x * lax.rsqrt(jnp.mean(x * x, axis=-1, keepdims=True) + EPS) * gain


def _mixer_kernel(x_ref, nmix_ref, win_ref, convw_ref, cgain_ref, lbl_ref, hgain_ref, wout_ref,
                  nffn_ref, wq_ref, keys_ref, cum_ref, gmean_ref, hmean_ref,
                  x1_ref, h2_ref, sc_ref,
                  up_ref, st_ref):
    L = SEQ_BLOCK
    sb = pl.program_id(1)

    @pl.when(sb == 0)
    def _():
        up_ref[0:SUBLANES, :] = jnp.zeros((SUBLANES, D_CONV), F32)
        st_ref[...] = jnp.zeros(st_ref.shape, F32)

    x = x_ref[0]
    h = _rms(x, nmix_ref[...])
    proj = _bdot(h, win_ref[...])

    u = proj[:, 0:D_CONV] * proj[:, D_CONV:2 * D_CONV]
    up_ref[SUBLANES:SUBLANES + L, :] = u
    u1 = up_ref[SUBLANES - 1:SUBLANES - 1 + L, :]
    u2 = up_ref[SUBLANES - 2:SUBLANES - 2 + L, :]
    cw = convw_ref[...]
    y = cw[0:1, :] * u2 + cw[1:2, :] * u1 + cw[2:3, :] * u
    y = proj[:, 2 * D_CONV:3 * D_CONV] * y
    up_ref[0:SUBLANES, :] = up_ref[L:L + SUBLANES, :]
    gms = _dot_sel_right(y * y, gmean_ref[...])
    conv_out = y * lax.rsqrt(gms + EPS) * cgain_ref[...]

    o0 = 3 * D_CONV
    qp = proj[:, o0:o0 + D_HG]
    fp = proj[:, o0 + D_HG:o0 + 2 * D_HG]
    vv = proj[:, o0 + 2 * D_HG:o0 + 3 * D_HG]
    gg = proj[:, o0 + 3 * D_HG:o0 + 4 * D_HG]
    lbl = lbl_ref[...]
    lbe = jnp.exp(lbl - jnp.max(lbl, axis=0, keepdims=True))
    lb = lbe[0:1, :] / jnp.sum(lbe, axis=0, keepdims=True)
    qh = _silu(qp) * (HG_KD ** -0.5)
    f = lb + (1.0 - lb) * _sigmoid(fp)
    kh = 1.0 - f
    logf = jnp.log(f)
    cums = _dot_sel(cum_ref[...], logf)
    gcum = cums[0:L, :]
    glast = cums[L:2 * L, :]
    q_dec = qh * jnp.exp(gcum)
    k_dec = kh * jnp.exp(-gcum)
    k_end = kh * jnp.exp(glast - gcum)
    decay = jnp.exp(glast)

    row = lax.broadcasted_iota(I32, (L, L), 0)
    col = lax.broadcasted_iota(I32, (L, L), 1)
    causal = (row // CHUNK == col // CHUNK) & (col <= row)

    o_heads = []
    for hh in range(HG_HEADS):
        cs = slice(hh * HG_KD, (hh + 1) * HG_KD)
        qd, kd, ke, vh, dc = q_dec[:, cs], k_dec[:, cs], k_end[:, cs], vv[:, cs], decay[:, cs]
        a = _bdot_nt(qd, kd)
        a = jnp.where(causal, a, 0.0)
        o_intra = _bdot(a, vh)
        st = st_ref[hh]
        pieces = []
        for n in range(L // CHUNK):
            rs = slice(n * CHUNK, (n + 1) * CHUNK)
            pieces.append(_bdot_nt(qd[rs], st))
            st = dc[n * CHUNK:n * CHUNK + 1, :] * st + _bdot(vh[rs].T, ke[rs])
        st_ref[hh] = st
        o_heads.append(o_intra + jnp.concatenate(pieces, axis=0))
    o = jnp.concatenate(o_heads, axis=1)
    oms = _dot_sel_right(o * o, hmean_ref[...])
    o = o * lax.rsqrt(oms + EPS) * hgain_ref[...]
    o = o * _silu(gg)

    mix = jnp.concatenate([conv_out, o], axis=1)
    x1 = x + _bdot(mix, wout_ref[...])
    x1_ref[0] = x1
    h2 = _rms(x1, nffn_ref[...])
    h2_ref[0] = h2.astype(BF16)
    q2 = _bdot(h2, wq_ref[...])
    for hp in range(2 * PEER_HEADS):
        sc_ref[hp] = _bdot_nt(keys_ref[hp], q2[:, hp * HALF_Q:(hp + 1) * HALF_Q])


def _mixer_call(x, norm_mix, w_in, conv_w, conv_gain, lb_logits, hg_gain, w_out, norm_ffn, wq, keys):
    B, S, _ = x.shape
    L = SEQ_BLOCK
    nsb = S // L
    T = B * S
    r = np.arange(L)
    same = (r[:, None] // CHUNK) == (r[None, :] // CHUNK)
    cum = np.concatenate([same & (r[None, :] <= r[:, None]), same], axis=0).astype(np.float32)
    c = np.arange(D_CONV)
    gmean = ((c[:, None] // CONV_GROUP) == (c[None, :] // CONV_GROUP)).astype(np.float32) / CONV_GROUP
    hmean = ((c[:, None] // HG_VD) == (c[None, :] // HG_VD)).astype(np.float32) / HG_VD

    const = lambda shape: pl.BlockSpec(shape, lambda b, s: (0,) * len(shape))
    return pl.pallas_call(
        _mixer_kernel,
        grid=(B, nsb),
        in_specs=[
            pl.BlockSpec((1, L, D_MODEL), lambda b, s: (b, s, 0)),
            const((1, D_MODEL)), const((D_MODEL, D_IN)), const((CONV_WIDTH, D_CONV)), const((1, D_CONV)),
            const(lb_logits.shape), const((1, D_HG)), const((D_MODEL, D_MODEL)), const((1, D_MODEL)),
            const((D_MODEL, 2 * PEER_HEADS * HALF_Q)), const((2 * PEER_HEADS, N_KEYS, HALF_Q)),
            const((2 * L, L)), const((D_CONV, D_CONV)), const((D_HG, D_HG)),
        ],
        out_specs=[
            pl.BlockSpec((1, L, D_MODEL), lambda b, s: (b, s, 0)),
            pl.BlockSpec((1, L, D_MODEL), lambda b, s: (b, s, 0)),
            pl.BlockSpec((2 * PEER_HEADS, N_KEYS, L), lambda b, s: (0, 0, b * nsb + s)),
        ],
        out_shape=[
            jax.ShapeDtypeStruct((B, S, D_MODEL), F32),
            jax.ShapeDtypeStruct((B, S, D_MODEL), BF16),
            jax.ShapeDtypeStruct((2 * PEER_HEADS, N_KEYS, T), F32),
        ],
        scratch_shapes=[
            pltpu.VMEM((L + 2 * SUBLANES, D_CONV), F32),
            pltpu.VMEM((HG_HEADS, HG_VD, HG_KD), F32),
        ],
        compiler_params=pltpu.CompilerParams(
            dimension_semantics=("arbitrary", "arbitrary"), vmem_limit_bytes=VMEM_LIMIT_MIXER),
        name="mixer",
    )(x, norm_mix.reshape(1, D_MODEL), w_in.astype(BF16), conv_w, conv_gain.reshape(1, D_CONV), lb_logits,
      jnp.tile(hg_gain.reshape(1, HG_VD), (1, HG_HEADS)), w_out.astype(BF16), norm_ffn.reshape(1, D_MODEL),
      wq.astype(BF16), keys.reshape(2 * PEER_HEADS, N_KEYS, HALF_Q).astype(BF16),
      jnp.asarray(cum, BF16), jnp.asarray(gmean, BF16), jnp.asarray(hmean, BF16))


def _extract_topk(vals, iota, n, sentinel):
    tops, idxs = [], []
    for _ in range(n):
        m = jnp.max(vals, axis=0, keepdims=True)
        i = jnp.min(jnp.where(vals == m, iota, sentinel), axis=0, keepdims=True)
        tops.append(m)
        idxs.append(i)
        vals = jnp.where(iota == i, -jnp.inf, vals)
    return jnp.concatenate(tops, axis=0), jnp.concatenate(idxs, axis=0)


def _topk_kernel(sc_ref, idx_ref, gate_ref, stop_ref, itop_ref):
    TB = TOK_BLOCK
    iota_keys = lax.broadcasted_iota(I32, (N_KEYS, TB), 0)

    def stage1(hp, c):
        s_top, i_top = _extract_topk(sc_ref[hp], iota_keys, PEER_TOPK, N_KEYS)
        stop_ref[hp] = s_top
        itop_ref[hp] = i_top
        return c

    lax.fori_loop(0, 2 * PEER_HEADS, stage1, 0)

    ncand = PEER_TOPK * PEER_TOPK
    iota_cand = lax.broadcasted_iota(I32, (ncand, TB), 0)
    iota_k = lax.broadcasted_iota(I32, (PEER_TOPK, TB), 0)

    def stage2(hd, c):
        s1 = stop_ref[2 * hd]
        s2 = stop_ref[2 * hd + 1]
        i1t = itop_ref[2 * hd]
        i2t = itop_ref[2 * hd + 1]
        cand = jnp.concatenate([s1[a:a + 1, :] + s2 for a in range(PEER_TOPK)], axis=0)
        c_score, c_idx = _extract_topk(cand, iota_cand, PEER_TOPK, ncand)
        a_sel = c_idx // PEER_TOPK
        b_sel = c_idx % PEER_TOPK
        i1 = jnp.zeros((PEER_TOPK, TB), I32)
        i2 = jnp.zeros((PEER_TOPK, TB), I32)
        for a in range(PEER_TOPK):
            i1 = i1 + jnp.where(a_sel == a, i1t[a:a + 1, :], 0)
            i2 = i2 + jnp.where(b_sel == a, i2t[a:a + 1, :], 0)
        expert = i1 * N_KEYS + i2
        e = jnp.exp(c_score - jnp.max(c_score, axis=0, keepdims=True))
        gates = e / jnp.sum(e, axis=0, keepdims=True)
        off = pl.multiple_of(hd * PEER_TOPK, PEER_TOPK)
        stop_ref[2 * hd] = gates
        itop_ref[2 * hd] = expert * ROW_WORDS
        return c

    lax.fori_loop(0, PEER_HEADS, stage2, 0)

    half = N_PAIRS // 2
    for hd in range(PEER_HEADS):
        gate_ref[0, hd * PEER_TOPK:(hd + 1) * PEER_TOPK, :] = stop_ref[2 * hd]
    lo = jnp.concatenate([itop_ref[2 * hd] for hd in range(PEER_HEADS // 2)], axis=0)
    hi = jnp.concatenate([itop_ref[2 * hd] for hd in range(PEER_HEADS // 2, PEER_HEADS)], axis=0)
    idx_ref[0] = lo | (hi << 16)


def _topk_call(scores_t):
    T = scores_t.shape[-1]
    nblk = T // TOK_BLOCK
    return pl.pallas_call(
        _topk_kernel,
        grid=(nblk,),
        in_specs=[pl.BlockSpec((2 * PEER_HEADS, N_KEYS, TOK_BLOCK), lambda i: (0, 0, i))],
        out_specs=[
            pl.BlockSpec((1, N_PAIRS // 2, TOK_BLOCK), lambda i: (i, 0, 0)),
            pl.BlockSpec((1, N_PAIRS, TOK_BLOCK), lambda i: (i, 0, 0)),
        ],
        out_shape=[
            jax.ShapeDtypeStruct((nblk, N_PAIRS // 2, TOK_BLOCK), I32),
            jax.ShapeDtypeStruct((nblk, N_PAIRS, TOK_BLOCK), F32),
        ],
        scratch_shapes=[
            pltpu.VMEM((2 * PEER_HEADS, PEER_TOPK, TOK_BLOCK), F32),
            pltpu.VMEM((2 * PEER_HEADS, PEER_TOPK, TOK_BLOCK), I32),
        ],
        compiler_params=pltpu.CompilerParams(dimension_semantics=("arbitrary",)),
        name="topk",
    )(scores_t)


def _pack_table(tab):
    n = tab.shape[0]
    t = tab.astype(BF16).reshape(n, ROW_WORDS, 2, LANES).transpose(0, 1, 3, 2)
    return lax.bitcast_convert_type(t, I32).reshape(n * ROW_WORDS, LANES)


def _pair_offsets(idx_ref, w, j):
    word = idx_ref[w * TOK_BLOCK + j]
    lo = pl.multiple_of(word & 0xFFFF, ROW_WORDS)
    hi = pl.multiple_of(lax.shift_right_logical(word, 16), ROW_WORDS)
    return ((w, lo), (w + N_PAIRS // 2, hi))


def _expert_u_kernel(idx_ref, tab_ref, h2_ref, gate_ref, expand_ref, wrep_ref, g0_ref, g1_ref, a_ref):
    TB = TOK_BLOCK
    nchunk = D_MODEL // LANES

    def gather(g_ref, j):
        for w in range(N_PAIRS // 2):
            for k, off in _pair_offsets(idx_ref, w, j):
                slab = tab_ref[pl.ds(off, ROW_WORDS), :]
                g_ref[pl.ds(k * SUBLANES, SUBLANES), :] = pltpu.bitcast(slab, BF16).astype(F32)

    def dots(g_ref):
        r = None
        for c in range(nchunk):
            lhs = g_ref[pl.ds(c, N_PAIRS, stride=nchunk), :].astype(BF16)
            d = lax.dot_general(lhs, h2_ref[:, c * LANES:(c + 1) * LANES], NT_DIMS, preferred_element_type=F32)
            r = d if r is None else r + d
        return r

    lane = lax.broadcasted_iota(I32, (N_PAIRS, TB), 1)
    a_ref[...] = jnp.zeros((N_PAIRS, TB), F32)
    gather(g0_ref, 0)

    def two_tokens(i, c):
        j = 2 * i
        gather(g1_ref, j + 1)
        a_ref[...] = jnp.where(lane == j, dots(g0_ref), a_ref[...])
        gather(g0_ref, jnp.minimum(j + 2, TB - 1))
        a_ref[...] = jnp.where(lane == j + 1, dots(g1_ref), a_ref[...])
        return c

    lax.fori_loop(0, TB // 2, two_tokens, 0)

    a = a_ref[...]
    w = gate_ref[0] * (0.5 * a * (1.0 + lax.erf(a * (2.0 ** -0.5))))
    wrep_ref[...] = jnp.dot(w.T.astype(BF16), expand_ref[...], preferred_element_type=F32)


def _expert_u_call(idx_words, tab, h2, gates):
    T = h2.shape[0]
    nblk = T // TOK_BLOCK
    expand = (np.arange(N_PAIRS)[:, None] == (np.arange(D_MODEL)[None, :] // SUBLANES)).astype(np.float32)
    return pl.pallas_call(
        _expert_u_kernel,
        grid=(nblk,),
        in_specs=[
            pl.BlockSpec((N_PAIRS // 2 * TOK_BLOCK,), lambda i: (i,), memory_space=pltpu.SMEM),
            pl.BlockSpec(memory_space=pltpu.VMEM),
            pl.BlockSpec((TOK_BLOCK, D_MODEL), lambda i: (i, 0)),
            pl.BlockSpec((1, N_PAIRS, TOK_BLOCK), lambda i: (i, 0, 0)),
            pl.BlockSpec((N_PAIRS, D_MODEL), lambda i: (0, 0)),
        ],
        out_specs=pl.BlockSpec((TOK_BLOCK, D_MODEL), lambda i: (i, 0)),
        out_shape=jax.ShapeDtypeStruct((T, D_MODEL), F32),
        scratch_shapes=[
            pltpu.VMEM((N_PAIRS * SUBLANES, LANES), F32),
            pltpu.VMEM((N_PAIRS * SUBLANES, LANES), F32),
            pltpu.VMEM((N_PAIRS, TOK_BLOCK), F32),
        ],
        compiler_params=pltpu.CompilerParams(
            dimension_semantics=("arbitrary",), vmem_limit_bytes=VMEM_LIMIT_EXPERT),
        name="expert_u",
    )(idx_words, tab, h2, gates, jnp.asarray(expand, BF16))


def _expert_v_kernel(idx_ref, tab_ref, wrep_ref, x1_ref, nf_ref, out_ref, g0_ref, g1_ref, y_ref):
    TB = TOK_BLOCK
    sub = lax.broadcasted_iota(I32, (SUBLANES, D_MODEL), 0)
    col = lax.broadcasted_iota(I32, (SUBLANES, D_MODEL), 1)
    diag = (col % SUBLANES) == sub

    def gather(g_ref, j):
        for w in range(N_PAIRS // 2):
            for k, off in _pair_offsets(idx_ref, w, j):
                g_ref[pl.ds(k * ROW_WORDS, ROW_WORDS), :] = tab_ref[pl.ds(off, ROW_WORDS), :]

    def weighted_sum(g_ref, j):
        wr = jnp.broadcast_to(wrep_ref[j], (SUBLANES, D_MODEL))
        wm = jnp.where(diag, wr, 0.0).astype(BF16)
        gb = pltpu.bitcast(g_ref[...], BF16)
        y_ref[j] = jnp.dot(wm, gb, preferred_element_type=F32)

    gather(g0_ref, 0)

    def two_tokens(i, c):
        j = 2 * i
        gather(g1_ref, j + 1)
        weighted_sum(g0_ref, j)
        gather(g0_ref, jnp.minimum(j + 2, TB - 1))
        weighted_sum(g1_ref, j + 1)
        return c

    lax.fori_loop(0, TB // 2, two_tokens, 0)

    z = x1_ref[...] + y_ref[...]
    ss = jnp.sum(jnp.sum(z * z, axis=2, keepdims=True), axis=1, keepdims=True)
    out_ref[...] = z * lax.rsqrt(ss * (1.0 / D_MODEL) + EPS) * nf_ref[...]


def _expert_v_call(idx_words, tab, wrep, x1, norm_f):
    T = x1.shape[0]
    nblk = T // TOK_BLOCK
    rows = D_MODEL // LANES
    return pl.pallas_call(
        _expert_v_kernel,
        grid=(nblk,),
        in_specs=[
            pl.BlockSpec((N_PAIRS // 2 * TOK_BLOCK,), lambda i: (i,), memory_space=pltpu.SMEM),
            pl.BlockSpec(memory_space=pltpu.VMEM),
            pl.BlockSpec((TOK_BLOCK, 1, D_MODEL), lambda i: (i, 0, 0)),
            pl.BlockSpec((TOK_BLOCK, rows, LANES), lambda i: (i, 0, 0)),
            pl.BlockSpec((1, rows, LANES), lambda i: (0, 0, 0)),
        ],
        out_specs=pl.BlockSpec((TOK_BLOCK, rows, LANES), lambda i: (i, 0, 0)),
        out_shape=jax.ShapeDtypeStruct((T, rows, LANES), F32),
        scratch_shapes=[
            pltpu.VMEM((N_PAIRS * ROW_WORDS, LANES), I32),
            pltpu.VMEM((N_PAIRS * ROW_WORDS, LANES), I32),
            pltpu.VMEM((TOK_BLOCK, rows, LANES), F32),
        ],
        compiler_params=pltpu.CompilerParams(
            dimension_semantics=("arbitrary",), vmem_limit_bytes=VMEM_LIMIT_EXPERT),
        name="expert_v",
    )(idx_words, tab, wrep.reshape(T, 1, D_MODEL), x1.reshape(T, rows, LANES), norm_f.reshape(1, rows, LANES))


def kernel(x, norm_mix, w_in, conv_w, conv_gain, hg_lb_logits, hg_gain, w_out, norm_ffn, peer_wq, peer_keys,
           peer_u, peer_v, norm_f):
    B, S, D = x.shape
    depth = norm_mix.shape[0]
    assert depth == 1 and D == D_MODEL and S % SEQ_BLOCK == 0 and (B * S) % TOK_BLOCK == 0
    T = B * S
    x1, h2, scores_t = _mixer_call(x, norm_mix[0], w_in[0], conv_w[0], conv_gain[0], hg_lb_logits, hg_gain[0],
                                   w_out[0], norm_ffn[0], peer_wq[0], peer_keys[0])
    idx_words, gates = _topk_call(scores_t)
    idx_words = idx_words.reshape(-1)
    wrep = _expert_u_call(idx_words, _pack_table(peer_u[0]), h2.reshape(T, D), gates)
    out = _expert_v_call(idx_words, _pack_table(peer_v[0]), wrep, x1.reshape(T, D), norm_f)
    return out.reshape(B, S, D)
```

```python
import functools

import numpy as np
import jax
import jax.numpy as jnp
from jax import lax
from jax.experimental import pallas as pl
from jax.experimental.pallas import tpu as pltpu

F32 = jnp.float32
BF16 = jnp.bfloat16
I32 = jnp.int32

D_MODEL = 1024
D_CONV = 512
CONV_GROUP = 64
CONV_WIDTH = 3
HG_HEADS = 4
HG_KD = 128
HG_VD = 128
D_HG = HG_HEADS * HG_KD
D_IN = 3 * D_CONV + 4 * D_HG
CHUNK = 32
PEER_HEADS = 8
N_KEYS = 128
HALF_Q = 128
PEER_TOPK = 16
N_PAIRS = PEER_HEADS * PEER_TOPK
EPS = 1e-6

LANES = 128
SUBLANES = 8
SEQ_BLOCK = 256
TOK_BLOCK = 128
ROW_WORDS = D_MODEL // (2 * LANES)
ARAW_PITCH = TOK_BLOCK + 1
VMEM_LIMIT_MIXER = 48 * 1024 * 1024
VMEM_LIMIT_EXPERT = 52 * 1024 * 1024

NT_DIMS = (((1,), (1,)), ((), ()))


def _bdot(a, b):
    return jnp.dot(a.astype(BF16), b.astype(BF16), preferred_element_type=F32)


def _bdot_nt(a, b):
    return lax.dot_general(a.astype(BF16), b.astype(BF16), NT_DIMS, preferred_element_type=F32)


def _split3(a):
    a1 = a.astype(BF16)
    r1 = a - a1.astype(F32)
    a2 = r1.astype(BF16)
    a3 = (r1 - a2.astype(F32)).astype(BF16)
    return a1, a2, a3


def _dot_sel(sel, a):
    out = None
    for piece in _split3(a):
        d = jnp.dot(sel, piece, preferred_element_type=F32)
        out = d if out is None else out + d
    return out


def _dot_sel_right(a, sel):
    out = None
    for piece in _split3(a):
        d = jnp.dot(piece, sel, preferred_element_type=F32)
        out = d if out is None else out + d
    return out


def _silu(x):
    return x * (1.0 / (1.0 + jnp.exp(-x)))


def _sigmoid(x):
    return 1.0 / (1.0 + jnp.exp(-x))


def _rms(x, gain):
    return x * lax.rsqrt(jnp.mean(x * x, axis=-1, keepdims=True) + EPS) * gain


def _mixer_kernel(x_ref, nmix_ref, win_ref, convw_ref, cgain_ref, lbl_ref, hgain_ref, wout_ref,
                  nffn_ref, wq_ref, keys_ref, cum_ref, gmean_ref, hmean_ref,
                  x1_ref, h2_ref, sc_ref,
                  up_ref, st_ref):
    L = SEQ_BLOCK
    sb = pl.program_id(1)

    @pl.when(sb == 0)
    def _():
        up_ref[0:SUBLANES, :] = jnp.zeros((SUBLANES, D_CONV), F32)
        st_ref[...] = jnp.zeros(st_ref.shape, F32)

    x = x_ref[0]
    h = _rms(x, nmix_ref[...])
    proj = _bdot(h, win_ref[...])

    u = proj[:, 0:D_CONV] * proj[:, D_CONV:2 * D_CONV]
    up_ref[SUBLANES:SUBLANES + L, :] = u
    u1 = up_ref[SUBLANES - 1:SUBLANES - 1 + L, :]
    u2 = up_ref[SUBLANES - 2:SUBLANES - 2 + L, :]
    cw = convw_ref[...]
    y = cw[0:1, :] * u2 + cw[1:2, :] * u1 + cw[2:3, :] * u
    y = proj[:, 2 * D_CONV:3 * D_CONV] * y
    up_ref[0:SUBLANES, :] = up_ref[L:L + SUBLANES, :]
    gms = _dot_sel_right(y * y, gmean_ref[...])
    conv_out = y * lax.rsqrt(gms + EPS) * cgain_ref[...]

    o0 = 3 * D_CONV
    qp = proj[:, o0:o0 + D_HG]
    fp = proj[:, o0 + D_HG:o0 + 2 * D_HG]
    vv = proj[:, o0 + 2 * D_HG:o0 + 3 * D_HG]
    gg = proj[:, o0 + 3 * D_HG:o0 + 4 * D_HG]
    lbl = lbl_ref[...]
    lbe = jnp.exp(lbl - jnp.max(lbl, axis=0, keepdims=True))
    lb = lbe[0:1, :] / jnp.sum(lbe, axis=0, keepdims=True)
    qh = _silu(qp) * (HG_KD ** -0.5)
    f = lb + (1.0 - lb) * _sigmoid(fp)
    kh = 1.0 - f
    logf = jnp.log(f)
    cums = _dot_sel(cum_ref[...], logf)
    gcum = cums[0:L, :]
    glast = cums[L:2 * L, :]
    q_dec = qh * jnp.exp(gcum)
    k_dec = kh * jnp.exp(-gcum)
    k_end = kh * jnp.exp(glast - gcum)
    decay = jnp.exp(glast)

    row = lax.broadcasted_iota(I32, (L, L), 0)
    col = lax.broadcasted_iota(I32, (L, L), 1)
    causal = (row // CHUNK == col // CHUNK) & (col <= row)

    o_heads = []
    for hh in range(HG_HEADS):
        cs = slice(hh * HG_KD, (hh + 1) * HG_KD)
        qd, kd, ke, vh, dc = q_dec[:, cs], k_dec[:, cs], k_end[:, cs], vv[:, cs], decay[:, cs]
        a = _bdot_nt(qd, kd)
        a = jnp.where(causal, a, 0.0)
        o_intra = _bdot(a, vh)
        st = st_ref[hh]
        pieces = []
        for n in range(L // CHUNK):
            rs = slice(n * CHUNK, (n + 1) * CHUNK)
            pieces.append(_bdot_nt(qd[rs], st))
            st = dc[n * CHUNK:n * CHUNK + 1, :] * st + _bdot(vh[rs].T, ke[rs])
        st_ref[hh] = st
        o_heads.append(o_intra + jnp.concatenate(pieces, axis=0))
    o = jnp.concatenate(o_heads, axis=1)
    oms = _dot_sel_right(o * o, hmean_ref[...])
    o = o * lax.rsqrt(oms + EPS) * hgain_ref[...]
    o = o * _silu(gg)

    mix = jnp.concatenate([conv_out, o], axis=1)
    x1 = x + _bdot(mix, wout_ref[...])
    x1_ref[0] = x1
    h2 = _rms(x1, nffn_ref[...])
    h2_ref[0] = h2.astype(BF16)
    q2 = _bdot(h2, wq_ref[...])
    for hp in range(2 * PEER_HEADS):
        sc_ref[hp] = _bdot_nt(keys_ref[hp], q2[:, hp * HALF_Q:(hp + 1) * HALF_Q])


def _mixer_call(x, norm_mix, w_in, conv_w, conv_gain, lb_logits, hg_gain, w_out, norm_ffn, wq, keys):
    B, S, _ = x.shape
    L = SEQ_BLOCK
    nsb = S // L
    T = B * S
    r = np.arange(L)
    same = (r[:, None] // CHUNK) == (r[None, :] // CHUNK)
    cum = np.concatenate([same & (r[None, :] <= r[:, None]), same], axis=0).astype(np.float32)
    c = np.arange(D_CONV)
    gmean = ((c[:, None] // CONV_GROUP) == (c[None, :] // CONV_GROUP)).astype(np.float32) / CONV_GROUP
    hmean = ((c[:, None] // HG_VD) == (c[None, :] // HG_VD)).astype(np.float32) / HG_VD

    const = lambda shape: pl.BlockSpec(shape, lambda b, s: (0,) * len(shape))
    return pl.pallas_call(
        _mixer_kernel,
        grid=(B, nsb),
        in_specs=[
            pl.BlockSpec((1, L, D_MODEL), lambda b, s: (b, s, 0)),
            const((1, D_MODEL)), const((D_MODEL, D_IN)), const((CONV_WIDTH, D_CONV)), const((1, D_CONV)),
            const(lb_logits.shape), const((1, D_HG)), const((D_MODEL, D_MODEL)), const((1, D_MODEL)),
            const((D_MODEL, 2 * PEER_HEADS * HALF_Q)), const((2 * PEER_HEADS, N_KEYS, HALF_Q)),
            const((2 * L, L)), const((D_CONV, D_CONV)), const((D_HG, D_HG)),
        ],
        out_specs=[
            pl.BlockSpec((1, L, D_MODEL), lambda b, s: (b, s, 0)),
            pl.BlockSpec((1, L, D_MODEL), lambda b, s: (b, s, 0)),
            pl.BlockSpec((2 * PEER_HEADS, N_KEYS, L), lambda b, s: (0, 0, b * nsb + s)),
        ],
        out_shape=[
            jax.ShapeDtypeStruct((B, S, D_MODEL), F32),
            jax.ShapeDtypeStruct((B, S, D_MODEL), BF16),
            jax.ShapeDtypeStruct((2 * PEER_HEADS, N_KEYS, T), F32),
        ],
        scratch_shapes=[
            pltpu.VMEM((L + 2 * SUBLANES, D_CONV), F32),
            pltpu.VMEM((HG_HEADS, HG_VD, HG_KD), F32),
        ],
        compiler_params=pltpu.CompilerParams(
            dimension_semantics=("arbitrary", "arbitrary"), vmem_limit_bytes=VMEM_LIMIT_MIXER),
        name="mixer",
    )(x, norm_mix.reshape(1, D_MODEL), w_in.astype(BF16), conv_w, conv_gain.reshape(1, D_CONV), lb_logits,
      jnp.tile(hg_gain.reshape(1, HG_VD), (1, HG_HEADS)), w_out.astype(BF16), norm_ffn.reshape(1, D_MODEL),
      wq.astype(BF16), keys.reshape(2 * PEER_HEADS, N_KEYS, HALF_Q).astype(BF16),
      jnp.asarray(cum, BF16), jnp.asarray(gmean, BF16), jnp.asarray(hmean, BF16))


def _extract_topk(vals, ids, n, sentinel):
    tops, sel = [], []
    for _ in range(n):
        m = jnp.max(vals, axis=0, keepdims=True)
        i = jnp.min(jnp.where(vals == m, ids, sentinel), axis=0, keepdims=True)
        tops.append(m)
        sel.append(i)
        vals = jnp.where(ids == i, -jnp.inf, vals)
    return jnp.concatenate(tops, axis=0), jnp.concatenate(sel, axis=0)


def _topk_kernel(sc_ref, idx_ref, gate_ref, exp_ref):
    TB = TOK_BLOCK
    K = PEER_TOPK
    iota_keys = lax.broadcasted_iota(I32, (N_KEYS, TB), 0).astype(F32)
    sub = lax.broadcasted_iota(I32, (SUBLANES, TB), 0)
    subf = sub.astype(F32)

    def head(hd, c):
        s1, i1t = _extract_topk(sc_ref[2 * hd], iota_keys, K, N_KEYS)
        s2, i2t = _extract_topk(sc_ref[2 * hd + 1], iota_keys, K, N_KEYS)
        vals, ids = [], []
        for b0 in range(0, K, SUBLANES):
            vals.append(s1[0:1, :] + s2[b0:b0 + SUBLANES, :])
            ids.append(subf + b0)
        for a in range(1, SUBLANES):
            v = s1[a:a + 1, :] + s2[0:SUBLANES, :]
            vals.append(jnp.where(sub < K // (a + 1), v, -jnp.inf))
            ids.append(subf + a * K)
        vals.append(s1[SUBLANES:K, :] + s2[0:1, :])
        ids.append((subf + SUBLANES) * K)
        c_score, c_idx = _extract_topk(jnp.concatenate(vals, axis=0), jnp.concatenate(ids, axis=0), K, K * K)
        c_idx = c_idx.astype(I32)
        i1t = i1t.astype(I32)
        i2t = i2t.astype(I32)
        a_sel = lax.shift_right_logical(c_idx, 4)
        b_sel = c_idx & (K - 1)
        i1 = jnp.zeros((K, TB), I32)
        i2 = jnp.zeros((K, TB), I32)
        for a in range(K):
            i1 = i1 + jnp.where(a_sel == a, i1t[a:a + 1, :], 0)
            i2 = i2 + jnp.where(b_sel == a, i2t[a:a + 1, :], 0)
        e = jnp.exp(c_score - jnp.max(c_score, axis=0, keepdims=True))
        row0 = pl.multiple_of(hd * K, K)
        gate_ref[0, pl.ds(row0, K), :] = e / jnp.sum(e, axis=0, keepdims=True)
        exp_ref[pl.ds(row0, K), :] = (i1 * N_KEYS + i2) * ROW_WORDS
        return c

    lax.fori_loop(0, PEER_HEADS, head, 0)
    half = N_PAIRS // 2
    idx_ref[0] = exp_ref[0:half, :] | (exp_ref[half:N_PAIRS, :] << 16)


def _topk_call(scores_t):
    T = scores_t.shape[-1]
    nblk = T // TOK_BLOCK
    return pl.pallas_call(
        _topk_kernel,
        grid=(nblk,),
        in_specs=[pl.BlockSpec((2 * PEER_HEADS, N_KEYS, TOK_BLOCK), lambda i: (0, 0, i))],
        out_specs=[
            pl.BlockSpec((1, N_PAIRS // 2, TOK_BLOCK), lambda i: (i, 0, 0)),
            pl.BlockSpec((1, N_PAIRS, TOK_BLOCK), lambda i: (i, 0, 0)),
        ],
        out_shape=[
            jax.ShapeDtypeStruct((nblk, N_PAIRS // 2, TOK_BLOCK), I32),
            jax.ShapeDtypeStruct((nblk, N_PAIRS, TOK_BLOCK), F32),
        ],
        scratch_shapes=[pltpu.VMEM((N_PAIRS, TOK_BLOCK), I32)],
        compiler_params=pltpu.CompilerParams(dimension_semantics=("arbitrary",)),
        name="topk",
    )(scores_t)


def _pack_table(tab):
    n = tab.shape[0]
    t = tab.astype(BF16).reshape(n, ROW_WORDS, 2, LANES).transpose(0, 1, 3, 2)
    return lax.bitcast_convert_type(t, I32).reshape(n * ROW_WORDS, LANES)


def _pair_offsets(idx_ref, w, j):
    word = idx_ref[w * TOK_BLOCK + j]
    lo = pl.multiple_of(word & 0xFFFF, ROW_WORDS)
    hi = pl.multiple_of(lax.shift_right_logical(word, 16), ROW_WORDS)
    return ((w, lo), (w + N_PAIRS // 2, hi))


def _gather_rows(idx_ref, tab_ref, g_ref, j):
    for w in range(N_PAIRS // 2):
        for k, off in _pair_offsets(idx_ref, w, j):
            g_ref[pl.ds(k * ROW_WORDS, ROW_WORDS), :] = tab_ref[pl.ds(off, ROW_WORDS), :]


def _two_token_pipeline(consume, gather, g0_ref, g1_ref):
    gather(g0_ref, 0)
    gather(g1_ref, 1)

    def step(i, c):
        j = 2 * i
        consume(g0_ref, j)
        consume(g1_ref, j + 1)
        gather(g0_ref, jnp.minimum(j + 2, TOK_BLOCK - 1))
        gather(g1_ref, jnp.minimum(j + 3, TOK_BLOCK - 1))
        return c

    lax.fori_loop(0, TOK_BLOCK // 2, step, 0)


def _expert_u_kernel(idx_ref, tab_ref, h2_ref, gate_ref, gsum_ref, expand_ref, wrep_ref,
                     g0_ref, g1_ref, xs_ref, araw_ref):
    TB = TOK_BLOCK
    nchunk = D_MODEL // LANES
    group = 2 * LANES // nchunk
    pitch = ARAW_PITCH

    for c in range(nchunk):
        xs_ref[pl.ds(c, TB, stride=nchunk), :] = h2_ref[:, c * LANES:(c + 1) * LANES].astype(F32)

    def dots(g_ref, j):
        xj = xs_ref[pl.ds(pl.multiple_of(j * nchunk, nchunk), nchunk), :].astype(BF16)
        for g in range(N_PAIRS // group):
            wgt = pltpu.bitcast(g_ref[pl.ds(g * LANES, LANES), :], BF16)
            r = lax.dot_general(xj, wgt, NT_DIMS, preferred_element_type=F32)
            for half in range(2):
                cc = 2 * g + half
                araw_ref[pl.ds(cc * nchunk * pitch + j, nchunk, stride=pitch), :] = r[:, half * LANES:(half + 1) * LANES]

    _two_token_pipeline(dots, functools.partial(_gather_rows, idx_ref, tab_ref), g0_ref, g1_ref)

    lane_chunk = lax.broadcasted_iota(I32, (TB, LANES), 1) % nchunk
    a = jnp.zeros((TB, N_PAIRS), F32)
    for cc in range(nchunk):
        part = jnp.zeros((TB, LANES), F32)
        for c in range(nchunk):
            part = part + jnp.where(lane_chunk == c, araw_ref[pl.ds((cc * nchunk + c) * pitch, TB), :], 0.0)
        a = a + _dot_sel_right(part, gsum_ref[cc * LANES:(cc + 1) * LANES, :])
    w = gate_ref[0].T * (0.5 * a * (1.0 + lax.erf(a * (2.0 ** -0.5))))
    wrep_ref[...] = jnp.dot(w.astype(BF16), expand_ref[...], preferred_element_type=F32)


def _expert_u_call(idx_words, tab, h2, gates):
    T = h2.shape[0]
    nblk = T // TOK_BLOCK
    nchunk = D_MODEL // LANES
    pair_of_col = np.arange(D_MODEL) // SUBLANES
    expand = (np.arange(N_PAIRS)[:, None] == pair_of_col[None, :]).astype(np.float32)
    return pl.pallas_call(
        _expert_u_kernel,
        grid=(nblk,),
        in_specs=[
            pl.BlockSpec((N_PAIRS // 2 * TOK_BLOCK,), lambda i: (i,), memory_space=pltpu.SMEM),
            pl.BlockSpec(memory_space=pltpu.VMEM),
            pl.BlockSpec((TOK_BLOCK, D_MODEL), lambda i: (i, 0)),
            pl.BlockSpec((1, N_PAIRS, TOK_BLOCK), lambda i: (i, 0, 0)),
            pl.BlockSpec((D_MODEL, N_PAIRS), lambda i: (0, 0)),
            pl.BlockSpec((N_PAIRS, D_MODEL), lambda i: (0, 0)),
        ],
        out_specs=pl.BlockSpec((TOK_BLOCK, D_MODEL), lambda i: (i, 0)),
        out_shape=jax.ShapeDtypeStruct((T, D_MODEL), F32),
        scratch_shapes=[
            pltpu.VMEM((N_PAIRS * ROW_WORDS, LANES), I32),
            pltpu.VMEM((N_PAIRS * ROW_WORDS, LANES), I32),
            pltpu.VMEM((TOK_BLOCK * nchunk, LANES), F32),
            pltpu.VMEM((nchunk * nchunk * ARAW_PITCH, LANES), F32),
        ],
        compiler_params=pltpu.CompilerParams(
            dimension_semantics=("arbitrary",), vmem_limit_bytes=VMEM_LIMIT_EXPERT),
        name="expert_u",
    )(idx_words, tab, h2, gates, jnp.asarray(expand.T, BF16), jnp.asarray(expand, BF16))


def _expert_v_kernel(idx_ref, tab_ref, wrep_ref, x1_ref, nf_ref, out_ref, g0_ref, g1_ref, y_ref):
    TB = TOK_BLOCK
    nchunk = D_MODEL // LANES
    sub = lax.broadcasted_iota(I32, (nchunk, D_MODEL), 0)
    col = lax.broadcasted_iota(I32, (nchunk, D_MODEL), 1)
    diag = (col % nchunk) == sub

    def weighted_sum(g_ref, j):
        wr = jnp.broadcast_to(wrep_ref[pl.ds(j, 1), :], (nchunk, D_MODEL))
        wm = jnp.where(diag, wr, 0.0).astype(BF16)
        gb = pltpu.bitcast(g_ref[...], BF16)
        y_ref[pl.ds(pl.multiple_of(j * nchunk, nchunk), nchunk), :] = jnp.dot(wm, gb, preferred_element_type=F32)

    _two_token_pipeline(weighted_sum, functools.partial(_gather_rows, idx_ref, tab_ref), g0_ref, g1_ref)

    y = jnp.concatenate([y_ref[pl.ds(c, TB, stride=nchunk), :] for c in range(nchunk)], axis=1)
    out_ref[...] = _rms(x1_ref[...] + y, nf_ref[...])


def _expert_v_call(idx_words, tab, wrep, x1, norm_f):
    T = x1.shape[0]
    nblk = T // TOK_BLOCK
    nchunk = D_MODEL // LANES
    return pl.pallas_call(
        _expert_v_kernel,
        grid=(nblk,),
        in_specs=[
            pl.BlockSpec((N_PAIRS // 2 * TOK_BLOCK,), lambda i: (i,), memory_space=pltpu.SMEM),
            pl.BlockSpec(memory_space=pltpu.VMEM),
            pl.BlockSpec((TOK_BLOCK, D_MODEL), lambda i: (i, 0)),
            pl.BlockSpec((TOK_BLOCK, D_MODEL), lambda i: (i, 0)),
            pl.BlockSpec((1, D_MODEL), lambda i: (0, 0)),
        ],
        out_specs=pl.BlockSpec((TOK_BLOCK, D_MODEL), lambda i: (i, 0)),
        out_shape=jax.ShapeDtypeStruct((T, D_MODEL), F32),
        scratch_shapes=[
            pltpu.VMEM((N_PAIRS * ROW_WORDS, LANES), I32),
            pltpu.VMEM((N_PAIRS * ROW_WORDS, LANES), I32),
            pltpu.VMEM((TOK_BLOCK * nchunk, LANES), F32),
        ],
        compiler_params=pltpu.CompilerParams(
            dimension_semantics=("arbitrary",), vmem_limit_bytes=VMEM_LIMIT_EXPERT),
        name="expert_v",
    )(idx_words, tab, wrep, x1, norm_f.reshape(1, D_MODEL))


def kernel(x, norm_mix, w_in, conv_w, conv_gain, hg_lb_logits, hg_gain, w_out, norm_ffn, peer_wq, peer_keys,
           peer_u, peer_v, norm_f):
    B, S, D = x.shape
    depth = norm_mix.shape[0]
    assert depth == 1 and D == D_MODEL and S % SEQ_BLOCK == 0 and (B * S) % TOK_BLOCK == 0
    T = B * S
    x1, h2, scores_t = _mixer_call(x, norm_mix[0], w_in[0], conv_w[0], conv_gain[0], hg_lb_logits, hg_gain[0],
                                   w_out[0], norm_ffn[0], peer_wq[0], peer_keys[0])
    idx_words, gates = _topk_call(scores_t)
    idx_words = idx_words.reshape(-1)
    wrep = _expert_u_call(idx_words, _pack_table(peer_u[0]), h2.reshape(T, D), gates)
    out = _expert_v_call(idx_words, _pack_table(peer_v[0]), wrep, x1.reshape(T, D), norm_f)
    return out.reshape(B, S, D)
```

```python
import functools

import numpy as np
import jax
import jax.numpy as jnp
from jax import lax
from jax.experimental import pallas as pl
from jax.experimental.pallas import tpu as pltpu

F32 = jnp.float32
BF16 = jnp.bfloat16
I32 = jnp.int32

D_MODEL = 1024
D_CONV = 512
CONV_GROUP = 64
CONV_WIDTH = 3
HG_HEADS = 4
HG_KD = 128
HG_VD = 128
D_HG = HG_HEADS * HG_KD
D_IN = 3 * D_CONV + 4 * D_HG
CHUNK = 32
PEER_HEADS = 8
N_KEYS = 128
HALF_Q = 128
PEER_TOPK = 16
N_PAIRS = PEER_HEADS * PEER_TOPK
EPS = 1e-6

LANES = 128
SUBLANES = 8
SEQ_BLOCK = 256
TOK_BLOCK = 128
ROW_WORDS = D_MODEL // (2 * LANES)
ARAW_PITCH = TOK_BLOCK + 1
N_CHUNKS = 8
VMEM_LIMIT_MIXER = 48 * 1024 * 1024
VMEM_LIMIT_EXPERT = 52 * 1024 * 1024

NT_DIMS = (((1,), (1,)), ((), ()))


def _bdot(a, b):
    return jnp.dot(a.astype(BF16), b.astype(BF16), preferred_element_type=F32)


def _bdot_nt(a, b):
    return lax.dot_general(a.astype(BF16), b.astype(BF16), NT_DIMS, preferred_element_type=F32)


def _split3(a):
    a1 = a.astype(BF16)
    r1 = a - a1.astype(F32)
    a2 = r1.astype(BF16)
    a3 = (r1 - a2.astype(F32)).astype(BF16)
    return a1, a2, a3


def _dot_sel(sel, a):
    out = None
    for piece in _split3(a):
        d = jnp.dot(sel, piece, preferred_element_type=F32)
        out = d if out is None else out + d
    return out


def _dot_sel_right(a, sel):
    out = None
    for piece in _split3(a):
        d = jnp.dot(piece, sel, preferred_element_type=F32)
        out = d if out is None else out + d
    return out


def _silu(x):
    return x * (1.0 / (1.0 + jnp.exp(-x)))


def _sigmoid(x):
    return 1.0 / (1.0 + jnp.exp(-x))


def _rms(x, gain):
    return x * lax.rsqrt(jnp.mean(x * x, axis=-1, keepdims=True) + EPS) * gain


def _mixer_kernel(x_ref, nmix_ref, win_ref, convw_ref, cgain_ref, lbl_ref, hgain_ref, wout_ref,
                  nffn_ref, wq_ref, keys_ref, cum_ref, gmean_ref, hmean_ref,
                  x1_ref, h2_ref, sc_ref,
                  up_ref, st_ref):
    L = SEQ_BLOCK
    sb = pl.program_id(1)

    @pl.when(sb == 0)
    def _():
        up_ref[0:SUBLANES, :] = jnp.zeros((SUBLANES, D_CONV), F32)
        st_ref[...] = jnp.zeros(st_ref.shape, F32)

    x = x_ref[0]
    h = _rms(x, nmix_ref[...])
    proj = _bdot(h, win_ref[...])

    u = proj[:, 0:D_CONV] * proj[:, D_CONV:2 * D_CONV]
    up_ref[SUBLANES:SUBLANES + L, :] = u
    u1 = up_ref[SUBLANES - 1:SUBLANES - 1 + L, :]
    u2 = up_ref[SUBLANES - 2:SUBLANES - 2 + L, :]
    cw = convw_ref[...]
    y = cw[0:1, :] * u2 + cw[1:2, :] * u1 + cw[2:3, :] * u
    y = proj[:, 2 * D_CONV:3 * D_CONV] * y
    up_ref[0:SUBLANES, :] = up_ref[L:L + SUBLANES, :]
    gms = _dot_sel_right(y * y, gmean_ref[...])
    conv_out = y * lax.rsqrt(gms + EPS) * cgain_ref[...]

    o0 = 3 * D_CONV
    qp = proj[:, o0:o0 + D_HG]
    fp = proj[:, o0 + D_HG:o0 + 2 * D_HG]
    vv = proj[:, o0 + 2 * D_HG:o0 + 3 * D_HG]
    gg = proj[:, o0 + 3 * D_HG:o0 + 4 * D_HG]
    lbl = lbl_ref[...]
    lbe = jnp.exp(lbl - jnp.max(lbl, axis=0, keepdims=True))
    lb = lbe[0:1, :] / jnp.sum(lbe, axis=0, keepdims=True)
    qh = _silu(qp) * (HG_KD ** -0.5)
    f = lb + (1.0 - lb) * _sigmoid(fp)
    kh = 1.0 - f
    logf = jnp.log(f)
    cums = _dot_sel(cum_ref[...], logf)
    gcum = cums[0:L, :]
    glast = cums[L:2 * L, :]
    q_dec = qh * jnp.exp(gcum)
    k_dec = kh * jnp.exp(-gcum)
    k_end = kh * jnp.exp(glast - gcum)
    decay = jnp.exp(glast)

    row = lax.broadcasted_iota(I32, (L, L), 0)
    col = lax.broadcasted_iota(I32, (L, L), 1)
    causal = (row // CHUNK == col // CHUNK) & (col <= row)

    o_heads = []
    for hh in range(HG_HEADS):
        cs = slice(hh * HG_KD, (hh + 1) * HG_KD)
        qd, kd, ke, vh, dc = q_dec[:, cs], k_dec[:, cs], k_end[:, cs], vv[:, cs], decay[:, cs]
        a = _bdot_nt(qd, kd)
        a = jnp.where(causal, a, 0.0)
        o_intra = _bdot(a, vh)
        st = st_ref[hh]
        pieces = []
        for n in range(L // CHUNK):
            rs = slice(n * CHUNK, (n + 1) * CHUNK)
            pieces.append(_bdot_nt(qd[rs], st))
            st = dc[n * CHUNK:n * CHUNK + 1, :] * st + _bdot(vh[rs].T, ke[rs])
        st_ref[hh] = st
        o_heads.append(o_intra + jnp.concatenate(pieces, axis=0))
    o = jnp.concatenate(o_heads, axis=1)
    oms = _dot_sel_right(o * o, hmean_ref[...])
    o = o * lax.rsqrt(oms + EPS) * hgain_ref[...]
    o = o * _silu(gg)

    mix = jnp.concatenate([conv_out, o], axis=1)
    x1 = x + _bdot(mix, wout_ref[...])
    x1_ref[0] = x1
    h2 = _rms(x1, nffn_ref[...])
    h2_ref[0] = h2.astype(BF16)
    q2 = _bdot(h2, wq_ref[...])
    for hp in range(2 * PEER_HEADS):
        sc_ref[hp] = _bdot_nt(keys_ref[hp], q2[:, hp * HALF_Q:(hp + 1) * HALF_Q])


def _mixer_call(x, norm_mix, w_in, conv_w, conv_gain, lb_logits, hg_gain, w_out, norm_ffn, wq, keys):
    B, S, _ = x.shape
    L = SEQ_BLOCK
    nsb = S // L
    T = B * S
    r = np.arange(L)
    same = (r[:, None] // CHUNK) == (r[None, :] // CHUNK)
    cum = np.concatenate([same & (r[None, :] <= r[:, None]), same], axis=0).astype(np.float32)
    c = np.arange(D_CONV)
    gmean = ((c[:, None] // CONV_GROUP) == (c[None, :] // CONV_GROUP)).astype(np.float32) / CONV_GROUP
    hmean = ((c[:, None] // HG_VD) == (c[None, :] // HG_VD)).astype(np.float32) / HG_VD

    const = lambda shape: pl.BlockSpec(shape, lambda b, s: (0,) * len(shape))
    return pl.pallas_call(
        _mixer_kernel,
        grid=(B, nsb),
        in_specs=[
            pl.BlockSpec((1, L, D_MODEL), lambda b, s: (b, s, 0)),
            const((1, D_MODEL)), const((D_MODEL, D_IN)), const((CONV_WIDTH, D_CONV)), const((1, D_CONV)),
            const(lb_logits.shape), const((1, D_HG)), const((D_MODEL, D_MODEL)), const((1, D_MODEL)),
            const((D_MODEL, 2 * PEER_HEADS * HALF_Q)), const((2 * PEER_HEADS, N_KEYS, HALF_Q)),
            const((2 * L, L)), const((D_CONV, D_CONV)), const((D_HG, D_HG)),
        ],
        out_specs=[
            pl.BlockSpec((1, L, D_MODEL), lambda b, s: (b, s, 0)),
            pl.BlockSpec((1, L, D_MODEL), lambda b, s: (b, s, 0)),
            pl.BlockSpec((2 * PEER_HEADS, N_KEYS, L), lambda b, s: (0, 0, b * nsb + s)),
        ],
        out_shape=[
            jax.ShapeDtypeStruct((B, S, D_MODEL), F32),
            jax.ShapeDtypeStruct((B, S, D_MODEL), BF16),
            jax.ShapeDtypeStruct((2 * PEER_HEADS, N_KEYS, T), F32),
        ],
        scratch_shapes=[
            pltpu.VMEM((L + 2 * SUBLANES, D_CONV), F32),
            pltpu.VMEM((HG_HEADS, HG_VD, HG_KD), F32),
        ],
        compiler_params=pltpu.CompilerParams(
            dimension_semantics=("arbitrary", "arbitrary"), vmem_limit_bytes=VMEM_LIMIT_MIXER),
        name="mixer",
    )(x, norm_mix.reshape(1, D_MODEL), w_in.astype(BF16), conv_w, conv_gain.reshape(1, D_CONV), lb_logits,
      jnp.tile(hg_gain.reshape(1, HG_VD), (1, HG_HEADS)), w_out.astype(BF16), norm_ffn.reshape(1, D_MODEL),
      wq.astype(BF16), keys.reshape(2 * PEER_HEADS, N_KEYS, HALF_Q).astype(BF16),
      jnp.asarray(cum, BF16), jnp.asarray(gmean, BF16), jnp.asarray(hmean, BF16))


def _extract_topk(vals, ids, n, sentinel):
    tops, sel = [], []
    for _ in range(n):
        m = jnp.max(vals, axis=0, keepdims=True)
        i = jnp.min(jnp.where(vals == m, ids, sentinel), axis=0, keepdims=True)
        tops.append(m)
        sel.append(i)
        vals = jnp.where(ids == i, -jnp.inf, vals)
    return jnp.concatenate(tops, axis=0), jnp.concatenate(sel, axis=0)


def _topk_head(hd, sc_ref, gate_ref, exp_ref):
    TB = TOK_BLOCK
    K = PEER_TOPK
    iota_keys = lax.broadcasted_iota(I32, (N_KEYS, TB), 0).astype(F32)
    sub = lax.broadcasted_iota(I32, (SUBLANES, TB), 0)
    subf = sub.astype(F32)
    s1, i1t = _extract_topk(sc_ref[2 * hd], iota_keys, K, N_KEYS)
    s2, i2t = _extract_topk(sc_ref[2 * hd + 1], iota_keys, K, N_KEYS)
    vals, ids = [], []
    for b0 in range(0, K, SUBLANES):
        vals.append(s1[0:1, :] + s2[b0:b0 + SUBLANES, :])
        ids.append(subf + b0)
    for a in range(1, SUBLANES):
        v = s1[a:a + 1, :] + s2[0:SUBLANES, :]
        vals.append(jnp.where(sub < K // (a + 1), v, -jnp.inf))
        ids.append(subf + a * K)
    vals.append(s1[SUBLANES:K, :] + s2[0:1, :])
    ids.append((subf + SUBLANES) * K)
    c_score, c_idx = _extract_topk(jnp.concatenate(vals, axis=0), jnp.concatenate(ids, axis=0), K, K * K)
    c_idx = c_idx.astype(I32)
    i1t = i1t.astype(I32)
    i2t = i2t.astype(I32)
    a_sel = lax.shift_right_logical(c_idx, 4)
    b_sel = c_idx & (K - 1)
    i1 = jnp.zeros((K, TB), I32)
    i2 = jnp.zeros((K, TB), I32)
    for a in range(K):
        i1 = i1 + jnp.where(a_sel == a, i1t[a:a + 1, :], 0)
        i2 = i2 + jnp.where(b_sel == a, i2t[a:a + 1, :], 0)
    e = jnp.exp(c_score - jnp.max(c_score, axis=0, keepdims=True))
    row0 = pl.multiple_of(hd * K, K)
    gate_ref[0, pl.ds(row0, K), :] = e / jnp.sum(e, axis=0, keepdims=True)
    exp_ref[pl.ds(row0, K), :] = (i1 * N_KEYS + i2) * ROW_WORDS


def _pack_offsets(exp_ref):
    half = N_PAIRS // 2
    return exp_ref[0:half, :] | (exp_ref[half:N_PAIRS, :] << 16)


def _topk_kernel(sc_ref, idx_ref, gate_ref, exp_ref):
    def head(hd, c):
        _topk_head(hd, sc_ref, gate_ref, exp_ref)
        return c

    lax.fori_loop(0, PEER_HEADS, head, 0)
    idx_ref[0] = _pack_offsets(exp_ref)


def _topk_call(scores_t, blk0, nblk):
    return pl.pallas_call(
        _topk_kernel,
        grid=(nblk,),
        in_specs=[pl.BlockSpec((2 * PEER_HEADS, N_KEYS, TOK_BLOCK), lambda i: (0, 0, i + blk0))],
        out_specs=[
            pl.BlockSpec((1, N_PAIRS // 2, TOK_BLOCK), lambda i: (i, 0, 0)),
            pl.BlockSpec((1, N_PAIRS, TOK_BLOCK), lambda i: (i, 0, 0)),
        ],
        out_shape=[
            jax.ShapeDtypeStruct((nblk, N_PAIRS // 2, TOK_BLOCK), I32),
            jax.ShapeDtypeStruct((nblk, N_PAIRS, TOK_BLOCK), F32),
        ],
        scratch_shapes=[pltpu.VMEM((N_PAIRS, TOK_BLOCK), I32)],
        compiler_params=pltpu.CompilerParams(dimension_semantics=("arbitrary",)),
        name="topk",
    )(scores_t)


def _pack_table(tab):
    n = tab.shape[0]
    t = tab.reshape(n, ROW_WORDS, 2, LANES)
    bits = lambda v: lax.bitcast_convert_type(v.astype(BF16).astype(F32), jnp.uint32)
    words = (bits(t[:, :, 0, :]) >> 16) | (bits(t[:, :, 1, :]) & jnp.uint32(0xFFFF0000))
    return lax.bitcast_convert_type(words, I32).reshape(n * ROW_WORDS, LANES)


def _pair_offsets(idx_ref, w, j):
    word = idx_ref[w * TOK_BLOCK + j]
    lo = pl.multiple_of(word & 0xFFFF, ROW_WORDS)
    hi = pl.multiple_of(lax.shift_right_logical(word, 16), ROW_WORDS)
    return ((w, lo), (w + N_PAIRS // 2, hi))


def _gather_rows(idx_ref, tab_ref, g_ref, j):
    for w in range(N_PAIRS // 2):
        for k, off in _pair_offsets(idx_ref, w, j):
            g_ref[pl.ds(k * ROW_WORDS, ROW_WORDS), :] = tab_ref[pl.ds(off, ROW_WORDS), :]


def _two_token_step(consume, gather, g0_ref, g1_ref, j):
    consume(g0_ref, j)
    consume(g1_ref, j + 1)
    gather(g0_ref, jnp.minimum(j + 2, TOK_BLOCK - 1))
    gather(g1_ref, jnp.minimum(j + 3, TOK_BLOCK - 1))


def _two_token_pipeline(consume, gather, g0_ref, g1_ref):
    gather(g0_ref, 0)
    gather(g1_ref, 1)

    def step(i, c):
        _two_token_step(consume, gather, g0_ref, g1_ref, 2 * i)
        return c

    lax.fori_loop(0, TOK_BLOCK // 2, step, 0)


def _expert_u_kernel(idx_ref, tab_ref, h2_ref, gate_ref, gsum_ref, sc_ref,
                     w_ref, idx_next_ref, gate_next_ref,
                     g0_ref, g1_ref, xs_ref, araw_ref, exp_ref):
    TB = TOK_BLOCK
    nchunk = D_MODEL // LANES
    group = 2 * LANES // nchunk
    pitch = ARAW_PITCH

    for c in range(nchunk):
        xs_ref[pl.ds(c, TB, stride=nchunk), :] = h2_ref[:, c * LANES:(c + 1) * LANES].astype(F32)

    def dots(g_ref, j):
        xj = xs_ref[pl.ds(pl.multiple_of(j * nchunk, nchunk), nchunk), :].astype(BF16)
        for g in range(N_PAIRS // group):
            wgt = pltpu.bitcast(g_ref[pl.ds(g * LANES, LANES), :], BF16)
            r = lax.dot_general(xj, wgt, NT_DIMS, preferred_element_type=F32)
            for half in range(2):
                cc = 2 * g + half
                araw_ref[pl.ds(cc * nchunk * pitch + j, nchunk, stride=pitch), :] = r[:, half * LANES:(half + 1) * LANES]

    gather = functools.partial(_gather_rows, idx_ref, tab_ref)
    tokens_per_head = TB // PEER_HEADS
    gather(g0_ref, 0)
    gather(g1_ref, 1)

    def head_step(hd, c):
        _topk_head(hd, sc_ref, gate_next_ref, exp_ref)
        for s in range(tokens_per_head // 2):
            _two_token_step(dots, gather, g0_ref, g1_ref, hd * tokens_per_head + 2 * s)
        return c

    lax.fori_loop(0, PEER_HEADS, head_step, 0)
    idx_next_ref[0] = _pack_offsets(exp_ref)

    lane_chunk = lax.broadcasted_iota(I32, (TB, LANES), 1) % nchunk
    a = jnp.zeros((TB, N_PAIRS), F32)
    for cc in range(nchunk):
        part = jnp.zeros((TB, LANES), F32)
        for c in range(nchunk):
            part = part + jnp.where(lane_chunk == c, araw_ref[pl.ds((cc * nchunk + c) * pitch, TB), :], 0.0)
        a = a + _dot_sel_right(part, gsum_ref[cc * LANES:(cc + 1) * LANES, :])
    w_ref[...] = gate_ref[0].T * (0.5 * a * (1.0 + lax.erf(a * (2.0 ** -0.5))))


def _pair_expand():
    pair_of_col = np.arange(D_MODEL) // SUBLANES
    return (np.arange(N_PAIRS)[:, None] == pair_of_col[None, :]).astype(np.float32)


def _expert_u_call(idx_words, tab, h2, gates, scores_t, blk0, nblk, next_blk0):
    nchunk = D_MODEL // LANES
    return pl.pallas_call(
        _expert_u_kernel,
        grid=(nblk,),
        in_specs=[
            pl.BlockSpec((N_PAIRS // 2 * TOK_BLOCK,), lambda i: (i,), memory_space=pltpu.SMEM),
            pl.BlockSpec(memory_space=pltpu.VMEM),
            pl.BlockSpec((TOK_BLOCK, D_MODEL), lambda i: (i + blk0, 0)),
            pl.BlockSpec((1, N_PAIRS, TOK_BLOCK), lambda i: (i, 0, 0)),
            pl.BlockSpec((D_MODEL, N_PAIRS), lambda i: (0, 0)),
            pl.BlockSpec((2 * PEER_HEADS, N_KEYS, TOK_BLOCK), lambda i: (0, 0, i + next_blk0)),
        ],
        out_specs=[
            pl.BlockSpec((TOK_BLOCK, N_PAIRS), lambda i: (i, 0)),
            pl.BlockSpec((1, N_PAIRS // 2, TOK_BLOCK), lambda i: (i, 0, 0)),
            pl.BlockSpec((1, N_PAIRS, TOK_BLOCK), lambda i: (i, 0, 0)),
        ],
        out_shape=[
            jax.ShapeDtypeStruct((nblk * TOK_BLOCK, N_PAIRS), F32),
            jax.ShapeDtypeStruct((nblk, N_PAIRS // 2, TOK_BLOCK), I32),
            jax.ShapeDtypeStruct((nblk, N_PAIRS, TOK_BLOCK), F32),
        ],
        scratch_shapes=[
            pltpu.VMEM((N_PAIRS * ROW_WORDS, LANES), I32),
            pltpu.VMEM((N_PAIRS * ROW_WORDS, LANES), I32),
            pltpu.VMEM((TOK_BLOCK * nchunk, LANES), F32),
            pltpu.VMEM((nchunk * nchunk * ARAW_PITCH, LANES), F32),
            pltpu.VMEM((N_PAIRS, TOK_BLOCK), I32),
        ],
        compiler_params=pltpu.CompilerParams(
            dimension_semantics=("arbitrary",), vmem_limit_bytes=VMEM_LIMIT_EXPERT),
        name="expert_u",
    )(idx_words, tab, h2, gates, jnp.asarray(_pair_expand().T, BF16), scores_t)


def _expert_v_kernel(idx_ref, tab_ref, w_ref, expand_ref, x1_ref, nf_ref, out_ref, g0_ref, g1_ref, y_ref, wrep_ref):
    TB = TOK_BLOCK
    nchunk = D_MODEL // LANES
    wrep_ref[...] = jnp.dot(w_ref[...].astype(BF16), expand_ref[...], preferred_element_type=F32)
    sub = lax.broadcasted_iota(I32, (nchunk, D_MODEL), 0)
    col = lax.broadcasted_iota(I32, (nchunk, D_MODEL), 1)
    diag = (col % nchunk) == sub

    def weighted_sum(g_ref, j):
        wr = jnp.broadcast_to(wrep_ref[pl.ds(j, 1), :], (nchunk, D_MODEL))
        wm = jnp.where(diag, wr, 0.0).astype(BF16)
        gb = pltpu.bitcast(g_ref[...], BF16)
        y_ref[pl.ds(pl.multiple_of(j * nchunk, nchunk), nchunk), :] = jnp.dot(wm, gb, preferred_element_type=F32)

    _two_token_pipeline(weighted_sum, functools.partial(_gather_rows, idx_ref, tab_ref), g0_ref, g1_ref)

    y = jnp.concatenate([y_ref[pl.ds(c, TB, stride=nchunk), :] for c in range(nchunk)], axis=1)
    out_ref[...] = _rms(x1_ref[...] + y, nf_ref[...])


def _expert_v_call(idx_words, tab, w, x1, norm_f):
    T = x1.shape[0]
    nblk = T // TOK_BLOCK
    nchunk = D_MODEL // LANES
    return pl.pallas_call(
        _expert_v_kernel,
        grid=(nblk,),
        in_specs=[
            pl.BlockSpec((N_PAIRS // 2 * TOK_BLOCK,), lambda i: (i,), memory_space=pltpu.SMEM),
            pl.BlockSpec(memory_space=pltpu.VMEM),
            pl.BlockSpec((TOK_BLOCK, N_PAIRS), lambda i: (i, 0)),
            pl.BlockSpec((N_PAIRS, D_MODEL), lambda i: (0, 0)),
            pl.BlockSpec((TOK_BLOCK, D_MODEL), lambda i: (i, 0)),
            pl.BlockSpec((1, D_MODEL), lambda i: (0, 0)),
        ],
        out_specs=pl.BlockSpec((TOK_BLOCK, D_MODEL), lambda i: (i, 0)),
        out_shape=jax.ShapeDtypeStruct((T, D_MODEL), F32),
        scratch_shapes=[
            pltpu.VMEM((N_PAIRS * ROW_WORDS, LANES), I32),
            pltpu.VMEM((N_PAIRS * ROW_WORDS, LANES), I32),
            pltpu.VMEM((TOK_BLOCK * nchunk, LANES), F32),
            pltpu.VMEM((TOK_BLOCK, D_MODEL), F32),
        ],
        compiler_params=pltpu.CompilerParams(
            dimension_semantics=("arbitrary",), vmem_limit_bytes=VMEM_LIMIT_EXPERT),
        name="expert_v",
    )(idx_words, tab, w, jnp.asarray(_pair_expand(), BF16), x1, norm_f.reshape(1, D_MODEL))


def kernel(x, norm_mix, w_in, conv_w, conv_gain, hg_lb_logits, hg_gain, w_out, norm_ffn, peer_wq, peer_keys,
           peer_u, peer_v, norm_f):
    B, S, D = x.shape
    depth = norm_mix.shape[0]
    assert depth == 1 and D == D_MODEL and S % SEQ_BLOCK == 0 and (B * S) % (TOK_BLOCK * N_CHUNKS) == 0
    T = B * S
    nb = T // (TOK_BLOCK * N_CHUNKS)
    x1, h2, scores_t = _mixer_call(x, norm_mix[0], w_in[0], conv_w[0], conv_gain[0], hg_lb_logits, hg_gain[0],
                                   w_out[0], norm_ffn[0], peer_wq[0], peer_keys[0])
    tab_u = _pack_table(peer_u[0])
    h2 = h2.reshape(T, D)
    idx_k, gates_k = _topk_call(scores_t, 0, nb)
    idx_chunks, w_chunks = [], []
    for k in range(N_CHUNKS):
        nxt = min(k + 1, N_CHUNKS - 1)
        idx_chunks.append(idx_k)
        w_k, idx_k, gates_k = _expert_u_call(idx_k.reshape(-1), tab_u, h2, gates_k, scores_t,
                                             k * nb, nb, nxt * nb)
        w_chunks.append(w_k)
    idx_words = jnp.concatenate(idx_chunks, axis=0).reshape(-1)
    w = jnp.concatenate(w_chunks, axis=0)
    out = _expert_v_call(idx_words, _pack_table(peer_v[0]), w, x1.reshape(T, D), norm_f)
    return out.reshape(B, S, D)
```

```python
import functools

import numpy as np
import jax
import jax.numpy as jnp
from jax import lax
from jax.experimental import pallas as pl
from jax.experimental.pallas import tpu as pltpu

F32 = jnp.float32
BF16 = jnp.bfloat16
I32 = jnp.int32

D_MODEL = 1024
D_CONV = 512
CONV_GROUP = 64
CONV_WIDTH = 3
HG_HEADS = 4
HG_KD = 128
HG_VD = 128
D_HG = HG_HEADS * HG_KD
D_IN = 3 * D_CONV + 4 * D_HG
CHUNK = 32
PEER_HEADS = 8
N_KEYS = 128
HALF_Q = 128
PEER_TOPK = 16
N_PAIRS = PEER_HEADS * PEER_TOPK
EPS = 1e-6

LANES = 128
SUBLANES = 8
SEQ_BLOCK = 256
TOK_BLOCK = 128
ROW_WORDS = D_MODEL // (2 * LANES)
ARAW_PITCH = TOK_BLOCK + 1
N_CHUNKS = 16
PACK_ROWS = 512
VMEM_LIMIT_MIXER = 48 * 1024 * 1024
VMEM_LIMIT_EXPERT = 52 * 1024 * 1024

NT_DIMS = (((1,), (1,)), ((), ()))


def _bdot(a, b):
    return jnp.dot(a.astype(BF16), b.astype(BF16), preferred_element_type=F32)


def _bdot_nt(a, b):
    return lax.dot_general(a.astype(BF16), b.astype(BF16), NT_DIMS, preferred_element_type=F32)


def _split3(a):
    a1 = a.astype(BF16)
    r1 = a - a1.astype(F32)
    a2 = r1.astype(BF16)
    a3 = (r1 - a2.astype(F32)).astype(BF16)
    return a1, a2, a3


def _dot_sel(sel, a):
    out = None
    for piece in _split3(a):
        d = jnp.dot(sel, piece, preferred_element_type=F32)
        out = d if out is None else out + d
    return out


def _dot_sel_right(a, sel):
    out = None
    for piece in _split3(a):
        d = jnp.dot(piece, sel, preferred_element_type=F32)
        out = d if out is None else out + d
    return out


def _silu(x):
    return x * (1.0 / (1.0 + jnp.exp(-x)))


def _sigmoid(x):
    return 1.0 / (1.0 + jnp.exp(-x))


def _rms(x, gain):
    return x * lax.rsqrt(jnp.mean(x * x, axis=-1, keepdims=True) + EPS) * gain


def _mixer_kernel(x_ref, nmix_ref, win_ref, convw_ref, cgain_ref, lbl_ref, hgain_ref, wout_ref,
                  nffn_ref, wq_ref, keys_ref, cum_ref, gmean_ref, hmean_ref,
                  x1_ref, h2_ref, sc_ref,
                  up_ref, st_ref):
    L = SEQ_BLOCK
    sb = pl.program_id(1)

    @pl.when(sb == 0)
    def _():
        up_ref[0:SUBLANES, :] = jnp.zeros((SUBLANES, D_CONV), F32)
        st_ref[...] = jnp.zeros(st_ref.shape, F32)

    x = x_ref[0]
    h = _rms(x, nmix_ref[...])
    proj = _bdot(h, win_ref[...])

    u = proj[:, 0:D_CONV] * proj[:, D_CONV:2 * D_CONV]
    up_ref[SUBLANES:SUBLANES + L, :] = u
    u1 = up_ref[SUBLANES - 1:SUBLANES - 1 + L, :]
    u2 = up_ref[SUBLANES - 2:SUBLANES - 2 + L, :]
    cw = convw_ref[...]
    y = cw[0:1, :] * u2 + cw[1:2, :] * u1 + cw[2:3, :] * u
    y = proj[:, 2 * D_CONV:3 * D_CONV] * y
    up_ref[0:SUBLANES, :] = up_ref[L:L + SUBLANES, :]
    gms = _dot_sel_right(y * y, gmean_ref[...])
    conv_out = y * lax.rsqrt(gms + EPS) * cgain_ref[...]

    o0 = 3 * D_CONV
    qp = proj[:, o0:o0 + D_HG]
    fp = proj[:, o0 + D_HG:o0 + 2 * D_HG]
    vv = proj[:, o0 + 2 * D_HG:o0 + 3 * D_HG]
    gg = proj[:, o0 + 3 * D_HG:o0 + 4 * D_HG]
    lbl = lbl_ref[...]
    lbe = jnp.exp(lbl - jnp.max(lbl, axis=0, keepdims=True))
    lb = lbe[0:1, :] / jnp.sum(lbe, axis=0, keepdims=True)
    qh = _silu(qp) * (HG_KD ** -0.5)
    f = lb + (1.0 - lb) * _sigmoid(fp)
    kh = 1.0 - f
    logf = jnp.log(f)
    cums = _dot_sel(cum_ref[...], logf)
    gcum = cums[0:L, :]
    glast = cums[L:2 * L, :]
    q_dec = qh * jnp.exp(gcum)
    k_dec = kh * jnp.exp(-gcum)
    k_end = kh * jnp.exp(glast - gcum)
    decay = jnp.exp(glast)

    row = lax.broadcasted_iota(I32, (L, L), 0)
    col = lax.broadcasted_iota(I32, (L, L), 1)
    causal = (row // CHUNK == col // CHUNK) & (col <= row)

    o_heads = []
    for hh in range(HG_HEADS):
        cs = slice(hh * HG_KD, (hh + 1) * HG_KD)
        qd, kd, ke, vh, dc = q_dec[:, cs], k_dec[:, cs], k_end[:, cs], vv[:, cs], decay[:, cs]
        a = _bdot_nt(qd, kd)
        a = jnp.where(causal, a, 0.0)
        o_intra = _bdot(a, vh)
        st = st_ref[hh]
        pieces = []
        for n in range(L // CHUNK):
            rs = slice(n * CHUNK, (n + 1) * CHUNK)
            pieces.append(_bdot_nt(qd[rs], st))
            st = dc[n * CHUNK:n * CHUNK + 1, :] * st + _bdot(vh[rs].T, ke[rs])
        st_ref[hh] = st
        o_heads.append(o_intra + jnp.concatenate(pieces, axis=0))
    o = jnp.concatenate(o_heads, axis=1)
    oms = _dot_sel_right(o * o, hmean_ref[...])
    o = o * lax.rsqrt(oms + EPS) * hgain_ref[...]
    o = o * _silu(gg)

    mix = jnp.concatenate([conv_out, o], axis=1)
    x1 = x + _bdot(mix, wout_ref[...])
    x1_ref[0] = x1
    h2 = _rms(x1, nffn_ref[...])
    h2_ref[0] = h2.astype(BF16)
    q2 = _bdot(h2, wq_ref[...])
    for hp in range(2 * PEER_HEADS):
        sc_ref[hp] = _bdot_nt(keys_ref[hp], q2[:, hp * HALF_Q:(hp + 1) * HALF_Q])


def _mixer_call(x, norm_mix, w_in, conv_w, conv_gain, lb_logits, hg_gain, w_out, norm_ffn, wq, keys):
    B, S, _ = x.shape
    L = SEQ_BLOCK
    nsb = S // L
    T = B * S
    r = np.arange(L)
    same = (r[:, None] // CHUNK) == (r[None, :] // CHUNK)
    cum = np.concatenate([same & (r[None, :] <= r[:, None]), same], axis=0).astype(np.float32)
    c = np.arange(D_CONV)
    gmean = ((c[:, None] // CONV_GROUP) == (c[None, :] // CONV_GROUP)).astype(np.float32) / CONV_GROUP
    hmean = ((c[:, None] // HG_VD) == (c[None, :] // HG_VD)).astype(np.float32) / HG_VD

    const = lambda shape: pl.BlockSpec(shape, lambda b, s: (0,) * len(shape))
    return pl.pallas_call(
        _mixer_kernel,
        grid=(B, nsb),
        in_specs=[
            pl.BlockSpec((1, L, D_MODEL), lambda b, s: (b, s, 0)),
            const((1, D_MODEL)), const((D_MODEL, D_IN)), const((CONV_WIDTH, D_CONV)), const((1, D_CONV)),
            const(lb_logits.shape), const((1, D_HG)), const((D_MODEL, D_MODEL)), const((1, D_MODEL)),
            const((D_MODEL, 2 * PEER_HEADS * HALF_Q)), const((2 * PEER_HEADS, N_KEYS, HALF_Q)),
            const((2 * L, L)), const((D_CONV, D_CONV)), const((D_HG, D_HG)),
        ],
        out_specs=[
            pl.BlockSpec((1, L, D_MODEL), lambda b, s: (b, s, 0)),
            pl.BlockSpec((1, L, D_MODEL), lambda b, s: (b, s, 0)),
            pl.BlockSpec((2 * PEER_HEADS, N_KEYS, L), lambda b, s: (0, 0, b * nsb + s)),
        ],
        out_shape=[
            jax.ShapeDtypeStruct((B, S, D_MODEL), F32),
            jax.ShapeDtypeStruct((B, S, D_MODEL), BF16),
            jax.ShapeDtypeStruct((2 * PEER_HEADS, N_KEYS, T), F32),
        ],
        scratch_shapes=[
            pltpu.VMEM((L + 2 * SUBLANES, D_CONV), F32),
            pltpu.VMEM((HG_HEADS, HG_VD, HG_KD), F32),
        ],
        compiler_params=pltpu.CompilerParams(
            dimension_semantics=("arbitrary", "arbitrary"), vmem_limit_bytes=VMEM_LIMIT_MIXER),
        name="mixer",
    )(x, norm_mix.reshape(1, D_MODEL), w_in.astype(BF16), conv_w, conv_gain.reshape(1, D_CONV), lb_logits,
      jnp.tile(hg_gain.reshape(1, HG_VD), (1, HG_HEADS)), w_out.astype(BF16), norm_ffn.reshape(1, D_MODEL),
      wq.astype(BF16), keys.reshape(2 * PEER_HEADS, N_KEYS, HALF_Q).astype(BF16),
      jnp.asarray(cum, BF16), jnp.asarray(gmean, BF16), jnp.asarray(hmean, BF16))


def _extract_topk(vals, ids, n, sentinel):
    tops, sel = [], []
    for _ in range(n):
        m = jnp.max(vals, axis=0, keepdims=True)
        i = jnp.min(jnp.where(vals == m, ids, sentinel), axis=0, keepdims=True)
        tops.append(m)
        sel.append(i)
        vals = jnp.where(ids == i, -jnp.inf, vals)
    return jnp.concatenate(tops, axis=0), jnp.concatenate(sel, axis=0)


def _topk_head(hd, sc_ref, gate_ref, exp_ref):
    TB = TOK_BLOCK
    K = PEER_TOPK
    iota_keys = lax.broadcasted_iota(I32, (N_KEYS, TB), 0).astype(F32)
    sub = lax.broadcasted_iota(I32, (SUBLANES, TB), 0)
    subf = sub.astype(F32)
    s1, i1t = _extract_topk(sc_ref[2 * hd], iota_keys, K, N_KEYS)
    s2, i2t = _extract_topk(sc_ref[2 * hd + 1], iota_keys, K, N_KEYS)
    vals, ids = [], []
    for b0 in range(0, K, SUBLANES):
        vals.append(s1[0:1, :] + s2[b0:b0 + SUBLANES, :])
        ids.append(subf + b0)
    for a in range(1, SUBLANES):
        v = s1[a:a + 1, :] + s2[0:SUBLANES, :]
        vals.append(jnp.where(sub < K // (a + 1), v, -jnp.inf))
        ids.append(subf + a * K)
    vals.append(s1[SUBLANES:K, :] + s2[0:1, :])
    ids.append((subf + SUBLANES) * K)
    c_score, c_idx = _extract_topk(jnp.concatenate(vals, axis=0), jnp.concatenate(ids, axis=0), K, K * K)
    c_idx = c_idx.astype(I32)
    i1t = i1t.astype(I32)
    i2t = i2t.astype(I32)
    a_sel = lax.shift_right_logical(c_idx, 4)
    b_sel = c_idx & (K - 1)
    i1 = jnp.zeros((K, TB), I32)
    i2 = jnp.zeros((K, TB), I32)
    for a in range(K):
        i1 = i1 + jnp.where(a_sel == a, i1t[a:a + 1, :], 0)
        i2 = i2 + jnp.where(b_sel == a, i2t[a:a + 1, :], 0)
    e = jnp.exp(c_score - jnp.max(c_score, axis=0, keepdims=True))
    row0 = pl.multiple_of(hd * K, K)
    gate_ref[0, pl.ds(row0, K), :] = e / jnp.sum(e, axis=0, keepdims=True)
    exp_ref[pl.ds(row0, K), :] = (i1 * N_KEYS + i2) * ROW_WORDS


def _pack_offsets(exp_ref):
    half = N_PAIRS // 2
    return exp_ref[0:half, :] | (exp_ref[half:N_PAIRS, :] << 16)


def _topk_kernel(sc_ref, idx_ref, gate_ref, exp_ref):
    def head(hd, c):
        _topk_head(hd, sc_ref, gate_ref, exp_ref)
        return c

    lax.fori_loop(0, PEER_HEADS, head, 0)
    idx_ref[0] = _pack_offsets(exp_ref)


def _topk_call(scores_t, blk0, nblk):
    return pl.pallas_call(
        _topk_kernel,
        grid=(nblk,),
        in_specs=[pl.BlockSpec((2 * PEER_HEADS, N_KEYS, TOK_BLOCK), lambda i: (0, 0, i + blk0))],
        out_specs=[
            pl.BlockSpec((1, N_PAIRS // 2, TOK_BLOCK), lambda i: (i, 0, 0)),
            pl.BlockSpec((1, N_PAIRS, TOK_BLOCK), lambda i: (i, 0, 0)),
        ],
        out_shape=[
            jax.ShapeDtypeStruct((nblk, N_PAIRS // 2, TOK_BLOCK), I32),
            jax.ShapeDtypeStruct((nblk, N_PAIRS, TOK_BLOCK), F32),
        ],
        scratch_shapes=[pltpu.VMEM((N_PAIRS, TOK_BLOCK), I32)],
        compiler_params=pltpu.CompilerParams(dimension_semantics=("arbitrary",)),
        name="topk",
    )(scores_t)


def _pack_kernel(tab_ref, out_ref):
    rows = tab_ref.shape[0]
    bits = lambda v: lax.bitcast_convert_type(v.astype(BF16).astype(F32), jnp.uint32)
    for i in range(ROW_WORDS):
        lo = bits(tab_ref[:, (2 * i) * LANES:(2 * i + 1) * LANES]) >> 16
        hi = bits(tab_ref[:, (2 * i + 1) * LANES:(2 * i + 2) * LANES]) & jnp.uint32(0xFFFF0000)
        out_ref[pl.ds(i, rows, stride=ROW_WORDS), :] = lax.bitcast_convert_type(lo | hi, I32)


def _pack_table(tab):
    n = tab.shape[0]
    return pl.pallas_call(
        _pack_kernel,
        grid=(n // PACK_ROWS,),
        in_specs=[pl.BlockSpec((PACK_ROWS, D_MODEL), lambda i: (i, 0))],
        out_specs=pl.BlockSpec((PACK_ROWS * ROW_WORDS, LANES), lambda i: (i, 0)),
        out_shape=jax.ShapeDtypeStruct((n * ROW_WORDS, LANES), I32),
        compiler_params=pltpu.CompilerParams(dimension_semantics=("arbitrary",)),
        name="pack_table",
    )(tab)


def _load_table(tab_hbm, tab_ref, sem):
    @pl.when(pl.program_id(0) == 0)
    def _():
        copy = pltpu.make_async_copy(tab_hbm, tab_ref, sem)
        copy.start()
        copy.wait()


def _pair_offsets(idx_ref, w, j):
    word = idx_ref[w * TOK_BLOCK + j]
    lo = pl.multiple_of(word & 0xFFFF, ROW_WORDS)
    hi = pl.multiple_of(lax.shift_right_logical(word, 16), ROW_WORDS)
    return ((w, lo), (w + N_PAIRS // 2, hi))


def _gather_rows(idx_ref, tab_ref, g_ref, j):
    for w in range(N_PAIRS // 2):
        for k, off in _pair_offsets(idx_ref, w, j):
            g_ref[pl.ds(k * ROW_WORDS, ROW_WORDS), :] = tab_ref[pl.ds(off, ROW_WORDS), :]


def _two_token_step(consume, gather, g0_ref, g1_ref, j):
    consume(g0_ref, j)
    consume(g1_ref, j + 1)
    gather(g0_ref, jnp.minimum(j + 2, TOK_BLOCK - 1))
    gather(g1_ref, jnp.minimum(j + 3, TOK_BLOCK - 1))


def _two_token_pipeline(consume, gather, g0_ref, g1_ref):
    gather(g0_ref, 0)
    gather(g1_ref, 1)

    def step(i, c):
        _two_token_step(consume, gather, g0_ref, g1_ref, 2 * i)
        return c

    lax.fori_loop(0, TOK_BLOCK // 2, step, 0)


def _expert_u_kernel(idx_ref, tab_hbm, h2_ref, gate_ref, gsum_ref, sc_ref,
                     w_ref, idx_next_ref, gate_next_ref,
                     tab_ref, tab_sem, g0_ref, g1_ref, xs_ref, araw_ref, exp_ref):
    TB = TOK_BLOCK
    nchunk = D_MODEL // LANES
    group = 2 * LANES // nchunk
    pitch = ARAW_PITCH
    _load_table(tab_hbm, tab_ref, tab_sem)

    for c in range(nchunk):
        xs_ref[pl.ds(c, TB, stride=nchunk), :] = h2_ref[:, c * LANES:(c + 1) * LANES].astype(F32)

    def dots(g_ref, j):
        xj = xs_ref[pl.ds(pl.multiple_of(j * nchunk, nchunk), nchunk), :].astype(BF16)
        for g in range(N_PAIRS // group):
            wgt = pltpu.bitcast(g_ref[pl.ds(g * LANES, LANES), :], BF16)
            r = lax.dot_general(xj, wgt, NT_DIMS, preferred_element_type=F32)
            for half in range(2):
                cc = 2 * g + half
                araw_ref[pl.ds(cc * nchunk * pitch + j, nchunk, stride=pitch), :] = r[:, half * LANES:(half + 1) * LANES]

    gather = functools.partial(_gather_rows, idx_ref, tab_ref)
    tokens_per_head = TB // PEER_HEADS
    gather(g0_ref, 0)
    gather(g1_ref, 1)

    def head_step(hd, c):
        _topk_head(hd, sc_ref, gate_next_ref, exp_ref)
        for s in range(tokens_per_head // 2):
            _two_token_step(dots, gather, g0_ref, g1_ref, hd * tokens_per_head + 2 * s)
        return c

    lax.fori_loop(0, PEER_HEADS, head_step, 0)
    idx_next_ref[0] = _pack_offsets(exp_ref)

    lane_chunk = lax.broadcasted_iota(I32, (TB, LANES), 1) % nchunk
    a = jnp.zeros((TB, N_PAIRS), F32)
    for cc in range(nchunk):
        part = jnp.zeros((TB, LANES), F32)
        for c in range(nchunk):
            part = part + jnp.where(lane_chunk == c, araw_ref[pl.ds((cc * nchunk + c) * pitch, TB), :], 0.0)
        a = a + _dot_sel_right(part, gsum_ref[cc * LANES:(cc + 1) * LANES, :])
    w_ref[...] = gate_ref[0].T * (0.5 * a * (1.0 + lax.erf(a * (2.0 ** -0.5))))


def _pair_expand():
    pair_of_col = np.arange(D_MODEL) // SUBLANES
    return (np.arange(N_PAIRS)[:, None] == pair_of_col[None, :]).astype(np.float32)


def _expert_u_call(idx_words, tab, h2, gates, scores_t, blk0, nblk, next_blk0):
    nchunk = D_MODEL // LANES
    return pl.pallas_call(
        _expert_u_kernel,
        grid=(nblk,),
        in_specs=[
            pl.BlockSpec((N_PAIRS // 2 * TOK_BLOCK,), lambda i: (i,), memory_space=pltpu.SMEM),
            pl.BlockSpec(memory_space=pl.ANY),
            pl.BlockSpec((TOK_BLOCK, D_MODEL), lambda i: (i + blk0, 0)),
            pl.BlockSpec((1, N_PAIRS, TOK_BLOCK), lambda i: (i, 0, 0)),
            pl.BlockSpec((D_MODEL, N_PAIRS), lambda i: (0, 0)),
            pl.BlockSpec((2 * PEER_HEADS, N_KEYS, TOK_BLOCK), lambda i: (0, 0, i + next_blk0)),
        ],
        out_specs=[
            pl.BlockSpec((TOK_BLOCK, N_PAIRS), lambda i: (i, 0)),
            pl.BlockSpec((1, N_PAIRS // 2, TOK_BLOCK), lambda i: (i, 0, 0)),
            pl.BlockSpec((1, N_PAIRS, TOK_BLOCK), lambda i: (i, 0, 0)),
        ],
        out_shape=[
            jax.ShapeDtypeStruct((nblk * TOK_BLOCK, N_PAIRS), F32),
            jax.ShapeDtypeStruct((nblk, N_PAIRS // 2, TOK_BLOCK), I32),
            jax.ShapeDtypeStruct((nblk, N_PAIRS, TOK_BLOCK), F32),
        ],
        scratch_shapes=[
            pltpu.VMEM(tab.shape, I32),
            pltpu.SemaphoreType.DMA,
            pltpu.VMEM((N_PAIRS * ROW_WORDS, LANES), I32),
            pltpu.VMEM((N_PAIRS * ROW_WORDS, LANES), I32),
            pltpu.VMEM((TOK_BLOCK * nchunk, LANES), F32),
            pltpu.VMEM((nchunk * nchunk * ARAW_PITCH, LANES), F32),
            pltpu.VMEM((N_PAIRS, TOK_BLOCK), I32),
        ],
        compiler_params=pltpu.CompilerParams(
            dimension_semantics=("arbitrary",), vmem_limit_bytes=VMEM_LIMIT_EXPERT),
        name="expert_u",
    )(idx_words, tab, h2, gates, jnp.asarray(_pair_expand().T, BF16), scores_t)


def _expert_v_kernel(idx_ref, tab_hbm, w_ref, expand_ref, x1_ref, nf_ref, out_ref,
                     tab_ref, tab_sem, g0_ref, g1_ref, y_ref, wrep_ref):
    TB = TOK_BLOCK
    nchunk = D_MODEL // LANES
    _load_table(tab_hbm, tab_ref, tab_sem)
    wrep_ref[...] = jnp.dot(w_ref[...].astype(BF16), expand_ref[...], preferred_element_type=F32)
    sub = lax.broadcasted_iota(I32, (nchunk, D_MODEL), 0)
    col = lax.broadcasted_iota(I32, (nchunk, D_MODEL), 1)
    diag = (col % nchunk) == sub

    def weighted_sum(g_ref, j):
        wr = jnp.broadcast_to(wrep_ref[pl.ds(j, 1), :], (nchunk, D_MODEL))
        wm = jnp.where(diag, wr, 0.0).astype(BF16)
        gb = pltpu.bitcast(g_ref[...], BF16)
        y_ref[pl.ds(pl.multiple_of(j * nchunk, nchunk), nchunk), :] = jnp.dot(wm, gb, preferred_element_type=F32)

    _two_token_pipeline(weighted_sum, functools.partial(_gather_rows, idx_ref, tab_ref), g0_ref, g1_ref)

    y = jnp.concatenate([y_ref[pl.ds(c, TB, stride=nchunk), :] for c in range(nchunk)], axis=1)
    out_ref[...] = _rms(x1_ref[...] + y, nf_ref[...])


def _expert_v_call(idx_words, tab, w, x1, norm_f):
    T = x1.shape[0]
    nblk = T // TOK_BLOCK
    nchunk = D_MODEL // LANES
    return pl.pallas_call(
        _expert_v_kernel,
        grid=(nblk,),
        in_specs=[
            pl.BlockSpec((N_PAIRS // 2 * TOK_BLOCK,), lambda i: (i,), memory_space=pltpu.SMEM),
            pl.BlockSpec(memory_space=pl.ANY),
            pl.BlockSpec((TOK_BLOCK, N_PAIRS), lambda i: (i, 0)),
            pl.BlockSpec((N_PAIRS, D_MODEL), lambda i: (0, 0)),
            pl.BlockSpec((TOK_BLOCK, D_MODEL), lambda i: (i, 0)),
            pl.BlockSpec((1, D_MODEL), lambda i: (0, 0)),
        ],
        out_specs=pl.BlockSpec((TOK_BLOCK, D_MODEL), lambda i: (i, 0)),
        out_shape=jax.ShapeDtypeStruct((T, D_MODEL), F32),
        scratch_shapes=[
            pltpu.VMEM(tab.shape, I32),
            pltpu.SemaphoreType.DMA,
            pltpu.VMEM((N_PAIRS * ROW_WORDS, LANES), I32),
            pltpu.VMEM((N_PAIRS * ROW_WORDS, LANES), I32),
            pltpu.VMEM((TOK_BLOCK * nchunk, LANES), F32),
            pltpu.VMEM((TOK_BLOCK, D_MODEL), F32),
        ],
        compiler_params=pltpu.CompilerParams(
            dimension_semantics=("arbitrary",), vmem_limit_bytes=VMEM_LIMIT_EXPERT),
        name="expert_v",
    )(idx_words, tab, w, jnp.asarray(_pair_expand(), BF16), x1, norm_f.reshape(1, D_MODEL))


def kernel(x, norm_mix, w_in, conv_w, conv_gain, hg_lb_logits, hg_gain, w_out, norm_ffn, peer_wq, peer_keys,
           peer_u, peer_v, norm_f):
    B, S, D = x.shape
    depth = norm_mix.shape[0]
    assert depth == 1 and D == D_MODEL and S % SEQ_BLOCK == 0 and (B * S) % (TOK_BLOCK * N_CHUNKS) == 0
    T = B * S
    nb = T // (TOK_BLOCK * N_CHUNKS)
    x1, h2, scores_t = _mixer_call(x, norm_mix[0], w_in[0], conv_w[0], conv_gain[0], hg_lb_logits, hg_gain[0],
                                   w_out[0], norm_ffn[0], peer_wq[0], peer_keys[0])
    tab_u = _pack_table(peer_u[0])
    h2 = h2.reshape(T, D)
    idx_k, gates_k = _topk_call(scores_t, 0, nb)
    idx_chunks, w_chunks = [], []
    for k in range(N_CHUNKS):
        nxt = min(k + 1, N_CHUNKS - 1)
        idx_chunks.append(idx_k)
        w_k, idx_k, gates_k = _expert_u_call(idx_k.reshape(-1), tab_u, h2, gates_k, scores_t,
                                             k * nb, nb, nxt * nb)
        w_chunks.append(w_k)
    idx_words = jnp.concatenate(idx_chunks, axis=0).reshape(-1)
    w = jnp.concatenate(w_chunks, axis=0)
    out = _expert_v_call(idx_words, _pack_table(peer_v[0]), w, x1.reshape(T, D), norm_f)
    return out.reshape(B, S, D)
```

```python
import functools

import numpy as np
import jax
import jax.numpy as jnp
from jax import lax
from jax.experimental import pallas as pl
from jax.experimental.pallas import tpu as pltpu
from jax.experimental.pallas import tpu_sc as plsc

F32 = jnp.float32
BF16 = jnp.bfloat16
I32 = jnp.int32

D_MODEL = 1024
D_CONV = 512
CONV_GROUP = 64
CONV_WIDTH = 3
HG_HEADS = 4
HG_KD = 128
HG_VD = 128
D_HG = HG_HEADS * HG_KD
D_IN = 3 * D_CONV + 4 * D_HG
CHUNK = 32
PEER_HEADS = 8
N_KEYS = 128
HALF_Q = 128
PEER_TOPK = 16
N_PAIRS = PEER_HEADS * PEER_TOPK
EPS = 1e-6

LANES = 128
SUBLANES = 8
SEQ_BLOCK = 256
TOK_BLOCK = 128
ROW_WORDS = D_MODEL // (2 * LANES)
ARAW_PITCH = TOK_BLOCK + 1
N_CHUNKS = 16
PACK_ROWS = 512
SC_CORES = 2
SC_SUBCORES = 16
SC_LANES = 16
SC_ROWS = 32
SC_CHUNKS = 1
VMEM_LIMIT_MIXER = 48 * 1024 * 1024
VMEM_LIMIT_EXPERT = 52 * 1024 * 1024

NT_DIMS = (((1,), (1,)), ((), ()))


def _bdot(a, b):
    return jnp.dot(a.astype(BF16), b.astype(BF16), preferred_element_type=F32)


def _bdot_nt(a, b):
    return lax.dot_general(a.astype(BF16), b.astype(BF16), NT_DIMS, preferred_element_type=F32)


def _split3(a):
    a1 = a.astype(BF16)
    r1 = a - a1.astype(F32)
    a2 = r1.astype(BF16)
    a3 = (r1 - a2.astype(F32)).astype(BF16)
    return a1, a2, a3


def _dot_sel(sel, a):
    out = None
    for piece in _split3(a):
        d = jnp.dot(sel, piece, preferred_element_type=F32)
        out = d if out is None else out + d
    return out


def _dot_sel_right(a, sel):
    out = None
    for piece in _split3(a):
        d = jnp.dot(piece, sel, preferred_element_type=F32)
        out = d if out is None else out + d
    return out


def _silu(x):
    return x * (1.0 / (1.0 + jnp.exp(-x)))


def _sigmoid(x):
    return 1.0 / (1.0 + jnp.exp(-x))


def _rms(x, gain):
    return x * lax.rsqrt(jnp.mean(x * x, axis=-1, keepdims=True) + EPS) * gain


def _mixer_kernel(x_ref, nmix_ref, win_ref, convw_ref, cgain_ref, lbl_ref, hgain_ref, wout_ref,
                  nffn_ref, wq_ref, keys_ref, cum_ref, gmean_ref, hmean_ref,
                  x1_ref, h2_ref, sc_ref,
                  up_ref, st_ref):
    L = SEQ_BLOCK
    sb = pl.program_id(1)

    @pl.when(sb == 0)
    def _():
        up_ref[0:SUBLANES, :] = jnp.zeros((SUBLANES, D_CONV), F32)
        st_ref[...] = jnp.zeros(st_ref.shape, F32)

    x = x_ref[0]
    h = _rms(x, nmix_ref[...])
    proj = _bdot(h, win_ref[...])

    u = proj[:, 0:D_CONV] * proj[:, D_CONV:2 * D_CONV]
    up_ref[SUBLANES:SUBLANES + L, :] = u
    u1 = up_ref[SUBLANES - 1:SUBLANES - 1 + L, :]
    u2 = up_ref[SUBLANES - 2:SUBLANES - 2 + L, :]
    cw = convw_ref[...]
    y = cw[0:1, :] * u2 + cw[1:2, :] * u1 + cw[2:3, :] * u
    y = proj[:, 2 * D_CONV:3 * D_CONV] * y
    up_ref[0:SUBLANES, :] = up_ref[L:L + SUBLANES, :]
    gms = _dot_sel_right(y * y, gmean_ref[...])
    conv_out = y * lax.rsqrt(gms + EPS) * cgain_ref[...]

    o0 = 3 * D_CONV
    qp = proj[:, o0:o0 + D_HG]
    fp = proj[:, o0 + D_HG:o0 + 2 * D_HG]
    vv = proj[:, o0 + 2 * D_HG:o0 + 3 * D_HG]
    gg = proj[:, o0 + 3 * D_HG:o0 + 4 * D_HG]
    lbl = lbl_ref[...]
    lbe = jnp.exp(lbl - jnp.max(lbl, axis=0, keepdims=True))
    lb = lbe[0:1, :] / jnp.sum(lbe, axis=0, keepdims=True)
    qh = _silu(qp) * (HG_KD ** -0.5)
    f = lb + (1.0 - lb) * _sigmoid(fp)
    kh = 1.0 - f
    logf = jnp.log(f)
    cums = _dot_sel(cum_ref[...], logf)
    gcum = cums[0:L, :]
    glast = cums[L:2 * L, :]
    q_dec = qh * jnp.exp(gcum)
    k_dec = kh * jnp.exp(-gcum)
    k_end = kh * jnp.exp(glast - gcum)
    decay = jnp.exp(glast)

    row = lax.broadcasted_iota(I32, (L, L), 0)
    col = lax.broadcasted_iota(I32, (L, L), 1)
    causal = (row // CHUNK == col // CHUNK) & (col <= row)

    o_heads = []
    for hh in range(HG_HEADS):
        cs = slice(hh * HG_KD, (hh + 1) * HG_KD)
        qd, kd, ke, vh, dc = q_dec[:, cs], k_dec[:, cs], k_end[:, cs], vv[:, cs], decay[:, cs]
        a = _bdot_nt(qd, kd)
        a = jnp.where(causal, a, 0.0)
        o_intra = _bdot(a, vh)
        st = st_ref[hh]
        pieces = []
        for n in range(L // CHUNK):
            rs = slice(n * CHUNK, (n + 1) * CHUNK)
            pieces.append(_bdot_nt(qd[rs], st))
            st = dc[n * CHUNK:n * CHUNK + 1, :] * st + _bdot(vh[rs].T, ke[rs])
        st_ref[hh] = st
        o_heads.append(o_intra + jnp.concatenate(pieces, axis=0))
    o = jnp.concatenate(o_heads, axis=1)
    oms = _dot_sel_right(o * o, hmean_ref[...])
    o = o * lax.rsqrt(oms + EPS) * hgain_ref[...]
    o = o * _silu(gg)

    mix = jnp.concatenate([conv_out, o], axis=1)
    x1 = x + _bdot(mix, wout_ref[...])
    x1_ref[0] = x1
    h2 = _rms(x1, nffn_ref[...])
    h2_ref[0] = h2.astype(BF16)
    q2 = _bdot(h2, wq_ref[...])
    for hp in range(2 * PEER_HEADS):
        sc_ref[hp] = _bdot_nt(keys_ref[hp], q2[:, hp * HALF_Q:(hp + 1) * HALF_Q])


def _mixer_call(x, norm_mix, w_in, conv_w, conv_gain, lb_logits, hg_gain, w_out, norm_ffn, wq, keys):
    B, S, _ = x.shape
    L = SEQ_BLOCK
    nsb = S // L
    T = B * S
    r = np.arange(L)
    same = (r[:, None] // CHUNK) == (r[None, :] // CHUNK)
    cum = np.concatenate([same & (r[None, :] <= r[:, None]), same], axis=0).astype(np.float32)
    c = np.arange(D_CONV)
    gmean = ((c[:, None] // CONV_GROUP) == (c[None, :] // CONV_GROUP)).astype(np.float32) / CONV_GROUP
    hmean = ((c[:, None] // HG_VD) == (c[None, :] // HG_VD)).astype(np.float32) / HG_VD

    const = lambda shape: pl.BlockSpec(shape, lambda b, s: (0,) * len(shape))
    return pl.pallas_call(
        _mixer_kernel,
        grid=(B, nsb),
        in_specs=[
            pl.BlockSpec((1, L, D_MODEL), lambda b, s: (b, s, 0)),
            const((1, D_MODEL)), const((D_MODEL, D_IN)), const((CONV_WIDTH, D_CONV)), const((1, D_CONV)),
            const(lb_logits.shape), const((1, D_HG)), const((D_MODEL, D_MODEL)), const((1, D_MODEL)),
            const((D_MODEL, 2 * PEER_HEADS * HALF_Q)), const((2 * PEER_HEADS, N_KEYS, HALF_Q)),
            const((2 * L, L)), const((D_CONV, D_CONV)), const((D_HG, D_HG)),
        ],
        out_specs=[
            pl.BlockSpec((1, L, D_MODEL), lambda b, s: (b, s, 0)),
            pl.BlockSpec((1, L, D_MODEL), lambda b, s: (b, s, 0)),
            pl.BlockSpec((2 * PEER_HEADS, N_KEYS, L), lambda b, s: (0, 0, b * nsb + s)),
        ],
        out_shape=[
            jax.ShapeDtypeStruct((B, S, D_MODEL), F32),
            jax.ShapeDtypeStruct((B, S, D_MODEL), BF16),
            jax.ShapeDtypeStruct((2 * PEER_HEADS, N_KEYS, T), F32),
        ],
        scratch_shapes=[
            pltpu.VMEM((L + 2 * SUBLANES, D_CONV), F32),
            pltpu.VMEM((HG_HEADS, HG_VD, HG_KD), F32),
        ],
        compiler_params=pltpu.CompilerParams(
            dimension_semantics=("arbitrary", "arbitrary"), vmem_limit_bytes=VMEM_LIMIT_MIXER),
        name="mixer",
    )(x, norm_mix.reshape(1, D_MODEL), w_in.astype(BF16), conv_w, conv_gain.reshape(1, D_CONV), lb_logits,
      jnp.tile(hg_gain.reshape(1, HG_VD), (1, HG_HEADS)), w_out.astype(BF16), norm_ffn.reshape(1, D_MODEL),
      wq.astype(BF16), keys.reshape(2 * PEER_HEADS, N_KEYS, HALF_Q).astype(BF16),
      jnp.asarray(cum, BF16), jnp.asarray(gmean, BF16), jnp.asarray(hmean, BF16))


def _extract_topk(vals, ids, n, sentinel):
    tops, sel = [], []
    for _ in range(n):
        m = jnp.max(vals, axis=0, keepdims=True)
        i = jnp.min(jnp.where(vals == m, ids, sentinel), axis=0, keepdims=True)
        tops.append(m)
        sel.append(i)
        vals = jnp.where(ids == i, -jnp.inf, vals)
    return jnp.concatenate(tops, axis=0), jnp.concatenate(sel, axis=0)


def _topk_head(hd, sc_ref, gate_ref, exp_ref):
    TB = TOK_BLOCK
    K = PEER_TOPK
    iota_keys = lax.broadcasted_iota(I32, (N_KEYS, TB), 0).astype(F32)
    sub = lax.broadcasted_iota(I32, (SUBLANES, TB), 0)
    subf = sub.astype(F32)
    s1, i1t = _extract_topk(sc_ref[2 * hd], iota_keys, K, N_KEYS)
    s2, i2t = _extract_topk(sc_ref[2 * hd + 1], iota_keys, K, N_KEYS)
    vals, ids = [], []
    for b0 in range(0, K, SUBLANES):
        vals.append(s1[0:1, :] + s2[b0:b0 + SUBLANES, :])
        ids.append(subf + b0)
    for a in range(1, SUBLANES):
        v = s1[a:a + 1, :] + s2[0:SUBLANES, :]
        vals.append(jnp.where(sub < K // (a + 1), v, -jnp.inf))
        ids.append(subf + a * K)
    vals.append(s1[SUBLANES:K, :] + s2[0:1, :])
    ids.append((subf + SUBLANES) * K)
    c_score, c_idx = _extract_topk(jnp.concatenate(vals, axis=0), jnp.concatenate(ids, axis=0), K, K * K)
    c_idx = c_idx.astype(I32)
    i1t = i1t.astype(I32)
    i2t = i2t.astype(I32)
    a_sel = lax.shift_right_logical(c_idx, 4)
    b_sel = c_idx & (K - 1)
    i1 = jnp.zeros((K, TB), I32)
    i2 = jnp.zeros((K, TB), I32)
    for a in range(K):
        i1 = i1 + jnp.where(a_sel == a, i1t[a:a + 1, :], 0)
        i2 = i2 + jnp.where(b_sel == a, i2t[a:a + 1, :], 0)
    e = jnp.exp(c_score - jnp.max(c_score, axis=0, keepdims=True))
    row0 = pl.multiple_of(hd * K, K)
    gate_ref[0, pl.ds(row0, K), :] = e / jnp.sum(e, axis=0, keepdims=True)
    exp_ref[pl.ds(row0, K), :] = (i1 * N_KEYS + i2) * ROW_WORDS


def _pack_offsets(exp_ref):
    half = N_PAIRS // 2
    return exp_ref[0:half, :] | (exp_ref[half:N_PAIRS, :] << 16)


def _topk_kernel(sc_ref, idx_ref, gate_ref, exp_ref):
    def head(hd, c):
        _topk_head(hd, sc_ref, gate_ref, exp_ref)
        return c

    lax.fori_loop(0, PEER_HEADS, head, 0)
    idx_ref[0] = _pack_offsets(exp_ref)


def _topk_call(scores_t, blk0, nblk):
    return pl.pallas_call(
        _topk_kernel,
        grid=(nblk,),
        in_specs=[pl.BlockSpec((2 * PEER_HEADS, N_KEYS, TOK_BLOCK), lambda i: (0, 0, i + blk0))],
        out_specs=[
            pl.BlockSpec((1, N_PAIRS // 2, TOK_BLOCK), lambda i: (i, 0, 0)),
            pl.BlockSpec((1, N_PAIRS, TOK_BLOCK), lambda i: (i, 0, 0)),
        ],
        out_shape=[
            jax.ShapeDtypeStruct((nblk, N_PAIRS // 2, TOK_BLOCK), I32),
            jax.ShapeDtypeStruct((nblk, N_PAIRS, TOK_BLOCK), F32),
        ],
        scratch_shapes=[pltpu.VMEM((N_PAIRS, TOK_BLOCK), I32)],
        compiler_params=pltpu.CompilerParams(dimension_semantics=("arbitrary",)),
        name="topk",
    )(scores_t)


def _pack_kernel(tab_ref, out_ref):
    rows = tab_ref.shape[0]
    bits = lambda v: lax.bitcast_convert_type(v.astype(BF16).astype(F32), jnp.uint32)
    for i in range(ROW_WORDS):
        lo = bits(tab_ref[:, (2 * i) * LANES:(2 * i + 1) * LANES]) >> 16
        hi = bits(tab_ref[:, (2 * i + 1) * LANES:(2 * i + 2) * LANES]) & jnp.uint32(0xFFFF0000)
        out_ref[pl.ds(i, rows, stride=ROW_WORDS), :] = lax.bitcast_convert_type(lo | hi, I32)


def _pack_table(tab):
    n = tab.shape[0]
    return pl.pallas_call(
        _pack_kernel,
        grid=(n // PACK_ROWS,),
        in_specs=[pl.BlockSpec((PACK_ROWS, D_MODEL), lambda i: (i, 0))],
        out_specs=pl.BlockSpec((PACK_ROWS * ROW_WORDS, LANES), lambda i: (i, 0)),
        out_shape=jax.ShapeDtypeStruct((n * ROW_WORDS, LANES), I32),
        compiler_params=pltpu.CompilerParams(dimension_semantics=("arbitrary",)),
        name="pack_table",
    )(tab)


def _load_table(tab_hbm, tab_ref, sem):
    @pl.when(pl.program_id(0) == 0)
    def _():
        copy = pltpu.make_async_copy(tab_hbm, tab_ref, sem)
        copy.start()
        copy.wait()


def _pair_offsets(idx_ref, w, j):
    word = idx_ref[w * TOK_BLOCK + j]
    lo = pl.multiple_of(word & 0xFFFF, ROW_WORDS)
    hi = pl.multiple_of(lax.shift_right_logical(word, 16), ROW_WORDS)
    return ((w, lo), (w + N_PAIRS // 2, hi))


def _gather_rows(idx_ref, tab_ref, g_ref, j):
    for w in range(N_PAIRS // 2):
        for k, off in _pair_offsets(idx_ref, w, j):
            g_ref[pl.ds(k * ROW_WORDS, ROW_WORDS), :] = tab_ref[pl.ds(off, ROW_WORDS), :]


def _two_token_step(consume, gather, g0_ref, g1_ref, j):
    consume(g0_ref, j)
    consume(g1_ref, j + 1)
    gather(g0_ref, jnp.minimum(j + 2, TOK_BLOCK - 1))
    gather(g1_ref, jnp.minimum(j + 3, TOK_BLOCK - 1))


def _two_token_pipeline(consume, gather, g0_ref, g1_ref):
    gather(g0_ref, 0)
    gather(g1_ref, 1)

    def step(i, c):
        _two_token_step(consume, gather, g0_ref, g1_ref, 2 * i)
        return c

    lax.fori_loop(0, TOK_BLOCK // 2, step, 0)


def _expert_u_kernel(idx_ref, tab_hbm, h2_ref, gate_ref, gsum_ref, sc_ref,
                     w_ref, idx_next_ref, gate_next_ref,
                     tab_ref, tab_sem, g0_ref, g1_ref, xs_ref, araw_ref, exp_ref):
    TB = TOK_BLOCK
    nchunk = D_MODEL // LANES
    group = 2 * LANES // nchunk
    pitch = ARAW_PITCH
    _load_table(tab_hbm, tab_ref, tab_sem)

    for c in range(nchunk):
        xs_ref[pl.ds(c, TB, stride=nchunk), :] = h2_ref[:, c * LANES:(c + 1) * LANES].astype(F32)

    def dots(g_ref, j):
        xj = xs_ref[pl.ds(pl.multiple_of(j * nchunk, nchunk), nchunk), :].astype(BF16)
        for g in range(N_PAIRS // group):
            wgt = pltpu.bitcast(g_ref[pl.ds(g * LANES, LANES), :], BF16)
            r = lax.dot_general(xj, wgt, NT_DIMS, preferred_element_type=F32)
            for half in range(2):
                cc = 2 * g + half
                araw_ref[pl.ds(cc * nchunk * pitch + j, nchunk, stride=pitch), :] = r[:, half * LANES:(half + 1) * LANES]

    gather = functools.partial(_gather_rows, idx_ref, tab_ref)
    tokens_per_head = TB // PEER_HEADS
    gather(g0_ref, 0)
    gather(g1_ref, 1)

    def head_step(hd, c):
        _topk_head(hd, sc_ref, gate_next_ref, exp_ref)
        for s in range(tokens_per_head // 2):
            _two_token_step(dots, gather, g0_ref, g1_ref, hd * tokens_per_head + 2 * s)
        return c

    lax.fori_loop(0, PEER_HEADS, head_step, 0)
    idx_next_ref[0] = _pack_offsets(exp_ref)

    lane_chunk = lax.broadcasted_iota(I32, (TB, LANES), 1) % nchunk
    a = jnp.zeros((TB, N_PAIRS), F32)
    for cc in range(nchunk):
        part = jnp.zeros((TB, LANES), F32)
        for c in range(nchunk):
            part = part + jnp.where(lane_chunk == c, araw_ref[pl.ds((cc * nchunk + c) * pitch, TB), :], 0.0)
        a = a + _dot_sel_right(part, gsum_ref[cc * LANES:(cc + 1) * LANES, :])
    w_ref[...] = gate_ref[0].T * (0.5 * a * (1.0 + lax.erf(a * (2.0 ** -0.5))))


def _pair_expand():
    pair_of_col = np.arange(D_MODEL) // SUBLANES
    return (np.arange(N_PAIRS)[:, None] == pair_of_col[None, :]).astype(np.float32)


def _expert_u_call(idx_words, tab, h2, gates, scores_t, blk0, nblk, next_blk0):
    nchunk = D_MODEL // LANES
    return pl.pallas_call(
        _expert_u_kernel,
        grid=(nblk,),
        in_specs=[
            pl.BlockSpec((N_PAIRS // 2 * TOK_BLOCK,), lambda i: (i,), memory_space=pltpu.SMEM),
            pl.BlockSpec(memory_space=pl.ANY),
            pl.BlockSpec((TOK_BLOCK, D_MODEL), lambda i: (i + blk0, 0)),
            pl.BlockSpec((1, N_PAIRS, TOK_BLOCK), lambda i: (i, 0, 0)),
            pl.BlockSpec((D_MODEL, N_PAIRS), lambda i: (0, 0)),
            pl.BlockSpec((2 * PEER_HEADS, N_KEYS, TOK_BLOCK), lambda i: (0, 0, i + next_blk0)),
        ],
        out_specs=[
            pl.BlockSpec((TOK_BLOCK, N_PAIRS), lambda i: (i, 0)),
            pl.BlockSpec((1, N_PAIRS // 2, TOK_BLOCK), lambda i: (i, 0, 0)),
            pl.BlockSpec((1, N_PAIRS, TOK_BLOCK), lambda i: (i, 0, 0)),
        ],
        out_shape=[
            jax.ShapeDtypeStruct((nblk * TOK_BLOCK, N_PAIRS), F32),
            jax.ShapeDtypeStruct((nblk, N_PAIRS // 2, TOK_BLOCK), I32),
            jax.ShapeDtypeStruct((nblk, N_PAIRS, TOK_BLOCK), F32),
        ],
        scratch_shapes=[
            pltpu.VMEM(tab.shape, I32),
            pltpu.SemaphoreType.DMA,
            pltpu.VMEM((N_PAIRS * ROW_WORDS, LANES), I32),
            pltpu.VMEM((N_PAIRS * ROW_WORDS, LANES), I32),
            pltpu.VMEM((TOK_BLOCK * nchunk, LANES), F32),
            pltpu.VMEM((nchunk * nchunk * ARAW_PITCH, LANES), F32),
            pltpu.VMEM((N_PAIRS, TOK_BLOCK), I32),
        ],
        compiler_params=pltpu.CompilerParams(
            dimension_semantics=("arbitrary",), vmem_limit_bytes=VMEM_LIMIT_EXPERT),
        name="expert_u",
    )(idx_words, tab, h2, gates, jnp.asarray(_pair_expand().T, BF16), scores_t)


def _expert_v_kernel(idx_ref, tab_hbm, w_ref, expand_ref, x1_ref, nf_ref, out_ref,
                     tab_ref, tab_sem, g0_ref, g1_ref, y_ref, wrep_ref):
    TB = TOK_BLOCK
    nchunk = D_MODEL // LANES
    _load_table(tab_hbm, tab_ref, tab_sem)
    wrep_ref[...] = jnp.dot(w_ref[...].astype(BF16), expand_ref[...], preferred_element_type=F32)
    sub = lax.broadcasted_iota(I32, (nchunk, D_MODEL), 0)
    col = lax.broadcasted_iota(I32, (nchunk, D_MODEL), 1)
    diag = (col % nchunk) == sub

    def weighted_sum(g_ref, j):
        wr = jnp.broadcast_to(wrep_ref[pl.ds(j, 1), :], (nchunk, D_MODEL))
        wm = jnp.where(diag, wr, 0.0).astype(BF16)
        gb = pltpu.bitcast(g_ref[...], BF16)
        y_ref[pl.ds(pl.multiple_of(j * nchunk, nchunk), nchunk), :] = jnp.dot(wm, gb, preferred_element_type=F32)

    _two_token_pipeline(weighted_sum, functools.partial(_gather_rows, idx_ref, tab_ref), g0_ref, g1_ref)

    y = jnp.concatenate([y_ref[pl.ds(c, TB, stride=nchunk), :] for c in range(nchunk)], axis=1)
    out_ref[...] = _rms(x1_ref[...] + y, nf_ref[...])


def _expert_v_call(idx_words, tab, w, x1, norm_f, blk0):
    T = x1.shape[0]
    nblk = T // TOK_BLOCK
    nchunk = D_MODEL // LANES
    return pl.pallas_call(
        _expert_v_kernel,
        grid=(nblk - blk0,),
        in_specs=[
            pl.BlockSpec((N_PAIRS // 2 * TOK_BLOCK,), lambda i: (i + blk0,), memory_space=pltpu.SMEM),
            pl.BlockSpec(memory_space=pl.ANY),
            pl.BlockSpec((TOK_BLOCK, N_PAIRS), lambda i: (i + blk0, 0)),
            pl.BlockSpec((N_PAIRS, D_MODEL), lambda i: (0, 0)),
            pl.BlockSpec((TOK_BLOCK, D_MODEL), lambda i: (i + blk0, 0)),
            pl.BlockSpec((1, D_MODEL), lambda i: (0, 0)),
        ],
        out_specs=pl.BlockSpec((TOK_BLOCK, D_MODEL), lambda i: (i + blk0, 0)),
        out_shape=jax.ShapeDtypeStruct((T, D_MODEL), F32),
        scratch_shapes=[
            pltpu.VMEM(tab.shape, I32),
            pltpu.SemaphoreType.DMA,
            pltpu.VMEM((N_PAIRS * ROW_WORDS, LANES), I32),
            pltpu.VMEM((N_PAIRS * ROW_WORDS, LANES), I32),
            pltpu.VMEM((TOK_BLOCK * nchunk, LANES), F32),
            pltpu.VMEM((TOK_BLOCK, D_MODEL), F32),
        ],
        compiler_params=pltpu.CompilerParams(
            dimension_semantics=("arbitrary",), vmem_limit_bytes=VMEM_LIMIT_EXPERT),
        name="expert_v",
    )(idx_words, tab, w, jnp.asarray(_pair_expand(), BF16), x1, norm_f.reshape(1, D_MODEL))


def _sc_weighted_rows_call(table, ids, w):
    ts = ids.shape[0]
    workers = SC_CORES * SC_SUBCORES
    tokens_per_worker = ts // workers
    mesh = plsc.VectorSubcoreMesh(core_axis_name="c", subcore_axis_name="s")

    @functools.partial(
        pl.kernel, mesh=mesh,
        out_type=jax.ShapeDtypeStruct((ts, D_MODEL), F32),
        scratch_types=[
            pltpu.VMEM((N_PAIRS,), I32),
            pltpu.VMEM((N_PAIRS,), F32),
            pltpu.VMEM((SC_ROWS, D_MODEL), F32),
            pltpu.VMEM((D_MODEL,), F32),
            pltpu.SemaphoreType.DMA,
        ],
    )
    def sc_kernel(table_hbm, ids_hbm, w_hbm, out_hbm, ids_v, w_v, rows_v, y_v, sem):
        worker = lax.axis_index("s") * SC_CORES + lax.axis_index("c")
        base = worker * tokens_per_worker

        def token(t, c):
            tok = base + t
            pltpu.sync_copy(ids_hbm.at[tok], ids_v)
            pltpu.sync_copy(w_hbm.at[tok], w_v)
            for ch in range(D_MODEL // SC_LANES):
                y_v[pl.ds(ch * SC_LANES, SC_LANES)] = jnp.zeros((SC_LANES,), F32)
            for b in range(N_PAIRS // SC_ROWS):
                pltpu.async_copy(table_hbm.at[ids_v.at[pl.ds(b * SC_ROWS, SC_ROWS)]], rows_v, sem).wait()
                wvs = [w_v[pl.ds(b * SC_ROWS + q * SC_LANES, SC_LANES)] for q in range(SC_ROWS // SC_LANES)]

                def chunk(ch, c2):
                    off = pl.multiple_of(ch * SC_LANES, SC_LANES)
                    acc = y_v[pl.ds(off, SC_LANES)]
                    for r in range(SC_ROWS):
                        wr = jnp.broadcast_to(wvs[r // SC_LANES][r % SC_LANES], (SC_LANES,))
                        acc = acc + wr * rows_v[r, pl.ds(off, SC_LANES)]
                    y_v[pl.ds(off, SC_LANES)] = acc
                    return c2

                lax.fori_loop(0, D_MODEL // SC_LANES, chunk, 0)
            pltpu.sync_copy(y_v, out_hbm.at[tok])
            return c

        lax.fori_loop(0, tokens_per_worker, token, 0)

    return sc_kernel(table, ids, w)


def _residual_norm_kernel(out_hbm, x1_ref, y_ref, nf_ref, out_ref):
    del out_hbm
    out_ref[...] = _rms(x1_ref[...] + y_ref[...], nf_ref[...])


def _residual_norm_call(out_partial, x1, y, norm_f):
    nblk = y.shape[0] // TOK_BLOCK
    blk = pl.BlockSpec((TOK_BLOCK, D_MODEL), lambda i: (i, 0))
    return pl.pallas_call(
        _residual_norm_kernel,
        grid=(nblk,),
        in_specs=[pl.BlockSpec(memory_space=pl.ANY), blk, blk, pl.BlockSpec((1, D_MODEL), lambda i: (0, 0))],
        out_specs=blk,
        out_shape=jax.ShapeDtypeStruct(out_partial.shape, F32),
        input_output_aliases={0: 0},
        compiler_params=pltpu.CompilerParams(dimension_semantics=("arbitrary",)),
        name="residual_norm",
    )(out_partial, x1, y, norm_f.reshape(1, D_MODEL))


def _expert_ids(idx_words):
    lo = (idx_words & 0xFFFF) // ROW_WORDS
    hi = lax.shift_right_logical(idx_words, 16) // ROW_WORDS
    ids = jnp.concatenate([lo, hi], axis=1)
    return ids.transpose(0, 2, 1).reshape(-1, N_PAIRS)


def kernel(x, norm_mix, w_in, conv_w, conv_gain, hg_lb_logits, hg_gain, w_out, norm_ffn, peer_wq, peer_keys,
           peer_u, peer_v, norm_f):
    B, S, D = x.shape
    depth = norm_mix.shape[0]
    assert depth == 1 and D == D_MODEL and S % SEQ_BLOCK == 0 and (B * S) % (TOK_BLOCK * N_CHUNKS) == 0
    T = B * S
    nb = T // (TOK_BLOCK * N_CHUNKS)
    x1, h2, scores_t = _mixer_call(x, norm_mix[0], w_in[0], conv_w[0], conv_gain[0], hg_lb_logits, hg_gain[0],
                                   w_out[0], norm_ffn[0], peer_wq[0], peer_keys[0])
    tab_u = _pack_table(peer_u[0])
    h2 = h2.reshape(T, D)
    idx_k, gates_k = _topk_call(scores_t, 0, nb)
    idx_chunks, w_chunks = [], []
    for k in range(N_CHUNKS):
        nxt = min(k + 1, N_CHUNKS - 1)
        idx_chunks.append(idx_k)
        w_k, idx_k, gates_k = _expert_u_call(idx_k.reshape(-1), tab_u, h2, gates_k, scores_t,
                                             k * nb, nb, nxt * nb)
        w_chunks.append(w_k)
    x1 = x1.reshape(T, D)
    y_sc = _sc_weighted_rows_call(peer_v[0], _expert_ids(jnp.concatenate(idx_chunks[:SC_CHUNKS], axis=0)),
                                  jnp.concatenate(w_chunks[:SC_CHUNKS], axis=0))
    idx_words = jnp.concatenate(idx_chunks, axis=0).reshape(-1)
    w = jnp.concatenate(w_chunks, axis=0)
    out = _expert_v_call(idx_words, _pack_table(peer_v[0]), w, x1, norm_f, SC_CHUNKS * nb)
    out = _residual_norm_call(out, x1, y_sc, norm_f)
    return out.reshape(B, S, D)
```

```python
import functools

import numpy as np
import jax
import jax.numpy as jnp
from jax import lax
from jax.experimental import pallas as pl
from jax.experimental.pallas import tpu as pltpu
from jax.experimental.pallas import tpu_sc as plsc

F32 = jnp.float32
BF16 = jnp.bfloat16
I32 = jnp.int32

D_MODEL = 1024
D_CONV = 512
CONV_GROUP = 64
CONV_WIDTH = 3
HG_HEADS = 4
HG_KD = 128
HG_VD = 128
D_HG = HG_HEADS * HG_KD
D_IN = 3 * D_CONV + 4 * D_HG
CHUNK = 32
PEER_HEADS = 8
N_KEYS = 128
HALF_Q = 128
PEER_TOPK = 16
N_PAIRS = PEER_HEADS * PEER_TOPK
EPS = 1e-6

LANES = 128
SUBLANES = 8
SEQ_BLOCK = 256
TOK_BLOCK = 128
ROW_WORDS = D_MODEL // (2 * LANES)
ARAW_PITCH = TOK_BLOCK + 1
N_CHUNKS = 16
PACK_ROWS = 512
SC_CORES = 2
SC_SUBCORES = 16
SC_LANES = 16
SC_ROWS = 32
SC_CHUNKS = 6
VMEM_LIMIT_MIXER = 48 * 1024 * 1024
VMEM_LIMIT_EXPERT = 52 * 1024 * 1024

NT_DIMS = (((1,), (1,)), ((), ()))


def _bdot(a, b):
    return jnp.dot(a.astype(BF16), b.astype(BF16), preferred_element_type=F32)


def _bdot_nt(a, b):
    return lax.dot_general(a.astype(BF16), b.astype(BF16), NT_DIMS, preferred_element_type=F32)


def _split3(a):
    a1 = a.astype(BF16)
    r1 = a - a1.astype(F32)
    a2 = r1.astype(BF16)
    a3 = (r1 - a2.astype(F32)).astype(BF16)
    return a1, a2, a3


def _dot_sel(sel, a):
    out = None
    for piece in _split3(a):
        d = jnp.dot(sel, piece, preferred_element_type=F32)
        out = d if out is None else out + d
    return out


def _dot_sel_right(a, sel):
    out = None
    for piece in _split3(a):
        d = jnp.dot(piece, sel, preferred_element_type=F32)
        out = d if out is None else out + d
    return out


def _silu(x):
    return x * (1.0 / (1.0 + jnp.exp(-x)))


def _sigmoid(x):
    return 1.0 / (1.0 + jnp.exp(-x))


def _rms(x, gain):
    return x * lax.rsqrt(jnp.mean(x * x, axis=-1, keepdims=True) + EPS) * gain


def _mixer_kernel(x_ref, nmix_ref, win_ref, convw_ref, cgain_ref, lbl_ref, hgain_ref, wout_ref,
                  nffn_ref, wq_ref, keys_ref, cum_ref, gmean_ref, hmean_ref,
                  x1_ref, h2_ref, sc_ref,
                  up_ref, st_ref):
    L = SEQ_BLOCK
    sb = pl.program_id(1)

    @pl.when(sb == 0)
    def _():
        up_ref[0:SUBLANES, :] = jnp.zeros((SUBLANES, D_CONV), F32)
        st_ref[...] = jnp.zeros(st_ref.shape, F32)

    x = x_ref[0]
    h = _rms(x, nmix_ref[...])
    proj = _bdot(h, win_ref[...])

    u = proj[:, 0:D_CONV] * proj[:, D_CONV:2 * D_CONV]
    up_ref[SUBLANES:SUBLANES + L, :] = u
    u1 = up_ref[SUBLANES - 1:SUBLANES - 1 + L, :]
    u2 = up_ref[SUBLANES - 2:SUBLANES - 2 + L, :]
    cw = convw_ref[...]
    y = cw[0:1, :] * u2 + cw[1:2, :] * u1 + cw[2:3, :] * u
    y = proj[:, 2 * D_CONV:3 * D_CONV] * y
    up_ref[0:SUBLANES, :] = up_ref[L:L + SUBLANES, :]
    gms = _dot_sel_right(y * y, gmean_ref[...])
    conv_out = y * lax.rsqrt(gms + EPS) * cgain_ref[...]

    o0 = 3 * D_CONV
    qp = proj[:, o0:o0 + D_HG]
    fp = proj[:, o0 + D_HG:o0 + 2 * D_HG]
    vv = proj[:, o0 + 2 * D_HG:o0 + 3 * D_HG]
    gg = proj[:, o0 + 3 * D_HG:o0 + 4 * D_HG]
    lbl = lbl_ref[...]
    lbe = jnp.exp(lbl - jnp.max(lbl, axis=0, keepdims=True))
    lb = lbe[0:1, :] / jnp.sum(lbe, axis=0, keepdims=True)
    qh = _silu(qp) * (HG_KD ** -0.5)
    f = lb + (1.0 - lb) * _sigmoid(fp)
    kh = 1.0 - f
    logf = jnp.log(f)
    cums = _dot_sel(cum_ref[...], logf)
    gcum = cums[0:L, :]
    glast = cums[L:2 * L, :]
    q_dec = qh * jnp.exp(gcum)
    k_dec = kh * jnp.exp(-gcum)
    k_end = kh * jnp.exp(glast - gcum)
    decay = jnp.exp(glast)

    row = lax.broadcasted_iota(I32, (L, L), 0)
    col = lax.broadcasted_iota(I32, (L, L), 1)
    causal = (row // CHUNK == col // CHUNK) & (col <= row)

    o_heads = []
    for hh in range(HG_HEADS):
        cs = slice(hh * HG_KD, (hh + 1) * HG_KD)
        qd, kd, ke, vh, dc = q_dec[:, cs], k_dec[:, cs], k_end[:, cs], vv[:, cs], decay[:, cs]
        a = _bdot_nt(qd, kd)
        a = jnp.where(causal, a, 0.0)
        o_intra = _bdot(a, vh)
        st = st_ref[hh]
        pieces = []
        for n in range(L // CHUNK):
            rs = slice(n * CHUNK, (n + 1) * CHUNK)
            pieces.append(_bdot_nt(qd[rs], st))
            st = dc[n * CHUNK:n * CHUNK + 1, :] * st + _bdot(vh[rs].T, ke[rs])
        st_ref[hh] = st
        o_heads.append(o_intra + jnp.concatenate(pieces, axis=0))
    o = jnp.concatenate(o_heads, axis=1)
    oms = _dot_sel_right(o * o, hmean_ref[...])
    o = o * lax.rsqrt(oms + EPS) * hgain_ref[...]
    o = o * _silu(gg)

    mix = jnp.concatenate([conv_out, o], axis=1)
    x1 = x + _bdot(mix, wout_ref[...])
    x1_ref[0] = x1
    h2 = _rms(x1, nffn_ref[...])
    h2_ref[0] = h2.astype(BF16)
    q2 = _bdot(h2, wq_ref[...])
    for hp in range(2 * PEER_HEADS):
        sc_ref[hp] = _bdot_nt(keys_ref[hp], q2[:, hp * HALF_Q:(hp + 1) * HALF_Q])


def _mixer_call(x, norm_mix, w_in, conv_w, conv_gain, lb_logits, hg_gain, w_out, norm_ffn, wq, keys):
    B, S, _ = x.shape
    L = SEQ_BLOCK
    nsb = S // L
    T = B * S
    r = np.arange(L)
    same = (r[:, None] // CHUNK) == (r[None, :] // CHUNK)
    cum = np.concatenate([same & (r[None, :] <= r[:, None]), same], axis=0).astype(np.float32)
    c = np.arange(D_CONV)
    gmean = ((c[:, None] // CONV_GROUP) == (c[None, :] // CONV_GROUP)).astype(np.float32) / CONV_GROUP
    hmean = ((c[:, None] // HG_VD) == (c[None, :] // HG_VD)).astype(np.float32) / HG_VD

    const = lambda shape: pl.BlockSpec(shape, lambda b, s: (0,) * len(shape))
    return pl.pallas_call(
        _mixer_kernel,
        grid=(B, nsb),
        in_specs=[
            pl.BlockSpec((1, L, D_MODEL), lambda b, s: (b, s, 0)),
            const((1, D_MODEL)), const((D_MODEL, D_IN)), const((CONV_WIDTH, D_CONV)), const((1, D_CONV)),
            const(lb_logits.shape), const((1, D_HG)), const((D_MODEL, D_MODEL)), const((1, D_MODEL)),
            const((D_MODEL, 2 * PEER_HEADS * HALF_Q)), const((2 * PEER_HEADS, N_KEYS, HALF_Q)),
            const((2 * L, L)), const((D_CONV, D_CONV)), const((D_HG, D_HG)),
        ],
        out_specs=[
            pl.BlockSpec((1, L, D_MODEL), lambda b, s: (b, s, 0)),
            pl.BlockSpec((1, L, D_MODEL), lambda b, s: (b, s, 0)),
            pl.BlockSpec((2 * PEER_HEADS, N_KEYS, L), lambda b, s: (0, 0, b * nsb + s)),
        ],
        out_shape=[
            jax.ShapeDtypeStruct((B, S, D_MODEL), F32),
            jax.ShapeDtypeStruct((B, S, D_MODEL), BF16),
            jax.ShapeDtypeStruct((2 * PEER_HEADS, N_KEYS, T), F32),
        ],
        scratch_shapes=[
            pltpu.VMEM((L + 2 * SUBLANES, D_CONV), F32),
            pltpu.VMEM((HG_HEADS, HG_VD, HG_KD), F32),
        ],
        compiler_params=pltpu.CompilerParams(
            dimension_semantics=("arbitrary", "arbitrary"), vmem_limit_bytes=VMEM_LIMIT_MIXER),
        name="mixer",
    )(x, norm_mix.reshape(1, D_MODEL), w_in.astype(BF16), conv_w, conv_gain.reshape(1, D_CONV), lb_logits,
      jnp.tile(hg_gain.reshape(1, HG_VD), (1, HG_HEADS)), w_out.astype(BF16), norm_ffn.reshape(1, D_MODEL),
      wq.astype(BF16), keys.reshape(2 * PEER_HEADS, N_KEYS, HALF_Q).astype(BF16),
      jnp.asarray(cum, BF16), jnp.asarray(gmean, BF16), jnp.asarray(hmean, BF16))


def _extract_topk(vals, ids, n, sentinel):
    tops, sel = [], []
    for _ in range(n):
        m = jnp.max(vals, axis=0, keepdims=True)
        i = jnp.min(jnp.where(vals == m, ids, sentinel), axis=0, keepdims=True)
        tops.append(m)
        sel.append(i)
        vals = jnp.where(ids == i, -jnp.inf, vals)
    return jnp.concatenate(tops, axis=0), jnp.concatenate(sel, axis=0)


def _topk_head(hd, sc_ref, gate_ref, exp_ref):
    TB = TOK_BLOCK
    K = PEER_TOPK
    iota_keys = lax.broadcasted_iota(I32, (N_KEYS, TB), 0).astype(F32)
    sub = lax.broadcasted_iota(I32, (SUBLANES, TB), 0)
    subf = sub.astype(F32)
    s1, i1t = _extract_topk(sc_ref[2 * hd], iota_keys, K, N_KEYS)
    s2, i2t = _extract_topk(sc_ref[2 * hd + 1], iota_keys, K, N_KEYS)
    vals, ids = [], []
    for b0 in range(0, K, SUBLANES):
        vals.append(s1[0:1, :] + s2[b0:b0 + SUBLANES, :])
        ids.append(subf + b0)
    for a in range(1, SUBLANES):
        v = s1[a:a + 1, :] + s2[0:SUBLANES, :]
        vals.append(jnp.where(sub < K // (a + 1), v, -jnp.inf))
        ids.append(subf + a * K)
    vals.append(s1[SUBLANES:K, :] + s2[0:1, :])
    ids.append((subf + SUBLANES) * K)
    c_score, c_idx = _extract_topk(jnp.concatenate(vals, axis=0), jnp.concatenate(ids, axis=0), K, K * K)
    c_idx = c_idx.astype(I32)
    i1t = i1t.astype(I32)
    i2t = i2t.astype(I32)
    a_sel = lax.shift_right_logical(c_idx, 4)
    b_sel = c_idx & (K - 1)
    i1 = jnp.zeros((K, TB), I32)
    i2 = jnp.zeros((K, TB), I32)
    for a in range(K):
        i1 = i1 + jnp.where(a_sel == a, i1t[a:a + 1, :], 0)
        i2 = i2 + jnp.where(b_sel == a, i2t[a:a + 1, :], 0)
    e = jnp.exp(c_score - jnp.max(c_score, axis=0, keepdims=True))
    row0 = pl.multiple_of(hd * K, K)
    gate_ref[0, pl.ds(row0, K), :] = e / jnp.sum(e, axis=0, keepdims=True)
    exp_ref[pl.ds(row0, K), :] = (i1 * N_KEYS + i2) * ROW_WORDS


def _pack_offsets(exp_ref):
    half = N_PAIRS // 2
    return exp_ref[0:half, :] | (exp_ref[half:N_PAIRS, :] << 16)


def _topk_kernel(sc_ref, idx_ref, gate_ref, exp_ref):
    def head(hd, c):
        _topk_head(hd, sc_ref, gate_ref, exp_ref)
        return c

    lax.fori_loop(0, PEER_HEADS, head, 0)
    idx_ref[0] = _pack_offsets(exp_ref)


def _topk_call(scores_t, blk0, nblk):
    return pl.pallas_call(
        _topk_kernel,
        grid=(nblk,),
        in_specs=[pl.BlockSpec((2 * PEER_HEADS, N_KEYS, TOK_BLOCK), lambda i: (0, 0, i + blk0))],
        out_specs=[
            pl.BlockSpec((1, N_PAIRS // 2, TOK_BLOCK), lambda i: (i, 0, 0)),
            pl.BlockSpec((1, N_PAIRS, TOK_BLOCK), lambda i: (i, 0, 0)),
        ],
        out_shape=[
            jax.ShapeDtypeStruct((nblk, N_PAIRS // 2, TOK_BLOCK), I32),
            jax.ShapeDtypeStruct((nblk, N_PAIRS, TOK_BLOCK), F32),
        ],
        scratch_shapes=[pltpu.VMEM((N_PAIRS, TOK_BLOCK), I32)],
        compiler_params=pltpu.CompilerParams(dimension_semantics=("arbitrary",)),
        name="topk",
    )(scores_t)


def _pack_kernel(tab_ref, out_ref):
    rows = tab_ref.shape[0]
    bits = lambda v: lax.bitcast_convert_type(v.astype(BF16).astype(F32), jnp.uint32)
    for i in range(ROW_WORDS):
        lo = bits(tab_ref[:, (2 * i) * LANES:(2 * i + 1) * LANES]) >> 16
        hi = bits(tab_ref[:, (2 * i + 1) * LANES:(2 * i + 2) * LANES]) & jnp.uint32(0xFFFF0000)
        out_ref[pl.ds(i, rows, stride=ROW_WORDS), :] = lax.bitcast_convert_type(lo | hi, I32)


def _pack_table(tab):
    n = tab.shape[0]
    return pl.pallas_call(
        _pack_kernel,
        grid=(n // PACK_ROWS,),
        in_specs=[pl.BlockSpec((PACK_ROWS, D_MODEL), lambda i: (i, 0))],
        out_specs=pl.BlockSpec((PACK_ROWS * ROW_WORDS, LANES), lambda i: (i, 0)),
        out_shape=jax.ShapeDtypeStruct((n * ROW_WORDS, LANES), I32),
        compiler_params=pltpu.CompilerParams(dimension_semantics=("arbitrary",)),
        name="pack_table",
    )(tab)


def _load_table(tab_hbm, tab_ref, sem):
    @pl.when(pl.program_id(0) == 0)
    def _():
        copy = pltpu.make_async_copy(tab_hbm, tab_ref, sem)
        copy.start()
        copy.wait()


def _pair_offsets(idx_ref, w, j):
    word = idx_ref[w * TOK_BLOCK + j]
    lo = pl.multiple_of(word & 0xFFFF, ROW_WORDS)
    hi = pl.multiple_of(lax.shift_right_logical(word, 16), ROW_WORDS)
    return ((w, lo), (w + N_PAIRS // 2, hi))


def _gather_rows(idx_ref, tab_ref, g_ref, j):
    for w in range(N_PAIRS // 2):
        for k, off in _pair_offsets(idx_ref, w, j):
            g_ref[pl.ds(k * ROW_WORDS, ROW_WORDS), :] = tab_ref[pl.ds(off, ROW_WORDS), :]


def _two_token_step(consume, gather, g0_ref, g1_ref, j):
    consume(g0_ref, j)
    consume(g1_ref, j + 1)
    gather(g0_ref, jnp.minimum(j + 2, TOK_BLOCK - 1))
    gather(g1_ref, jnp.minimum(j + 3, TOK_BLOCK - 1))


def _two_token_pipeline(consume, gather, g0_ref, g1_ref):
    gather(g0_ref, 0)
    gather(g1_ref, 1)

    def step(i, c):
        _two_token_step(consume, gather, g0_ref, g1_ref, 2 * i)
        return c

    lax.fori_loop(0, TOK_BLOCK // 2, step, 0)


def _expert_u_kernel(idx_ref, tab_hbm, h2_ref, gate_ref, gsum_ref, sc_ref,
                     w_ref, idx_next_ref, gate_next_ref,
                     tab_ref, tab_sem, g0_ref, g1_ref, xs_ref, araw_ref, exp_ref):
    TB = TOK_BLOCK
    nchunk = D_MODEL // LANES
    group = 2 * LANES // nchunk
    pitch = ARAW_PITCH
    _load_table(tab_hbm, tab_ref, tab_sem)

    for c in range(nchunk):
        xs_ref[pl.ds(c, TB, stride=nchunk), :] = h2_ref[:, c * LANES:(c + 1) * LANES].astype(F32)

    def dots(g_ref, j):
        xj = xs_ref[pl.ds(pl.multiple_of(j * nchunk, nchunk), nchunk), :].astype(BF16)
        for g in range(N_PAIRS // group):
            wgt = pltpu.bitcast(g_ref[pl.ds(g * LANES, LANES), :], BF16)
            r = lax.dot_general(xj, wgt, NT_DIMS, preferred_element_type=F32)
            for half in range(2):
                cc = 2 * g + half
                araw_ref[pl.ds(cc * nchunk * pitch + j, nchunk, stride=pitch), :] = r[:, half * LANES:(half + 1) * LANES]

    gather = functools.partial(_gather_rows, idx_ref, tab_ref)
    tokens_per_head = TB // PEER_HEADS
    gather(g0_ref, 0)
    gather(g1_ref, 1)

    def head_step(hd, c):
        _topk_head(hd, sc_ref, gate_next_ref, exp_ref)
        for s in range(tokens_per_head // 2):
            _two_token_step(dots, gather, g0_ref, g1_ref, hd * tokens_per_head + 2 * s)
        return c

    lax.fori_loop(0, PEER_HEADS, head_step, 0)
    idx_next_ref[0] = _pack_offsets(exp_ref)

    lane_chunk = lax.broadcasted_iota(I32, (TB, LANES), 1) % nchunk
    a = jnp.zeros((TB, N_PAIRS), F32)
    for cc in range(nchunk):
        part = jnp.zeros((TB, LANES), F32)
        for c in range(nchunk):
            part = part + jnp.where(lane_chunk == c, araw_ref[pl.ds((cc * nchunk + c) * pitch, TB), :], 0.0)
        a = a + _dot_sel_right(part, gsum_ref[cc * LANES:(cc + 1) * LANES, :])
    w_ref[...] = gate_ref[0].T * (0.5 * a * (1.0 + lax.erf(a * (2.0 ** -0.5))))


def _pair_expand():
    pair_of_col = np.arange(D_MODEL) // SUBLANES
    return (np.arange(N_PAIRS)[:, None] == pair_of_col[None, :]).astype(np.float32)


def _expert_u_call(idx_words, tab, h2, gates, scores_t, blk0, nblk, next_blk0):
    nchunk = D_MODEL // LANES
    return pl.pallas_call(
        _expert_u_kernel,
        grid=(nblk,),
        in_specs=[
            pl.BlockSpec((N_PAIRS // 2 * TOK_BLOCK,), lambda i: (i,), memory_space=pltpu.SMEM),
            pl.BlockSpec(memory_space=pl.ANY),
            pl.BlockSpec((TOK_BLOCK, D_MODEL), lambda i: (i + blk0, 0)),
            pl.BlockSpec((1, N_PAIRS, TOK_BLOCK), lambda i: (i, 0, 0)),
            pl.BlockSpec((D_MODEL, N_PAIRS), lambda i: (0, 0)),
            pl.BlockSpec((2 * PEER_HEADS, N_KEYS, TOK_BLOCK), lambda i: (0, 0, i + next_blk0)),
        ],
        out_specs=[
            pl.BlockSpec((TOK_BLOCK, N_PAIRS), lambda i: (i, 0)),
            pl.BlockSpec((1, N_PAIRS // 2, TOK_BLOCK), lambda i: (i, 0, 0)),
            pl.BlockSpec((1, N_PAIRS, TOK_BLOCK), lambda i: (i, 0, 0)),
        ],
        out_shape=[
            jax.ShapeDtypeStruct((nblk * TOK_BLOCK, N_PAIRS), F32),
            jax.ShapeDtypeStruct((nblk, N_PAIRS // 2, TOK_BLOCK), I32),
            jax.ShapeDtypeStruct((nblk, N_PAIRS, TOK_BLOCK), F32),
        ],
        scratch_shapes=[
            pltpu.VMEM(tab.shape, I32),
            pltpu.SemaphoreType.DMA,
            pltpu.VMEM((N_PAIRS * ROW_WORDS, LANES), I32),
            pltpu.VMEM((N_PAIRS * ROW_WORDS, LANES), I32),
            pltpu.VMEM((TOK_BLOCK * nchunk, LANES), F32),
            pltpu.VMEM((nchunk * nchunk * ARAW_PITCH, LANES), F32),
            pltpu.VMEM((N_PAIRS, TOK_BLOCK), I32),
        ],
        compiler_params=pltpu.CompilerParams(
            dimension_semantics=("arbitrary",), vmem_limit_bytes=VMEM_LIMIT_EXPERT),
        name="expert_u",
    )(idx_words, tab, h2, gates, jnp.asarray(_pair_expand().T, BF16), scores_t)


def _expert_v_kernel(idx_ref, tab_hbm, w_ref, expand_ref, x1_ref, nf_ref, out_ref,
                     tab_ref, tab_sem, g0_ref, g1_ref, y_ref, wrep_ref):
    TB = TOK_BLOCK
    nchunk = D_MODEL // LANES
    _load_table(tab_hbm, tab_ref, tab_sem)
    wrep_ref[...] = jnp.dot(w_ref[...].astype(BF16), expand_ref[...], preferred_element_type=F32)
    sub = lax.broadcasted_iota(I32, (nchunk, D_MODEL), 0)
    col = lax.broadcasted_iota(I32, (nchunk, D_MODEL), 1)
    diag = (col % nchunk) == sub

    def weighted_sum(g_ref, j):
        wr = jnp.broadcast_to(wrep_ref[pl.ds(j, 1), :], (nchunk, D_MODEL))
        wm = jnp.where(diag, wr, 0.0).astype(BF16)
        gb = pltpu.bitcast(g_ref[...], BF16)
        y_ref[pl.ds(pl.multiple_of(j * nchunk, nchunk), nchunk), :] = jnp.dot(wm, gb, preferred_element_type=F32)

    _two_token_pipeline(weighted_sum, functools.partial(_gather_rows, idx_ref, tab_ref), g0_ref, g1_ref)

    y = jnp.concatenate([y_ref[pl.ds(c, TB, stride=nchunk), :] for c in range(nchunk)], axis=1)
    out_ref[...] = _rms(x1_ref[...] + y, nf_ref[...])


def _expert_v_call(idx_words, tab, w, x1, norm_f, blk0):
    T = x1.shape[0]
    nblk = T // TOK_BLOCK
    nchunk = D_MODEL // LANES
    return pl.pallas_call(
        _expert_v_kernel,
        grid=(nblk - blk0,),
        in_specs=[
            pl.BlockSpec((N_PAIRS // 2 * TOK_BLOCK,), lambda i: (i + blk0,), memory_space=pltpu.SMEM),
            pl.BlockSpec(memory_space=pl.ANY),
            pl.BlockSpec((TOK_BLOCK, N_PAIRS), lambda i: (i + blk0, 0)),
            pl.BlockSpec((N_PAIRS, D_MODEL), lambda i: (0, 0)),
            pl.BlockSpec((TOK_BLOCK, D_MODEL), lambda i: (i + blk0, 0)),
            pl.BlockSpec((1, D_MODEL), lambda i: (0, 0)),
        ],
        out_specs=pl.BlockSpec((TOK_BLOCK, D_MODEL), lambda i: (i + blk0, 0)),
        out_shape=jax.ShapeDtypeStruct((T, D_MODEL), F32),
        scratch_shapes=[
            pltpu.VMEM(tab.shape, I32),
            pltpu.SemaphoreType.DMA,
            pltpu.VMEM((N_PAIRS * ROW_WORDS, LANES), I32),
            pltpu.VMEM((N_PAIRS * ROW_WORDS, LANES), I32),
            pltpu.VMEM((TOK_BLOCK * nchunk, LANES), F32),
            pltpu.VMEM((TOK_BLOCK, D_MODEL), F32),
        ],
        compiler_params=pltpu.CompilerParams(
            dimension_semantics=("arbitrary",), vmem_limit_bytes=VMEM_LIMIT_EXPERT),
        name="expert_v",
    )(idx_words, tab, w, jnp.asarray(_pair_expand(), BF16), x1, norm_f.reshape(1, D_MODEL))


def _sc_weighted_rows_call(table, ids, w):
    ts = ids.shape[0]
    workers = SC_CORES * SC_SUBCORES
    tokens_per_worker = ts // workers
    mesh = plsc.VectorSubcoreMesh(core_axis_name="c", subcore_axis_name="s")

    @functools.partial(
        pl.kernel, mesh=mesh,
        out_type=jax.ShapeDtypeStruct((ts, D_MODEL), F32),
        scratch_types=[
            pltpu.VMEM((N_PAIRS,), I32),
            pltpu.VMEM((N_PAIRS,), F32),
            pltpu.VMEM((SC_ROWS, D_MODEL), F32),
            pltpu.VMEM((D_MODEL,), F32),
            pltpu.SemaphoreType.DMA,
        ],
    )
    def sc_kernel(table_hbm, ids_hbm, w_hbm, out_hbm, ids_v, w_v, rows_v, y_v, sem):
        worker = lax.axis_index("s") * SC_CORES + lax.axis_index("c")
        base = worker * tokens_per_worker

        def token(t, c):
            tok = base + t
            pltpu.sync_copy(ids_hbm.at[tok], ids_v)
            pltpu.sync_copy(w_hbm.at[tok], w_v)
            for ch in range(D_MODEL // SC_LANES):
                y_v[pl.ds(ch * SC_LANES, SC_LANES)] = jnp.zeros((SC_LANES,), F32)
            for b in range(N_PAIRS // SC_ROWS):
                pltpu.async_copy(table_hbm.at[ids_v.at[pl.ds(b * SC_ROWS, SC_ROWS)]], rows_v, sem).wait()
                wvs = [w_v[pl.ds(b * SC_ROWS + q * SC_LANES, SC_LANES)] for q in range(SC_ROWS // SC_LANES)]

                def chunk(ch, c2):
                    off = pl.multiple_of(ch * SC_LANES, SC_LANES)
                    acc = y_v[pl.ds(off, SC_LANES)]
                    for r in range(SC_ROWS):
                        wr = jnp.broadcast_to(wvs[r // SC_LANES][r % SC_LANES], (SC_LANES,))
                        acc = acc + wr * rows_v[r, pl.ds(off, SC_LANES)]
                    y_v[pl.ds(off, SC_LANES)] = acc
                    return c2

                lax.fori_loop(0, D_MODEL // SC_LANES, chunk, 0)
            pltpu.sync_copy(y_v, out_hbm.at[tok])
            return c

        lax.fori_loop(0, tokens_per_worker, token, 0)

    return sc_kernel(table, ids, w)


def _residual_norm_kernel(out_hbm, x1_ref, y_ref, nf_ref, out_ref):
    del out_hbm
    out_ref[...] = _rms(x1_ref[...] + y_ref[...], nf_ref[...])


def _residual_norm_call(out_partial, x1, y, norm_f):
    nblk = y.shape[0] // TOK_BLOCK
    blk = pl.BlockSpec((TOK_BLOCK, D_MODEL), lambda i: (i, 0))
    return pl.pallas_call(
        _residual_norm_kernel,
        grid=(nblk,),
        in_specs=[pl.BlockSpec(memory_space=pl.ANY), blk, blk, pl.BlockSpec((1, D_MODEL), lambda i: (0, 0))],
        out_specs=blk,
        out_shape=jax.ShapeDtypeStruct(out_partial.shape, F32),
        input_output_aliases={0: 0},
        compiler_params=pltpu.CompilerParams(dimension_semantics=("arbitrary",)),
        name="residual_norm",
    )(out_partial, x1, y, norm_f.reshape(1, D_MODEL))


def _expert_ids(idx_words):
    lo = (idx_words & 0xFFFF) // ROW_WORDS
    hi = lax.shift_right_logical(idx_words, 16) // ROW_WORDS
    ids = jnp.concatenate([lo, hi], axis=1)
    return ids.transpose(0, 2, 1).reshape(-1, N_PAIRS)


def kernel(x, norm_mix, w_in, conv_w, conv_gain, hg_lb_logits, hg_gain, w_out, norm_ffn, peer_wq, peer_keys,
           peer_u, peer_v, norm_f):
    B, S, D = x.shape
    depth = norm_mix.shape[0]
    assert depth == 1 and D == D_MODEL and S % SEQ_BLOCK == 0 and (B * S) % (TOK_BLOCK * N_CHUNKS) == 0
    T = B * S
    nb = T // (TOK_BLOCK * N_CHUNKS)
    x1, h2, scores_t = _mixer_call(x, norm_mix[0], w_in[0], conv_w[0], conv_gain[0], hg_lb_logits, hg_gain[0],
                                   w_out[0], norm_ffn[0], peer_wq[0], peer_keys[0])
    tab_u = _pack_table(peer_u[0])
    h2 = h2.reshape(T, D)
    idx_k, gates_k = _topk_call(scores_t, 0, nb)
    idx_chunks, w_chunks = [], []
    for k in range(N_CHUNKS):
        nxt = min(k + 1, N_CHUNKS - 1)
        idx_chunks.append(idx_k)
        w_k, idx_k, gates_k = _expert_u_call(idx_k.reshape(-1), tab_u, h2, gates_k, scores_t,
                                             k * nb, nb, nxt * nb)
        w_chunks.append(w_k)
    x1 = x1.reshape(T, D)
    y_sc = jnp.concatenate([_sc_weighted_rows_call(peer_v[0], _expert_ids(idx_chunks[k]), w_chunks[k])
                            for k in range(SC_CHUNKS)], axis=0)
    idx_words = jnp.concatenate(idx_chunks, axis=0).reshape(-1)
    w = jnp.concatenate(w_chunks, axis=0)
    out = _expert_v_call(idx_words, _pack_table(peer_v[0]), w, x1, norm_f, SC_CHUNKS * nb)
    out = _residual_norm_call(out, x1, y_sc, norm_f)
    return out.reshape(B, S, D)
```

```python
import dataclasses
import functools

import numpy as np
import jax
import jax.numpy as jnp
from jax import lax
from jax.experimental import pallas as pl
from jax.experimental.pallas import tpu as pltpu
from jax.experimental.pallas import tpu_sc as plsc

F32 = jnp.float32
BF16 = jnp.bfloat16
I32 = jnp.int32

D_MODEL = 1024
D_CONV = 512
CONV_GROUP = 64
CONV_WIDTH = 3
HG_HEADS = 4
HG_KD = 128
HG_VD = 128
D_HG = HG_HEADS * HG_KD
D_IN = 3 * D_CONV + 4 * D_HG
CHUNK = 32
PEER_HEADS = 8
N_KEYS = 128
HALF_Q = 128
PEER_TOPK = 16
N_PAIRS = PEER_HEADS * PEER_TOPK
EPS = 1e-6

LANES = 128
SUBLANES = 8
SEQ_BLOCK = 256
TOK_BLOCK = 128
ROW_WORDS = D_MODEL // (2 * LANES)
ARAW_PITCH = TOK_BLOCK + 1
N_CHUNKS = 16
PACK_ROWS = 512
SC_CORES = 2
SC_SUBCORES = 16
SC_LANES = 16
SC_ROWS = 32
SC_CHUNKS = 10
VMEM_LIMIT_MIXER = 48 * 1024 * 1024
VMEM_LIMIT_EXPERT = 52 * 1024 * 1024

NT_DIMS = (((1,), (1,)), ((), ()))


def _bdot(a, b):
    return jnp.dot(a.astype(BF16), b.astype(BF16), preferred_element_type=F32)


def _bdot_nt(a, b):
    return lax.dot_general(a.astype(BF16), b.astype(BF16), NT_DIMS, preferred_element_type=F32)


def _split3(a):
    a1 = a.astype(BF16)
    r1 = a - a1.astype(F32)
    a2 = r1.astype(BF16)
    a3 = (r1 - a2.astype(F32)).astype(BF16)
    return a1, a2, a3


def _dot_sel(sel, a):
    out = None
    for piece in _split3(a):
        d = jnp.dot(sel, piece, preferred_element_type=F32)
        out = d if out is None else out + d
    return out


def _dot_sel_right(a, sel):
    out = None
    for piece in _split3(a):
        d = jnp.dot(piece, sel, preferred_element_type=F32)
        out = d if out is None else out + d
    return out


def _silu(x):
    return x * (1.0 / (1.0 + jnp.exp(-x)))


def _sigmoid(x):
    return 1.0 / (1.0 + jnp.exp(-x))


def _rms(x, gain):
    return x * lax.rsqrt(jnp.mean(x * x, axis=-1, keepdims=True) + EPS) * gain


def _mixer_kernel(x_ref, nmix_ref, win_ref, convw_ref, cgain_ref, lbl_ref, hgain_ref, wout_ref,
                  nffn_ref, wq_ref, keys_ref, cum_ref, gmean_ref, hmean_ref,
                  x1_ref, h2_ref, sc_ref,
                  up_ref, st_ref):
    L = SEQ_BLOCK
    sb = pl.program_id(1)

    @pl.when(sb == 0)
    def _():
        up_ref[0:SUBLANES, :] = jnp.zeros((SUBLANES, D_CONV), F32)
        st_ref[...] = jnp.zeros(st_ref.shape, F32)

    x = x_ref[0]
    h = _rms(x, nmix_ref[...])
    proj = _bdot(h, win_ref[...])

    u = proj[:, 0:D_CONV] * proj[:, D_CONV:2 * D_CONV]
    up_ref[SUBLANES:SUBLANES + L, :] = u
    u1 = up_ref[SUBLANES - 1:SUBLANES - 1 + L, :]
    u2 = up_ref[SUBLANES - 2:SUBLANES - 2 + L, :]
    cw = convw_ref[...]
    y = cw[0:1, :] * u2 + cw[1:2, :] * u1 + cw[2:3, :] * u
    y = proj[:, 2 * D_CONV:3 * D_CONV] * y
    up_ref[0:SUBLANES, :] = up_ref[L:L + SUBLANES, :]
    gms = _dot_sel_right(y * y, gmean_ref[...])
    conv_out = y * lax.rsqrt(gms + EPS) * cgain_ref[...]

    o0 = 3 * D_CONV
    qp = proj[:, o0:o0 + D_HG]
    fp = proj[:, o0 + D_HG:o0 + 2 * D_HG]
    vv = proj[:, o0 + 2 * D_HG:o0 + 3 * D_HG]
    gg = proj[:, o0 + 3 * D_HG:o0 + 4 * D_HG]
    lbl = lbl_ref[...]
    lbe = jnp.exp(lbl - jnp.max(lbl, axis=0, keepdims=True))
    lb = lbe[0:1, :] / jnp.sum(lbe, axis=0, keepdims=True)
    qh = _silu(qp) * (HG_KD ** -0.5)
    f = lb + (1.0 - lb) * _sigmoid(fp)
    kh = 1.0 - f
    logf = jnp.log(f)
    cums = _dot_sel(cum_ref[...], logf)
    gcum = cums[0:L, :]
    glast = cums[L:2 * L, :]
    q_dec = qh * jnp.exp(gcum)
    k_dec = kh * jnp.exp(-gcum)
    k_end = kh * jnp.exp(glast - gcum)
    decay = jnp.exp(glast)

    row = lax.broadcasted_iota(I32, (L, L), 0)
    col = lax.broadcasted_iota(I32, (L, L), 1)
    causal = (row // CHUNK == col // CHUNK) & (col <= row)

    o_heads = []
    for hh in range(HG_HEADS):
        cs = slice(hh * HG_KD, (hh + 1) * HG_KD)
        qd, kd, ke, vh, dc = q_dec[:, cs], k_dec[:, cs], k_end[:, cs], vv[:, cs], decay[:, cs]
        a = _bdot_nt(qd, kd)
        a = jnp.where(causal, a, 0.0)
        o_intra = _bdot(a, vh)
        st = st_ref[hh]
        pieces = []
        for n in range(L // CHUNK):
            rs = slice(n * CHUNK, (n + 1) * CHUNK)
            pieces.append(_bdot_nt(qd[rs], st))
            st = dc[n * CHUNK:n * CHUNK + 1, :] * st + _bdot(vh[rs].T, ke[rs])
        st_ref[hh] = st
        o_heads.append(o_intra + jnp.concatenate(pieces, axis=0))
    o = jnp.concatenate(o_heads, axis=1)
    oms = _dot_sel_right(o * o, hmean_ref[...])
    o = o * lax.rsqrt(oms + EPS) * hgain_ref[...]
    o = o * _silu(gg)

    mix = jnp.concatenate([conv_out, o], axis=1)
    x1 = x + _bdot(mix, wout_ref[...])
    x1_ref[0] = x1
    h2 = _rms(x1, nffn_ref[...])
    h2_ref[0] = h2.astype(BF16)
    q2 = _bdot(h2, wq_ref[...])
    for hp in range(2 * PEER_HEADS):
        sc_ref[hp] = _bdot_nt(keys_ref[hp], q2[:, hp * HALF_Q:(hp + 1) * HALF_Q])


def _mixer_call(x, norm_mix, w_in, conv_w, conv_gain, lb_logits, hg_gain, w_out, norm_ffn, wq, keys):
    B, S, _ = x.shape
    L = SEQ_BLOCK
    nsb = S // L
    T = B * S
    r = np.arange(L)
    same = (r[:, None] // CHUNK) == (r[None, :] // CHUNK)
    cum = np.concatenate([same & (r[None, :] <= r[:, None]), same], axis=0).astype(np.float32)
    c = np.arange(D_CONV)
    gmean = ((c[:, None] // CONV_GROUP) == (c[None, :] // CONV_GROUP)).astype(np.float32) / CONV_GROUP
    hmean = ((c[:, None] // HG_VD) == (c[None, :] // HG_VD)).astype(np.float32) / HG_VD

    const = lambda shape: pl.BlockSpec(shape, lambda b, s: (0,) * len(shape))
    return pl.pallas_call(
        _mixer_kernel,
        grid=(B, nsb),
        in_specs=[
            pl.BlockSpec((1, L, D_MODEL), lambda b, s: (b, s, 0)),
            const((1, D_MODEL)), const((D_MODEL, D_IN)), const((CONV_WIDTH, D_CONV)), const((1, D_CONV)),
            const(lb_logits.shape), const((1, D_HG)), const((D_MODEL, D_MODEL)), const((1, D_MODEL)),
            const((D_MODEL, 2 * PEER_HEADS * HALF_Q)), const((2 * PEER_HEADS, N_KEYS, HALF_Q)),
            const((2 * L, L)), const((D_CONV, D_CONV)), const((D_HG, D_HG)),
        ],
        out_specs=[
            pl.BlockSpec((1, L, D_MODEL), lambda b, s: (b, s, 0)),
            pl.BlockSpec((1, L, D_MODEL), lambda b, s: (b, s, 0)),
            pl.BlockSpec((2 * PEER_HEADS, N_KEYS, L), lambda b, s: (0, 0, b * nsb + s)),
        ],
        out_shape=[
            jax.ShapeDtypeStruct((B, S, D_MODEL), F32),
            jax.ShapeDtypeStruct((B, S, D_MODEL), BF16),
            jax.ShapeDtypeStruct((2 * PEER_HEADS, N_KEYS, T), F32),
        ],
        scratch_shapes=[
            pltpu.VMEM((L + 2 * SUBLANES, D_CONV), F32),
            pltpu.VMEM((HG_HEADS, HG_VD, HG_KD), F32),
        ],
        compiler_params=pltpu.CompilerParams(
            dimension_semantics=("arbitrary", "arbitrary"), vmem_limit_bytes=VMEM_LIMIT_MIXER),
        name="mixer",
    )(x, norm_mix.reshape(1, D_MODEL), w_in.astype(BF16), conv_w, conv_gain.reshape(1, D_CONV), lb_logits,
      jnp.tile(hg_gain.reshape(1, HG_VD), (1, HG_HEADS)), w_out.astype(BF16), norm_ffn.reshape(1, D_MODEL),
      wq.astype(BF16), keys.reshape(2 * PEER_HEADS, N_KEYS, HALF_Q).astype(BF16),
      jnp.asarray(cum, BF16), jnp.asarray(gmean, BF16), jnp.asarray(hmean, BF16))


def _extract_topk(vals, ids, n, sentinel):
    tops, sel = [], []
    for _ in range(n):
        m = jnp.max(vals, axis=0, keepdims=True)
        i = jnp.min(jnp.where(vals == m, ids, sentinel), axis=0, keepdims=True)
        tops.append(m)
        sel.append(i)
        vals = jnp.where(ids == i, -jnp.inf, vals)
    return jnp.concatenate(tops, axis=0), jnp.concatenate(sel, axis=0)


def _topk_head(hd, sc_ref, gate_ref, exp_ref):
    TB = TOK_BLOCK
    K = PEER_TOPK
    iota_keys = lax.broadcasted_iota(I32, (N_KEYS, TB), 0).astype(F32)
    sub = lax.broadcasted_iota(I32, (SUBLANES, TB), 0)
    subf = sub.astype(F32)
    s1, i1t = _extract_topk(sc_ref[2 * hd], iota_keys, K, N_KEYS)
    s2, i2t = _extract_topk(sc_ref[2 * hd + 1], iota_keys, K, N_KEYS)
    vals, ids = [], []
    for b0 in range(0, K, SUBLANES):
        vals.append(s1[0:1, :] + s2[b0:b0 + SUBLANES, :])
        ids.append(subf + b0)
    for a in range(1, SUBLANES):
        v = s1[a:a + 1, :] + s2[0:SUBLANES, :]
        vals.append(jnp.where(sub < K // (a + 1), v, -jnp.inf))
        ids.append(subf + a * K)
    vals.append(s1[SUBLANES:K, :] + s2[0:1, :])
    ids.append((subf + SUBLANES) * K)
    c_score, c_idx = _extract_topk(jnp.concatenate(vals, axis=0), jnp.concatenate(ids, axis=0), K, K * K)
    c_idx = c_idx.astype(I32)
    i1t = i1t.astype(I32)
    i2t = i2t.astype(I32)
    a_sel = lax.shift_right_logical(c_idx, 4)
    b_sel = c_idx & (K - 1)
    i1 = jnp.zeros((K, TB), I32)
    i2 = jnp.zeros((K, TB), I32)
    for a in range(K):
        i1 = i1 + jnp.where(a_sel == a, i1t[a:a + 1, :], 0)
        i2 = i2 + jnp.where(b_sel == a, i2t[a:a + 1, :], 0)
    e = jnp.exp(c_score - jnp.max(c_score, axis=0, keepdims=True))
    row0 = pl.multiple_of(hd * K, K)
    gate_ref[0, pl.ds(row0, K), :] = e / jnp.sum(e, axis=0, keepdims=True)
    exp_ref[pl.ds(row0, K), :] = (i1 * N_KEYS + i2) * ROW_WORDS


def _pack_offsets(exp_ref):
    half = N_PAIRS // 2
    return exp_ref[0:half, :] | (exp_ref[half:N_PAIRS, :] << 16)


def _topk_kernel(sc_ref, idx_ref, gate_ref, exp_ref):
    def head(hd, c):
        _topk_head(hd, sc_ref, gate_ref, exp_ref)
        return c

    lax.fori_loop(0, PEER_HEADS, head, 0)
    idx_ref[0] = _pack_offsets(exp_ref)


def _topk_call(scores_t, blk0, nblk):
    return pl.pallas_call(
        _topk_kernel,
        grid=(nblk,),
        in_specs=[pl.BlockSpec((2 * PEER_HEADS, N_KEYS, TOK_BLOCK), lambda i: (0, 0, i + blk0))],
        out_specs=[
            pl.BlockSpec((1, N_PAIRS // 2, TOK_BLOCK), lambda i: (i, 0, 0)),
            pl.BlockSpec((1, N_PAIRS, TOK_BLOCK), lambda i: (i, 0, 0)),
        ],
        out_shape=[
            jax.ShapeDtypeStruct((nblk, N_PAIRS // 2, TOK_BLOCK), I32),
            jax.ShapeDtypeStruct((nblk, N_PAIRS, TOK_BLOCK), F32),
        ],
        scratch_shapes=[pltpu.VMEM((N_PAIRS, TOK_BLOCK), I32)],
        compiler_params=pltpu.CompilerParams(dimension_semantics=("arbitrary",)),
        name="topk",
    )(scores_t)


def _pack_kernel(tab_ref, out_ref):
    rows = tab_ref.shape[0]
    bits = lambda v: lax.bitcast_convert_type(v.astype(BF16).astype(F32), jnp.uint32)
    for i in range(ROW_WORDS):
        lo = bits(tab_ref[:, (2 * i) * LANES:(2 * i + 1) * LANES]) >> 16
        hi = bits(tab_ref[:, (2 * i + 1) * LANES:(2 * i + 2) * LANES]) & jnp.uint32(0xFFFF0000)
        out_ref[pl.ds(i, rows, stride=ROW_WORDS), :] = lax.bitcast_convert_type(lo | hi, I32)


def _pack_table(tab):
    n = tab.shape[0]
    return pl.pallas_call(
        _pack_kernel,
        grid=(n // PACK_ROWS,),
        in_specs=[pl.BlockSpec((PACK_ROWS, D_MODEL), lambda i: (i, 0))],
        out_specs=pl.BlockSpec((PACK_ROWS * ROW_WORDS, LANES), lambda i: (i, 0)),
        out_shape=jax.ShapeDtypeStruct((n * ROW_WORDS, LANES), I32),
        compiler_params=pltpu.CompilerParams(dimension_semantics=("arbitrary",)),
        name="pack_table",
    )(tab)


def _load_table(tab_hbm, tab_ref, sem):
    @pl.when(pl.program_id(0) == 0)
    def _():
        copy = pltpu.make_async_copy(tab_hbm, tab_ref, sem)
        copy.start()
        copy.wait()


def _pair_offsets(idx_ref, w, j):
    word = idx_ref[w * TOK_BLOCK + j]
    lo = pl.multiple_of(word & 0xFFFF, ROW_WORDS)
    hi = pl.multiple_of(lax.shift_right_logical(word, 16), ROW_WORDS)
    return ((w, lo), (w + N_PAIRS // 2, hi))


def _gather_rows(idx_ref, tab_ref, g_ref, j):
    for w in range(N_PAIRS // 2):
        for k, off in _pair_offsets(idx_ref, w, j):
            g_ref[pl.ds(k * ROW_WORDS, ROW_WORDS), :] = tab_ref[pl.ds(off, ROW_WORDS), :]


def _two_token_step(consume, gather, g0_ref, g1_ref, j):
    consume(g0_ref, j)
    consume(g1_ref, j + 1)
    gather(g0_ref, jnp.minimum(j + 2, TOK_BLOCK - 1))
    gather(g1_ref, jnp.minimum(j + 3, TOK_BLOCK - 1))


def _two_token_pipeline(consume, gather, g0_ref, g1_ref):
    gather(g0_ref, 0)
    gather(g1_ref, 1)

    def step(i, c):
        _two_token_step(consume, gather, g0_ref, g1_ref, 2 * i)
        return c

    lax.fori_loop(0, TOK_BLOCK // 2, step, 0)


def _expert_u_kernel(idx_ref, tab_hbm, h2_ref, gate_ref, gsum_ref, sc_ref,
                     w_ref, idx_next_ref, gate_next_ref,
                     tab_ref, tab_sem, g0_ref, g1_ref, xs_ref, araw_ref, exp_ref):
    TB = TOK_BLOCK
    nchunk = D_MODEL // LANES
    group = 2 * LANES // nchunk
    pitch = ARAW_PITCH
    _load_table(tab_hbm, tab_ref, tab_sem)

    for c in range(nchunk):
        xs_ref[pl.ds(c, TB, stride=nchunk), :] = h2_ref[:, c * LANES:(c + 1) * LANES].astype(F32)

    def dots(g_ref, j):
        xj = xs_ref[pl.ds(pl.multiple_of(j * nchunk, nchunk), nchunk), :].astype(BF16)
        for g in range(N_PAIRS // group):
            wgt = pltpu.bitcast(g_ref[pl.ds(g * LANES, LANES), :], BF16)
            r = lax.dot_general(xj, wgt, NT_DIMS, preferred_element_type=F32)
            for half in range(2):
                cc = 2 * g + half
                araw_ref[pl.ds(cc * nchunk * pitch + j, nchunk, stride=pitch), :] = r[:, half * LANES:(half + 1) * LANES]

    gather = functools.partial(_gather_rows, idx_ref, tab_ref)
    tokens_per_head = TB // PEER_HEADS
    gather(g0_ref, 0)
    gather(g1_ref, 1)

    def head_step(hd, c):
        _topk_head(hd, sc_ref, gate_next_ref, exp_ref)
        for s in range(tokens_per_head // 2):
            _two_token_step(dots, gather, g0_ref, g1_ref, hd * tokens_per_head + 2 * s)
        return c

    lax.fori_loop(0, PEER_HEADS, head_step, 0)
    idx_next_ref[0] = _pack_offsets(exp_ref)

    lane_chunk = lax.broadcasted_iota(I32, (TB, LANES), 1) % nchunk
    a = jnp.zeros((TB, N_PAIRS), F32)
    for cc in range(nchunk):
        part = jnp.zeros((TB, LANES), F32)
        for c in range(nchunk):
            part = part + jnp.where(lane_chunk == c, araw_ref[pl.ds((cc * nchunk + c) * pitch, TB), :], 0.0)
        a = a + _dot_sel_right(part, gsum_ref[cc * LANES:(cc + 1) * LANES, :])
    w_ref[...] = gate_ref[0].T * (0.5 * a * (1.0 + lax.erf(a * (2.0 ** -0.5))))


def _pair_expand():
    pair_of_col = np.arange(D_MODEL) // SUBLANES
    return (np.arange(N_PAIRS)[:, None] == pair_of_col[None, :]).astype(np.float32)


def _expert_u_call(idx_words, tab, h2, gates, scores_t, blk0, nblk, next_blk0):
    nchunk = D_MODEL // LANES
    return pl.pallas_call(
        _expert_u_kernel,
        grid=(nblk,),
        in_specs=[
            pl.BlockSpec((N_PAIRS // 2 * TOK_BLOCK,), lambda i: (i,), memory_space=pltpu.SMEM),
            pl.BlockSpec(memory_space=pl.ANY),
            pl.BlockSpec((TOK_BLOCK, D_MODEL), lambda i: (i + blk0, 0)),
            pl.BlockSpec((1, N_PAIRS, TOK_BLOCK), lambda i: (i, 0, 0)),
            pl.BlockSpec((D_MODEL, N_PAIRS), lambda i: (0, 0)),
            pl.BlockSpec((2 * PEER_HEADS, N_KEYS, TOK_BLOCK), lambda i: (0, 0, i + next_blk0)),
        ],
        out_specs=[
            pl.BlockSpec((TOK_BLOCK, N_PAIRS), lambda i: (i, 0)),
            pl.BlockSpec((1, N_PAIRS // 2, TOK_BLOCK), lambda i: (i, 0, 0)),
            pl.BlockSpec((1, N_PAIRS, TOK_BLOCK), lambda i: (i, 0, 0)),
        ],
        out_shape=[
            jax.ShapeDtypeStruct((nblk * TOK_BLOCK, N_PAIRS), F32),
            jax.ShapeDtypeStruct((nblk, N_PAIRS // 2, TOK_BLOCK), I32),
            jax.ShapeDtypeStruct((nblk, N_PAIRS, TOK_BLOCK), F32),
        ],
        scratch_shapes=[
            pltpu.VMEM(tab.shape, I32),
            pltpu.SemaphoreType.DMA,
            pltpu.VMEM((N_PAIRS * ROW_WORDS, LANES), I32),
            pltpu.VMEM((N_PAIRS * ROW_WORDS, LANES), I32),
            pltpu.VMEM((TOK_BLOCK * nchunk, LANES), F32),
            pltpu.VMEM((nchunk * nchunk * ARAW_PITCH, LANES), F32),
            pltpu.VMEM((N_PAIRS, TOK_BLOCK), I32),
        ],
        compiler_params=pltpu.CompilerParams(
            dimension_semantics=("arbitrary",), vmem_limit_bytes=VMEM_LIMIT_EXPERT),
        name="expert_u",
    )(idx_words, tab, h2, gates, jnp.asarray(_pair_expand().T, BF16), scores_t)


def _expert_v_kernel(idx_ref, tab_hbm, w_ref, expand_ref, x1_ref, nf_ref, out_ref,
                     tab_ref, tab_sem, g0_ref, g1_ref, y_ref, wrep_ref):
    TB = TOK_BLOCK
    nchunk = D_MODEL // LANES
    _load_table(tab_hbm, tab_ref, tab_sem)
    wrep_ref[...] = jnp.dot(w_ref[...].astype(BF16), expand_ref[...], preferred_element_type=F32)
    sub = lax.broadcasted_iota(I32, (nchunk, D_MODEL), 0)
    col = lax.broadcasted_iota(I32, (nchunk, D_MODEL), 1)
    diag = (col % nchunk) == sub

    def weighted_sum(g_ref, j):
        wr = jnp.broadcast_to(wrep_ref[pl.ds(j, 1), :], (nchunk, D_MODEL))
        wm = jnp.where(diag, wr, 0.0).astype(BF16)
        gb = pltpu.bitcast(g_ref[...], BF16)
        y_ref[pl.ds(pl.multiple_of(j * nchunk, nchunk), nchunk), :] = jnp.dot(wm, gb, preferred_element_type=F32)

    _two_token_pipeline(weighted_sum, functools.partial(_gather_rows, idx_ref, tab_ref), g0_ref, g1_ref)

    y = jnp.concatenate([y_ref[pl.ds(c, TB, stride=nchunk), :] for c in range(nchunk)], axis=1)
    out_ref[...] = _rms(x1_ref[...] + y, nf_ref[...])


def _expert_v_call(idx_words, tab, w, x1, norm_f, blk0):
    T = x1.shape[0]
    nblk = T // TOK_BLOCK
    nchunk = D_MODEL // LANES
    return pl.pallas_call(
        _expert_v_kernel,
        grid=(nblk - blk0,),
        in_specs=[
            pl.BlockSpec((N_PAIRS // 2 * TOK_BLOCK,), lambda i: (i + blk0,), memory_space=pltpu.SMEM),
            pl.BlockSpec(memory_space=pl.ANY),
            pl.BlockSpec((TOK_BLOCK, N_PAIRS), lambda i: (i + blk0, 0)),
            pl.BlockSpec((N_PAIRS, D_MODEL), lambda i: (0, 0)),
            pl.BlockSpec((TOK_BLOCK, D_MODEL), lambda i: (i + blk0, 0)),
            pl.BlockSpec((1, D_MODEL), lambda i: (0, 0)),
        ],
        out_specs=pl.BlockSpec((TOK_BLOCK, D_MODEL), lambda i: (i + blk0, 0)),
        out_shape=jax.ShapeDtypeStruct((T, D_MODEL), F32),
        scratch_shapes=[
            pltpu.VMEM(tab.shape, I32),
            pltpu.SemaphoreType.DMA,
            pltpu.VMEM((N_PAIRS * ROW_WORDS, LANES), I32),
            pltpu.VMEM((N_PAIRS * ROW_WORDS, LANES), I32),
            pltpu.VMEM((TOK_BLOCK * nchunk, LANES), F32),
            pltpu.VMEM((TOK_BLOCK, D_MODEL), F32),
        ],
        compiler_params=pltpu.CompilerParams(
            dimension_semantics=("arbitrary",), vmem_limit_bytes=VMEM_LIMIT_EXPERT),
        name="expert_v",
    )(idx_words, tab, w, jnp.asarray(_pair_expand(), BF16), x1, norm_f.reshape(1, D_MODEL))


def _sc_weighted_rows_call(table_words, ids, w):
    ts = ids.shape[0]
    row_len = table_words.shape[1]
    workers = SC_CORES * SC_SUBCORES
    tpw = ts // workers
    nbatch = N_PAIRS // SC_ROWS
    groups = LANES // SC_LANES
    mesh = plsc.VectorSubcoreMesh(core_axis_name="c", subcore_axis_name="s")
    params = pltpu.CompilerParams()
    if "needs_layout_passes" in pltpu.CompilerParams.__dataclass_fields__:
        params = dataclasses.replace(params, needs_layout_passes=False)

    @functools.partial(
        pl.kernel, mesh=mesh, compiler_params=params,
        out_type=jax.ShapeDtypeStruct((ts, D_MODEL), F32),
        scratch_types=[
            pltpu.VMEM((N_PAIRS,), I32), pltpu.VMEM((N_PAIRS,), I32),
            pltpu.VMEM((N_PAIRS,), F32), pltpu.VMEM((N_PAIRS,), F32),
            pltpu.VMEM((SC_ROWS, row_len), I32), pltpu.VMEM((SC_ROWS, row_len), I32),
            pltpu.VMEM((D_MODEL,), F32),
            pltpu.SemaphoreType.DMA, pltpu.SemaphoreType.DMA, pltpu.SemaphoreType.DMA, pltpu.SemaphoreType.DMA,
        ],
    )
    def sc_kernel(table_hbm, ids_hbm, w_hbm, out_hbm, ids0, ids1, w0, w1, rows0, rows1, y_v,
                  sem0, sem1, sem_ids, sem_w):
        worker = lax.axis_index("s") * SC_CORES + lax.axis_index("c")
        base = worker * tpw
        ids_v, w_v, rows_v, sems = (ids0, ids1), (w0, w1), (rows0, rows1), (sem0, sem1)

        def gather(slot, b, buf):
            src = table_hbm.at[ids_v[slot].at[pl.ds(b * SC_ROWS, SC_ROWS)]]
            return pltpu.make_async_copy(src, rows_v[buf], sems[buf])

        def fetch(slot, tok):
            return (pltpu.make_async_copy(ids_hbm.at[tok], ids_v[slot], sem_ids),
                    pltpu.make_async_copy(w_hbm.at[tok], w_v[slot], sem_w))

        def accumulate(slot, b, buf):
            wvs = [w_v[slot][pl.ds(b * SC_ROWS + q * SC_LANES, SC_LANES)] for q in range(SC_ROWS // SC_LANES)]
            wrs = [jnp.broadcast_to(wvs[r // SC_LANES][r % SC_LANES], (SC_LANES,)) for r in range(SC_ROWS)]

            def group(g, c):
                off = pl.multiple_of(g * SC_LANES, SC_LANES)
                f_lo = pl.multiple_of((2 * (g // groups)) * LANES + (g % groups) * SC_LANES, SC_LANES)
                f_hi = pl.multiple_of(f_lo + LANES, SC_LANES)
                if b == 0:
                    acc_lo = jnp.zeros((SC_LANES,), F32)
                    acc_hi = jnp.zeros((SC_LANES,), F32)
                else:
                    acc_lo = y_v[pl.ds(f_lo, SC_LANES)]
                    acc_hi = y_v[pl.ds(f_hi, SC_LANES)]
                for r in range(SC_ROWS):
                    word = rows_v[buf][r, pl.ds(off, SC_LANES)]
                    acc_lo = acc_lo + wrs[r] * lax.bitcast_convert_type(word << 16, F32)
                    acc_hi = acc_hi + wrs[r] * lax.bitcast_convert_type(word & jnp.int32(-65536), F32)
                y_v[pl.ds(f_lo, SC_LANES)] = acc_lo
                y_v[pl.ds(f_hi, SC_LANES)] = acc_hi
                return c

            lax.fori_loop(0, row_len // SC_LANES, group, 0)

        for copy in fetch(0, base):
            copy.start()
        for copy in fetch(0, base):
            copy.wait()
        gather(0, 0, 0).start()

        def token_pair(p, c):
            t0 = base + 2 * p
            t_next = base + jnp.minimum(2 * p + 2, tpw - 1)
            for q in range(2 * nbatch):
                slot, b, buf = q // nbatch, q % nbatch, q % 2
                if q == 0:
                    for copy in fetch(1, t0 + 1):
                        copy.start()
                if q == nbatch:
                    for copy in fetch(0, t_next):
                        copy.start()
                if q == nbatch - 1:
                    for copy in fetch(1, t0 + 1):
                        copy.wait()
                if q == 2 * nbatch - 1:
                    for copy in fetch(0, t_next):
                        copy.wait()
                nq = (q + 1) % (2 * nbatch)
                gather(nq // nbatch, nq % nbatch, nq % 2).start()
                gather(slot, b, buf).wait()
                accumulate(slot, b, buf)
                if b == nbatch - 1:
                    pltpu.sync_copy(y_v, out_hbm.at[t0 + slot])
            return c

        lax.fori_loop(0, tpw // 2, token_pair, 0)
        gather(0, 0, 0).wait()

    return sc_kernel(table_words, ids, w)


def _residual_norm_kernel(out_hbm, x1_ref, y_ref, nf_ref, out_ref):
    del out_hbm
    out_ref[...] = _rms(x1_ref[...] + y_ref[...], nf_ref[...])


def _residual_norm_call(out_partial, x1, y, norm_f):
    nblk = y.shape[0] // TOK_BLOCK
    blk = pl.BlockSpec((TOK_BLOCK, D_MODEL), lambda i: (i, 0))
    return pl.pallas_call(
        _residual_norm_kernel,
        grid=(nblk,),
        in_specs=[pl.BlockSpec(memory_space=pl.ANY), blk, blk, pl.BlockSpec((1, D_MODEL), lambda i: (0, 0))],
        out_specs=blk,
        out_shape=jax.ShapeDtypeStruct(out_partial.shape, F32),
        input_output_aliases={0: 0},
        compiler_params=pltpu.CompilerParams(dimension_semantics=("arbitrary",)),
        name="residual_norm",
    )(out_partial, x1, y, norm_f.reshape(1, D_MODEL))


def _expert_ids(idx_words):
    lo = (idx_words & 0xFFFF) // ROW_WORDS
    hi = lax.shift_right_logical(idx_words, 16) // ROW_WORDS
    ids = jnp.concatenate([lo, hi], axis=1)
    return ids.transpose(0, 2, 1).reshape(-1, N_PAIRS)


def kernel(x, norm_mix, w_in, conv_w, conv_gain, hg_lb_logits, hg_gain, w_out, norm_ffn, peer_wq, peer_keys,
           peer_u, peer_v, norm_f):
    B, S, D = x.shape
    depth = norm_mix.shape[0]
    assert depth == 1 and D == D_MODEL and S % SEQ_BLOCK == 0 and (B * S) % (TOK_BLOCK * N_CHUNKS) == 0
    T = B * S
    nb = T // (TOK_BLOCK * N_CHUNKS)
    x1, h2, scores_t = _mixer_call(x, norm_mix[0], w_in[0], conv_w[0], conv_gain[0], hg_lb_logits, hg_gain[0],
                                   w_out[0], norm_ffn[0], peer_wq[0], peer_keys[0])
    tab_u = _pack_table(peer_u[0])
    h2 = h2.reshape(T, D)
    idx_k, gates_k = _topk_call(scores_t, 0, nb)
    idx_chunks, w_chunks = [], []
    for k in range(N_CHUNKS):
        nxt = min(k + 1, N_CHUNKS - 1)
        idx_chunks.append(idx_k)
        w_k, idx_k, gates_k = _expert_u_call(idx_k.reshape(-1), tab_u, h2, gates_k, scores_t,
                                             k * nb, nb, nxt * nb)
        w_chunks.append(w_k)
    x1 = x1.reshape(T, D)
    tab_v = _pack_table(peer_v[0])
    tab_v_rows = tab_v.reshape(peer_v.shape[1], D // 2)
    y_sc = jnp.concatenate([_sc_weighted_rows_call(tab_v_rows, _expert_ids(idx_chunks[k]), w_chunks[k])
                            for k in range(SC_CHUNKS)], axis=0)
    idx_words = jnp.concatenate(idx_chunks, axis=0).reshape(-1)
    w = jnp.concatenate(w_chunks, axis=0)
    out = _expert_v_call(idx_words, tab_v, w, x1, norm_f, SC_CHUNKS * nb)
    out = _residual_norm_call(out, x1, y_sc, norm_f)
    return out.reshape(B, S, D)
```

```python
import dataclasses
import functools

import numpy as np
import jax
import jax.numpy as jnp
from jax import lax
from jax.experimental import pallas as pl
from jax.experimental.pallas import tpu as pltpu
from jax.experimental.pallas import tpu_sc as plsc

F32 = jnp.float32
BF16 = jnp.bfloat16
I32 = jnp.int32

D_MODEL = 1024
D_CONV = 512
CONV_GROUP = 64
CONV_WIDTH = 3
HG_HEADS = 4
HG_KD = 128
HG_VD = 128
D_HG = HG_HEADS * HG_KD
D_IN = 3 * D_CONV + 4 * D_HG
CHUNK = 32
PEER_HEADS = 8
N_KEYS = 128
HALF_Q = 128
PEER_TOPK = 16
N_PAIRS = PEER_HEADS * PEER_TOPK
EPS = 1e-6

LANES = 128
SUBLANES = 8
SEQ_BLOCK = 256
TOK_BLOCK = 128
ROW_WORDS = D_MODEL // (2 * LANES)
ARAW_PITCH = TOK_BLOCK + 1
N_CHUNKS = 16
PACK_ROWS = 512
SC_CORES = 2
SC_SUBCORES = 16
SC_LANES = 16
SC_ROWS = 32
VMEM_LIMIT_MIXER = 48 * 1024 * 1024
VMEM_LIMIT_EXPERT = 52 * 1024 * 1024

NT_DIMS = (((1,), (1,)), ((), ()))


def _bdot(a, b):
    return jnp.dot(a.astype(BF16), b.astype(BF16), preferred_element_type=F32)


def _bdot_nt(a, b):
    return lax.dot_general(a.astype(BF16), b.astype(BF16), NT_DIMS, preferred_element_type=F32)


def _split3(a):
    a1 = a.astype(BF16)
    r1 = a - a1.astype(F32)
    a2 = r1.astype(BF16)
    a3 = (r1 - a2.astype(F32)).astype(BF16)
    return a1, a2, a3


def _dot_sel(sel, a):
    out = None
    for piece in _split3(a):
        d = jnp.dot(sel, piece, preferred_element_type=F32)
        out = d if out is None else out + d
    return out


def _dot_sel_right(a, sel):
    out = None
    for piece in _split3(a):
        d = jnp.dot(piece, sel, preferred_element_type=F32)
        out = d if out is None else out + d
    return out


def _silu(x):
    return x * (1.0 / (1.0 + jnp.exp(-x)))


def _sigmoid(x):
    return 1.0 / (1.0 + jnp.exp(-x))


def _rms(x, gain):
    return x * lax.rsqrt(jnp.mean(x * x, axis=-1, keepdims=True) + EPS) * gain


def _mixer_kernel(x_ref, nmix_ref, win_ref, convw_ref, cgain_ref, lbl_ref, hgain_ref, wout_ref,
                  nffn_ref, wq_ref, keys_ref, cum_ref, gmean_ref, hmean_ref,
                  x1_ref, h2_ref, sc_ref,
                  up_ref, st_ref):
    L = SEQ_BLOCK
    sb = pl.program_id(1)

    @pl.when(sb == 0)
    def _():
        up_ref[0:SUBLANES, :] = jnp.zeros((SUBLANES, D_CONV), F32)
        st_ref[...] = jnp.zeros(st_ref.shape, F32)

    x = x_ref[0]
    h = _rms(x, nmix_ref[...])
    proj = _bdot(h, win_ref[...])

    u = proj[:, 0:D_CONV] * proj[:, D_CONV:2 * D_CONV]
    up_ref[SUBLANES:SUBLANES + L, :] = u
    u1 = up_ref[SUBLANES - 1:SUBLANES - 1 + L, :]
    u2 = up_ref[SUBLANES - 2:SUBLANES - 2 + L, :]
    cw = convw_ref[...]
    y = cw[0:1, :] * u2 + cw[1:2, :] * u1 + cw[2:3, :] * u
    y = proj[:, 2 * D_CONV:3 * D_CONV] * y
    up_ref[0:SUBLANES, :] = up_ref[L:L + SUBLANES, :]
    gms = _dot_sel_right(y * y, gmean_ref[...])
    conv_out = y * lax.rsqrt(gms + EPS) * cgain_ref[...]

    o0 = 3 * D_CONV
    qp = proj[:, o0:o0 + D_HG]
    fp = proj[:, o0 + D_HG:o0 + 2 * D_HG]
    vv = proj[:, o0 + 2 * D_HG:o0 + 3 * D_HG]
    gg = proj[:, o0 + 3 * D_HG:o0 + 4 * D_HG]
    lbl = lbl_ref[...]
    lbe = jnp.exp(lbl - jnp.max(lbl, axis=0, keepdims=True))
    lb = lbe[0:1, :] / jnp.sum(lbe, axis=0, keepdims=True)
    qh = _silu(qp) * (HG_KD ** -0.5)
    f = lb + (1.0 - lb) * _sigmoid(fp)
    kh = 1.0 - f
    logf = jnp.log(f)
    cums = _dot_sel(cum_ref[...], logf)
    gcum = cums[0:L, :]
    glast = cums[L:2 * L, :]
    q_dec = qh * jnp.exp(gcum)
    k_dec = kh * jnp.exp(-gcum)
    k_end = kh * jnp.exp(glast - gcum)
    decay = jnp.exp(glast)

    row = lax.broadcasted_iota(I32, (L, L), 0)
    col = lax.broadcasted_iota(I32, (L, L), 1)
    causal = (row // CHUNK == col // CHUNK) & (col <= row)

    o_heads = []
    for hh in range(HG_HEADS):
        cs = slice(hh * HG_KD, (hh + 1) * HG_KD)
        qd, kd, ke, vh, dc = q_dec[:, cs], k_dec[:, cs], k_end[:, cs], vv[:, cs], decay[:, cs]
        a = _bdot_nt(qd, kd)
        a = jnp.where(causal, a, 0.0)
        o_intra = _bdot(a, vh)
        st = st_ref[hh]
        pieces = []
        for n in range(L // CHUNK):
            rs = slice(n * CHUNK, (n + 1) * CHUNK)
            pieces.append(_bdot_nt(qd[rs], st))
            st = dc[n * CHUNK:n * CHUNK + 1, :] * st + _bdot(vh[rs].T, ke[rs])
        st_ref[hh] = st
        o_heads.append(o_intra + jnp.concatenate(pieces, axis=0))
    o = jnp.concatenate(o_heads, axis=1)
    oms = _dot_sel_right(o * o, hmean_ref[...])
    o = o * lax.rsqrt(oms + EPS) * hgain_ref[...]
    o = o * _silu(gg)

    mix = jnp.concatenate([conv_out, o], axis=1)
    x1 = x + _bdot(mix, wout_ref[...])
    x1_ref[0] = x1
    h2 = _rms(x1, nffn_ref[...])
    h2_ref[0] = h2.astype(BF16)
    q2 = _bdot(h2, wq_ref[...])
    for hp in range(2 * PEER_HEADS):
        sc_ref[hp] = _bdot_nt(keys_ref[hp], q2[:, hp * HALF_Q:(hp + 1) * HALF_Q])


def _mixer_call(x, norm_mix, w_in, conv_w, conv_gain, lb_logits, hg_gain, w_out, norm_ffn, wq, keys):
    B, S, _ = x.shape
    L = SEQ_BLOCK
    nsb = S // L
    T = B * S
    r = np.arange(L)
    same = (r[:, None] // CHUNK) == (r[None, :] // CHUNK)
    cum = np.concatenate([same & (r[None, :] <= r[:, None]), same], axis=0).astype(np.float32)
    c = np.arange(D_CONV)
    gmean = ((c[:, None] // CONV_GROUP) == (c[None, :] // CONV_GROUP)).astype(np.float32) / CONV_GROUP
    hmean = ((c[:, None] // HG_VD) == (c[None, :] // HG_VD)).astype(np.float32) / HG_VD

    const = lambda shape: pl.BlockSpec(shape, lambda b, s: (0,) * len(shape))
    return pl.pallas_call(
        _mixer_kernel,
        grid=(B, nsb),
        in_specs=[
            pl.BlockSpec((1, L, D_MODEL), lambda b, s: (b, s, 0)),
            const((1, D_MODEL)), const((D_MODEL, D_IN)), const((CONV_WIDTH, D_CONV)), const((1, D_CONV)),
            const(lb_logits.shape), const((1, D_HG)), const((D_MODEL, D_MODEL)), const((1, D_MODEL)),
            const((D_MODEL, 2 * PEER_HEADS * HALF_Q)), const((2 * PEER_HEADS, N_KEYS, HALF_Q)),
            const((2 * L, L)), const((D_CONV, D_CONV)), const((D_HG, D_HG)),
        ],
        out_specs=[
            pl.BlockSpec((1, L, D_MODEL), lambda b, s: (b, s, 0)),
            pl.BlockSpec((1, L, D_MODEL), lambda b, s: (b, s, 0)),
            pl.BlockSpec((2 * PEER_HEADS, N_KEYS, L), lambda b, s: (0, 0, b * nsb + s)),
        ],
        out_shape=[
            jax.ShapeDtypeStruct((B, S, D_MODEL), F32),
            jax.ShapeDtypeStruct((B, S, D_MODEL), BF16),
            jax.ShapeDtypeStruct((2 * PEER_HEADS, N_KEYS, T), F32),
        ],
        scratch_shapes=[
            pltpu.VMEM((L + 2 * SUBLANES, D_CONV), F32),
            pltpu.VMEM((HG_HEADS, HG_VD, HG_KD), F32),
        ],
        compiler_params=pltpu.CompilerParams(
            dimension_semantics=("arbitrary", "arbitrary"), vmem_limit_bytes=VMEM_LIMIT_MIXER),
        name="mixer",
    )(x, norm_mix.reshape(1, D_MODEL), w_in.astype(BF16), conv_w, conv_gain.reshape(1, D_CONV), lb_logits,
      jnp.tile(hg_gain.reshape(1, HG_VD), (1, HG_HEADS)), w_out.astype(BF16), norm_ffn.reshape(1, D_MODEL),
      wq.astype(BF16), keys.reshape(2 * PEER_HEADS, N_KEYS, HALF_Q).astype(BF16),
      jnp.asarray(cum, BF16), jnp.asarray(gmean, BF16), jnp.asarray(hmean, BF16))


def _extract_topk(vals, ids, n, sentinel):
    tops, sel = [], []
    for _ in range(n):
        m = jnp.max(vals, axis=0, keepdims=True)
        i = jnp.min(jnp.where(vals == m, ids, sentinel), axis=0, keepdims=True)
        tops.append(m)
        sel.append(i)
        vals = jnp.where(ids == i, -jnp.inf, vals)
    return jnp.concatenate(tops, axis=0), jnp.concatenate(sel, axis=0)


def _topk_head(hd, sc_ref, gate_ref, exp_ref):
    TB = TOK_BLOCK
    K = PEER_TOPK
    iota_keys = lax.broadcasted_iota(I32, (N_KEYS, TB), 0).astype(F32)
    sub = lax.broadcasted_iota(I32, (SUBLANES, TB), 0)
    subf = sub.astype(F32)
    s1, i1t = _extract_topk(sc_ref[2 * hd], iota_keys, K, N_KEYS)
    s2, i2t = _extract_topk(sc_ref[2 * hd + 1], iota_keys, K, N_KEYS)
    vals, ids = [], []
    for b0 in range(0, K, SUBLANES):
        vals.append(s1[0:1, :] + s2[b0:b0 + SUBLANES, :])
        ids.append(subf + b0)
    for a in range(1, SUBLANES):
        v = s1[a:a + 1, :] + s2[0:SUBLANES, :]
        vals.append(jnp.where(sub < K // (a + 1), v, -jnp.inf))
        ids.append(subf + a * K)
    vals.append(s1[SUBLANES:K, :] + s2[0:1, :])
    ids.append((subf + SUBLANES) * K)
    c_score, c_idx = _extract_topk(jnp.concatenate(vals, axis=0), jnp.concatenate(ids, axis=0), K, K * K)
    c_idx = c_idx.astype(I32)
    i1t = i1t.astype(I32)
    i2t = i2t.astype(I32)
    a_sel = lax.shift_right_logical(c_idx, 4)
    b_sel = c_idx & (K - 1)
    i1 = jnp.zeros((K, TB), I32)
    i2 = jnp.zeros((K, TB), I32)
    for a in range(K):
        i1 = i1 + jnp.where(a_sel == a, i1t[a:a + 1, :], 0)
        i2 = i2 + jnp.where(b_sel == a, i2t[a:a + 1, :], 0)
    e = jnp.exp(c_score - jnp.max(c_score, axis=0, keepdims=True))
    row0 = pl.multiple_of(hd * K, K)
    gate_ref[0, pl.ds(row0, K), :] = e / jnp.sum(e, axis=0, keepdims=True)
    exp_ref[pl.ds(row0, K), :] = (i1 * N_KEYS + i2) * ROW_WORDS


def _pack_offsets(exp_ref):
    half = N_PAIRS // 2
    return exp_ref[0:half, :] | (exp_ref[half:N_PAIRS, :] << 16)


def _topk_kernel(sc_ref, idx_ref, gate_ref, exp_ref):
    def head(hd, c):
        _topk_head(hd, sc_ref, gate_ref, exp_ref)
        return c

    lax.fori_loop(0, PEER_HEADS, head, 0)
    idx_ref[0] = _pack_offsets(exp_ref)


def _topk_call(scores_t, blk0, nblk):
    return pl.pallas_call(
        _topk_kernel,
        grid=(nblk,),
        in_specs=[pl.BlockSpec((2 * PEER_HEADS, N_KEYS, TOK_BLOCK), lambda i: (0, 0, i + blk0))],
        out_specs=[
            pl.BlockSpec((1, N_PAIRS // 2, TOK_BLOCK), lambda i: (i, 0, 0)),
            pl.BlockSpec((1, N_PAIRS, TOK_BLOCK), lambda i: (i, 0, 0)),
        ],
        out_shape=[
            jax.ShapeDtypeStruct((nblk, N_PAIRS // 2, TOK_BLOCK), I32),
            jax.ShapeDtypeStruct((nblk, N_PAIRS, TOK_BLOCK), F32),
        ],
        scratch_shapes=[pltpu.VMEM((N_PAIRS, TOK_BLOCK), I32)],
        compiler_params=pltpu.CompilerParams(dimension_semantics=("arbitrary",)),
        name="topk",
    )(scores_t)


def _pack_kernel(tab_ref, out_ref):
    rows = tab_ref.shape[0]
    bits = lambda v: lax.bitcast_convert_type(v.astype(BF16).astype(F32), jnp.uint32)
    for i in range(ROW_WORDS):
        lo = bits(tab_ref[:, (2 * i) * LANES:(2 * i + 1) * LANES]) >> 16
        hi = bits(tab_ref[:, (2 * i + 1) * LANES:(2 * i + 2) * LANES]) & jnp.uint32(0xFFFF0000)
        out_ref[pl.ds(i, rows, stride=ROW_WORDS), :] = lax.bitcast_convert_type(lo | hi, I32)


def _pack_table(tab):
    n = tab.shape[0]
    return pl.pallas_call(
        _pack_kernel,
        grid=(n // PACK_ROWS,),
        in_specs=[pl.BlockSpec((PACK_ROWS, D_MODEL), lambda i: (i, 0))],
        out_specs=pl.BlockSpec((PACK_ROWS * ROW_WORDS, LANES), lambda i: (i, 0)),
        out_shape=jax.ShapeDtypeStruct((n * ROW_WORDS, LANES), I32),
        compiler_params=pltpu.CompilerParams(dimension_semantics=("arbitrary",)),
        name="pack_table",
    )(tab)


def _load_table(tab_hbm, tab_ref, sem):
    @pl.when(pl.program_id(0) == 0)
    def _():
        copy = pltpu.make_async_copy(tab_hbm, tab_ref, sem)
        copy.start()
        copy.wait()


def _pair_offsets(idx_ref, w, j):
    word = idx_ref[w * TOK_BLOCK + j]
    lo = pl.multiple_of(word & 0xFFFF, ROW_WORDS)
    hi = pl.multiple_of(lax.shift_right_logical(word, 16), ROW_WORDS)
    return ((w, lo), (w + N_PAIRS // 2, hi))


def _gather_rows(idx_ref, tab_ref, g_ref, j):
    for w in range(N_PAIRS // 2):
        for k, off in _pair_offsets(idx_ref, w, j):
            g_ref[pl.ds(k * ROW_WORDS, ROW_WORDS), :] = tab_ref[pl.ds(off, ROW_WORDS), :]


def _two_token_step(consume, gather, g0_ref, g1_ref, j):
    consume(g0_ref, j)
    consume(g1_ref, j + 1)
    gather(g0_ref, jnp.minimum(j + 2, TOK_BLOCK - 1))
    gather(g1_ref, jnp.minimum(j + 3, TOK_BLOCK - 1))


def _expert_u_kernel(idx_ref, tab_hbm, h2_ref, gate_ref, gsum_ref, sc_ref,
                     w_ref, idx_next_ref, gate_next_ref,
                     tab_ref, tab_sem, g0_ref, g1_ref, xs_ref, araw_ref, exp_ref):
    TB = TOK_BLOCK
    nchunk = D_MODEL // LANES
    group = 2 * LANES // nchunk
    pitch = ARAW_PITCH
    _load_table(tab_hbm, tab_ref, tab_sem)

    for c in range(nchunk):
        xs_ref[pl.ds(c, TB, stride=nchunk), :] = h2_ref[:, c * LANES:(c + 1) * LANES].astype(F32)

    def dots(g_ref, j):
        xj = xs_ref[pl.ds(pl.multiple_of(j * nchunk, nchunk), nchunk), :].astype(BF16)
        for g in range(N_PAIRS // group):
            wgt = pltpu.bitcast(g_ref[pl.ds(g * LANES, LANES), :], BF16)
            r = lax.dot_general(xj, wgt, NT_DIMS, preferred_element_type=F32)
            for half in range(2):
                cc = 2 * g + half
                araw_ref[pl.ds(cc * nchunk * pitch + j, nchunk, stride=pitch), :] = r[:, half * LANES:(half + 1) * LANES]

    gather = functools.partial(_gather_rows, idx_ref, tab_ref)
    tokens_per_head = TB // PEER_HEADS
    gather(g0_ref, 0)
    gather(g1_ref, 1)

    def head_step(hd, c):
        _topk_head(hd, sc_ref, gate_next_ref, exp_ref)
        for s in range(tokens_per_head // 2):
            _two_token_step(dots, gather, g0_ref, g1_ref, hd * tokens_per_head + 2 * s)
        return c

    lax.fori_loop(0, PEER_HEADS, head_step, 0)
    idx_next_ref[0] = _pack_offsets(exp_ref)

    lane_chunk = lax.broadcasted_iota(I32, (TB, LANES), 1) % nchunk
    a = jnp.zeros((TB, N_PAIRS), F32)
    for cc in range(nchunk):
        part = jnp.zeros((TB, LANES), F32)
        for c in range(nchunk):
            part = part + jnp.where(lane_chunk == c, araw_ref[pl.ds((cc * nchunk + c) * pitch, TB), :], 0.0)
        a = a + _dot_sel_right(part, gsum_ref[cc * LANES:(cc + 1) * LANES, :])
    w_ref[...] = gate_ref[0].T * (0.5 * a * (1.0 + lax.erf(a * (2.0 ** -0.5))))


def _pair_expand():
    pair_of_col = np.arange(D_MODEL) // SUBLANES
    return (np.arange(N_PAIRS)[:, None] == pair_of_col[None, :]).astype(np.float32)


def _expert_u_call(idx_words, tab, h2, gates, scores_t, blk0, nblk, next_blk0):
    nchunk = D_MODEL // LANES
    return pl.pallas_call(
        _expert_u_kernel,
        grid=(nblk,),
        in_specs=[
            pl.BlockSpec((N_PAIRS // 2 * TOK_BLOCK,), lambda i: (i,), memory_space=pltpu.SMEM),
            pl.BlockSpec(memory_space=pl.ANY),
            pl.BlockSpec((TOK_BLOCK, D_MODEL), lambda i: (i + blk0, 0)),
            pl.BlockSpec((1, N_PAIRS, TOK_BLOCK), lambda i: (i, 0, 0)),
            pl.BlockSpec((D_MODEL, N_PAIRS), lambda i: (0, 0)),
            pl.BlockSpec((2 * PEER_HEADS, N_KEYS, TOK_BLOCK), lambda i: (0, 0, i + next_blk0)),
        ],
        out_specs=[
            pl.BlockSpec((TOK_BLOCK, N_PAIRS), lambda i: (i, 0)),
            pl.BlockSpec((1, N_PAIRS // 2, TOK_BLOCK), lambda i: (i, 0, 0)),
            pl.BlockSpec((1, N_PAIRS, TOK_BLOCK), lambda i: (i, 0, 0)),
        ],
        out_shape=[
            jax.ShapeDtypeStruct((nblk * TOK_BLOCK, N_PAIRS), F32),
            jax.ShapeDtypeStruct((nblk, N_PAIRS // 2, TOK_BLOCK), I32),
            jax.ShapeDtypeStruct((nblk, N_PAIRS, TOK_BLOCK), F32),
        ],
        scratch_shapes=[
            pltpu.VMEM(tab.shape, I32),
            pltpu.SemaphoreType.DMA,
            pltpu.VMEM((N_PAIRS * ROW_WORDS, LANES), I32),
            pltpu.VMEM((N_PAIRS * ROW_WORDS, LANES), I32),
            pltpu.VMEM((TOK_BLOCK * nchunk, LANES), F32),
            pltpu.VMEM((nchunk * nchunk * ARAW_PITCH, LANES), F32),
            pltpu.VMEM((N_PAIRS, TOK_BLOCK), I32),
        ],
        compiler_params=pltpu.CompilerParams(
            dimension_semantics=("arbitrary",), vmem_limit_bytes=VMEM_LIMIT_EXPERT),
        name="expert_u",
    )(idx_words, tab, h2, gates, jnp.asarray(_pair_expand().T, BF16), scores_t)


def _sc_weighted_rows_call(table_words, ids, w):
    ts = ids.shape[0]
    row_len = table_words.shape[1]
    workers = SC_CORES * SC_SUBCORES
    tpw = ts // workers
    nbatch = N_PAIRS // SC_ROWS
    groups = LANES // SC_LANES
    mesh = plsc.VectorSubcoreMesh(core_axis_name="c", subcore_axis_name="s")
    params = pltpu.CompilerParams()
    if "needs_layout_passes" in pltpu.CompilerParams.__dataclass_fields__:
        params = dataclasses.replace(params, needs_layout_passes=False)

    @functools.partial(
        pl.kernel, mesh=mesh, compiler_params=params,
        out_type=jax.ShapeDtypeStruct((ts, D_MODEL), F32),
        scratch_types=[
            pltpu.VMEM((N_PAIRS,), I32), pltpu.VMEM((N_PAIRS,), I32),
            pltpu.VMEM((N_PAIRS,), F32), pltpu.VMEM((N_PAIRS,), F32),
            pltpu.VMEM((SC_ROWS, row_len), I32), pltpu.VMEM((SC_ROWS, row_len), I32),
            pltpu.VMEM((D_MODEL,), F32),
            pltpu.SemaphoreType.DMA, pltpu.SemaphoreType.DMA, pltpu.SemaphoreType.DMA, pltpu.SemaphoreType.DMA,
        ],
    )
    def sc_kernel(table_hbm, ids_hbm, w_hbm, out_hbm, ids0, ids1, w0, w1, rows0, rows1, y_v,
                  sem0, sem1, sem_ids, sem_w):
        worker = lax.axis_index("s") * SC_CORES + lax.axis_index("c")
        base = worker * tpw
        ids_v, w_v, rows_v, sems = (ids0, ids1), (w0, w1), (rows0, rows1), (sem0, sem1)

        def gather(slot, b, buf):
            src = table_hbm.at[ids_v[slot].at[pl.ds(b * SC_ROWS, SC_ROWS)]]
            return pltpu.make_async_copy(src, rows_v[buf], sems[buf])

        def fetch(slot, tok):
            return (pltpu.make_async_copy(ids_hbm.at[tok], ids_v[slot], sem_ids),
                    pltpu.make_async_copy(w_hbm.at[tok], w_v[slot], sem_w))

        def accumulate(slot, b, buf):
            wvs = [w_v[slot][pl.ds(b * SC_ROWS + q * SC_LANES, SC_LANES)] for q in range(SC_ROWS // SC_LANES)]
            wrs = [jnp.broadcast_to(wvs[r // SC_LANES][r % SC_LANES], (SC_LANES,)) for r in range(SC_ROWS)]

            def group(g, c):
                off = pl.multiple_of(g * SC_LANES, SC_LANES)
                f_lo = pl.multiple_of((2 * (g // groups)) * LANES + (g % groups) * SC_LANES, SC_LANES)
                f_hi = pl.multiple_of(f_lo + LANES, SC_LANES)
                if b == 0:
                    acc_lo = jnp.zeros((SC_LANES,), F32)
                    acc_hi = jnp.zeros((SC_LANES,), F32)
                else:
                    acc_lo = y_v[pl.ds(f_lo, SC_LANES)]
                    acc_hi = y_v[pl.ds(f_hi, SC_LANES)]
                for r in range(SC_ROWS):
                    word = rows_v[buf][r, pl.ds(off, SC_LANES)]
                    acc_lo = acc_lo + wrs[r] * lax.bitcast_convert_type(word << 16, F32)
                    acc_hi = acc_hi + wrs[r] * lax.bitcast_convert_type(word & jnp.int32(-65536), F32)
                y_v[pl.ds(f_lo, SC_LANES)] = acc_lo
                y_v[pl.ds(f_hi, SC_LANES)] = acc_hi
                return c

            lax.fori_loop(0, row_len // SC_LANES, group, 0)

        for copy in fetch(0, base):
            copy.start()
        for copy in fetch(0, base):
            copy.wait()
        gather(0, 0, 0).start()

        def token_pair(p, c):
            t0 = base + 2 * p
            t_next = base + jnp.minimum(2 * p + 2, tpw - 1)
            for q in range(2 * nbatch):
                slot, b, buf = q // nbatch, q % nbatch, q % 2
                if q == 0:
                    for copy in fetch(1, t0 + 1):
                        copy.start()
                if q == nbatch:
                    for copy in fetch(0, t_next):
                        copy.start()
                if q == nbatch - 1:
                    for copy in fetch(1, t0 + 1):
                        copy.wait()
                if q == 2 * nbatch - 1:
                    for copy in fetch(0, t_next):
                        copy.wait()
                nq = (q + 1) % (2 * nbatch)
                gather(nq // nbatch, nq % nbatch, nq % 2).start()
                gather(slot, b, buf).wait()
                accumulate(slot, b, buf)
                if b == nbatch - 1:
                    pltpu.sync_copy(y_v, out_hbm.at[t0 + slot])
            return c

        lax.fori_loop(0, tpw // 2, token_pair, 0)
        gather(0, 0, 0).wait()

    return sc_kernel(table_words, ids, w)


def _residual_norm_kernel(*refs):
    x1_ref, y_ref, nf_ref, out_ref = refs[-4:]
    out_ref[...] = _rms(x1_ref[...] + y_ref[...], nf_ref[...])


def _residual_norm_call(out_prev, x1, y, norm_f, blk0):
    nblk = y.shape[0] // TOK_BLOCK
    here = pl.BlockSpec((TOK_BLOCK, D_MODEL), lambda i: (i + blk0, 0))
    in_specs = [here, pl.BlockSpec((TOK_BLOCK, D_MODEL), lambda i: (i, 0)), pl.BlockSpec((1, D_MODEL), lambda i: (0, 0))]
    args = (x1, y, norm_f.reshape(1, D_MODEL))
    aliases = {}
    if out_prev is not None:
        in_specs = [pl.BlockSpec(memory_space=pl.ANY)] + in_specs
        args = (out_prev,) + args
        aliases = {0: 0}
    return pl.pallas_call(
        _residual_norm_kernel,
        grid=(nblk,),
        in_specs=in_specs,
        out_specs=here,
        out_shape=jax.ShapeDtypeStruct(x1.shape, F32),
        input_output_aliases=aliases,
        compiler_params=pltpu.CompilerParams(dimension_semantics=("arbitrary",)),
        name="residual_norm",
    )(*args)


def _expert_ids(idx_words):
    lo = (idx_words & 0xFFFF) // ROW_WORDS
    hi = lax.shift_right_logical(idx_words, 16) // ROW_WORDS
    ids = jnp.concatenate([lo, hi], axis=1)
    return ids.transpose(0, 2, 1).reshape(-1, N_PAIRS)


def kernel(x, norm_mix, w_in, conv_w, conv_gain, hg_lb_logits, hg_gain, w_out, norm_ffn, peer_wq, peer_keys,
           peer_u, peer_v, norm_f):
    B, S, D = x.shape
    depth = norm_mix.shape[0]
    assert depth == 1 and D == D_MODEL and S % SEQ_BLOCK == 0 and (B * S) % (TOK_BLOCK * N_CHUNKS) == 0
    T = B * S
    nb = T // (TOK_BLOCK * N_CHUNKS)
    x1, h2, scores_t = _mixer_call(x, norm_mix[0], w_in[0], conv_w[0], conv_gain[0], hg_lb_logits, hg_gain[0],
                                   w_out[0], norm_ffn[0], peer_wq[0], peer_keys[0])
    tab_u = _pack_table(peer_u[0])
    tab_v = _pack_table(peer_v[0]).reshape(peer_v.shape[1], D // 2)
    h2 = h2.reshape(T, D)
    x1 = x1.reshape(T, D)
    idx_k, gates_k = _topk_call(scores_t, 0, nb)
    out = None
    for k in range(N_CHUNKS):
        nxt = min(k + 1, N_CHUNKS - 1)
        ids_k = _expert_ids(idx_k)
        w_k, idx_k, gates_k = _expert_u_call(idx_k.reshape(-1), tab_u, h2, gates_k, scores_t,
                                             k * nb, nb, nxt * nb)
        y_k = _sc_weighted_rows_call(tab_v, ids_k, w_k)
        out = _residual_norm_call(out, x1, y_k, norm_f, k * nb)
    return out.reshape(B, S, D)
```

```python
import dataclasses
import functools

import numpy as np
import jax
import jax.numpy as jnp
from jax import lax
from jax.experimental import pallas as pl
from jax.experimental.pallas import tpu as pltpu
from jax.experimental.pallas import tpu_sc as plsc

F32 = jnp.float32
BF16 = jnp.bfloat16
I32 = jnp.int32

D_MODEL = 1024
D_CONV = 512
CONV_GROUP = 64
CONV_WIDTH = 3
HG_HEADS = 4
HG_KD = 128
HG_VD = 128
D_HG = HG_HEADS * HG_KD
D_IN = 3 * D_CONV + 4 * D_HG
CHUNK = 32
PEER_HEADS = 8
N_KEYS = 128
HALF_Q = 128
PEER_TOPK = 16
N_PAIRS = PEER_HEADS * PEER_TOPK
EPS = 1e-6

LANES = 128
SUBLANES = 8
SEQ_BLOCK = 256
TOK_BLOCK = 128
ROW_WORDS = D_MODEL // (2 * LANES)
ARAW_PITCH = TOK_BLOCK + 1
N_CHUNKS = 16
PACK_ROWS = 512
SC_CORES = 2
SC_SUBCORES = 16
SC_LANES = 16
SC_ROWS = 32
VMEM_LIMIT_MIXER = 48 * 1024 * 1024
VMEM_LIMIT_EXPERT = 52 * 1024 * 1024

NT_DIMS = (((1,), (1,)), ((), ()))


def _bdot(a, b):
    return jnp.dot(a.astype(BF16), b.astype(BF16), preferred_element_type=F32)


def _bdot_nt(a, b):
    return lax.dot_general(a.astype(BF16), b.astype(BF16), NT_DIMS, preferred_element_type=F32)


def _split3(a):
    a1 = a.astype(BF16)
    r1 = a - a1.astype(F32)
    a2 = r1.astype(BF16)
    a3 = (r1 - a2.astype(F32)).astype(BF16)
    return a1, a2, a3


def _dot_sel(sel, a):
    out = None
    for piece in _split3(a):
        d = jnp.dot(sel, piece, preferred_element_type=F32)
        out = d if out is None else out + d
    return out


def _dot_sel_right(a, sel):
    out = None
    for piece in _split3(a):
        d = jnp.dot(piece, sel, preferred_element_type=F32)
        out = d if out is None else out + d
    return out


def _silu(x):
    return x * (1.0 / (1.0 + jnp.exp(-x)))


def _sigmoid(x):
    return 1.0 / (1.0 + jnp.exp(-x))


def _rms(x, gain):
    return x * lax.rsqrt(jnp.mean(x * x, axis=-1, keepdims=True) + EPS) * gain


def _mixer_kernel(x_ref, nmix_ref, win_ref, convw_ref, cgain_ref, lbl_ref, hgain_ref, wout_ref,
                  nffn_ref, wq_ref, keys_ref, cum_ref, gmean_ref, hmean_ref,
                  x1_ref, h2_ref, sc_ref,
                  up_ref, st_ref):
    L = SEQ_BLOCK
    sb = pl.program_id(1)

    @pl.when(sb == 0)
    def _():
        up_ref[0:SUBLANES, :] = jnp.zeros((SUBLANES, D_CONV), F32)
        st_ref[...] = jnp.zeros(st_ref.shape, F32)

    x = x_ref[0]
    h = _rms(x, nmix_ref[...])
    proj = _bdot(h, win_ref[...])

    u = proj[:, 0:D_CONV] * proj[:, D_CONV:2 * D_CONV]
    up_ref[SUBLANES:SUBLANES + L, :] = u
    u1 = up_ref[SUBLANES - 1:SUBLANES - 1 + L, :]
    u2 = up_ref[SUBLANES - 2:SUBLANES - 2 + L, :]
    cw = convw_ref[...]
    y = cw[0:1, :] * u2 + cw[1:2, :] * u1 + cw[2:3, :] * u
    y = proj[:, 2 * D_CONV:3 * D_CONV] * y
    up_ref[0:SUBLANES, :] = up_ref[L:L + SUBLANES, :]
    gms = _dot_sel_right(y * y, gmean_ref[...])
    conv_out = y * lax.rsqrt(gms + EPS) * cgain_ref[...]

    o0 = 3 * D_CONV
    qp = proj[:, o0:o0 + D_HG]
    fp = proj[:, o0 + D_HG:o0 + 2 * D_HG]
    vv = proj[:, o0 + 2 * D_HG:o0 + 3 * D_HG]
    gg = proj[:, o0 + 3 * D_HG:o0 + 4 * D_HG]
    lbl = lbl_ref[...]
    lbe = jnp.exp(lbl - jnp.max(lbl, axis=0, keepdims=True))
    lb = lbe[0:1, :] / jnp.sum(lbe, axis=0, keepdims=True)
    qh = _silu(qp) * (HG_KD ** -0.5)
    f = lb + (1.0 - lb) * _sigmoid(fp)
    kh = 1.0 - f
    logf = jnp.log(f)
    cums = _dot_sel(cum_ref[...], logf)
    gcum = cums[0:L, :]
    glast = cums[L:2 * L, :]
    q_dec = qh * jnp.exp(gcum)
    k_dec = kh * jnp.exp(-gcum)
    k_end = kh * jnp.exp(glast - gcum)
    decay = jnp.exp(glast)

    row = lax.broadcasted_iota(I32, (L, L), 0)
    col = lax.broadcasted_iota(I32, (L, L), 1)
    causal = (row // CHUNK == col // CHUNK) & (col <= row)

    o_heads = []
    for hh in range(HG_HEADS):
        cs = slice(hh * HG_KD, (hh + 1) * HG_KD)
        qd, kd, ke, vh, dc = q_dec[:, cs], k_dec[:, cs], k_end[:, cs], vv[:, cs], decay[:, cs]
        a = _bdot_nt(qd, kd)
        a = jnp.where(causal, a, 0.0)
        o_intra = _bdot(a, vh)
        st = st_ref[hh]
        pieces = []
        for n in range(L // CHUNK):
            rs = slice(n * CHUNK, (n + 1) * CHUNK)
            pieces.append(_bdot_nt(qd[rs], st))
            st = dc[n * CHUNK:n * CHUNK + 1, :] * st + _bdot(vh[rs].T, ke[rs])
        st_ref[hh] = st
        o_heads.append(o_intra + jnp.concatenate(pieces, axis=0))
    o = jnp.concatenate(o_heads, axis=1)
    oms = _dot_sel_right(o * o, hmean_ref[...])
    o = o * lax.rsqrt(oms + EPS) * hgain_ref[...]
    o = o * _silu(gg)

    mix = jnp.concatenate([conv_out, o], axis=1)
    x1 = x + _bdot(mix, wout_ref[...])
    x1_ref[0] = x1
    h2 = _rms(x1, nffn_ref[...])
    h2_ref[0] = h2.astype(BF16)
    q2 = _bdot(h2, wq_ref[...])
    for hp in range(2 * PEER_HEADS):
        sc_ref[hp] = _bdot_nt(keys_ref[hp], q2[:, hp * HALF_Q:(hp + 1) * HALF_Q])


def _mixer_params(norm_mix, w_in, conv_w, conv_gain, lb_logits, hg_gain, w_out, norm_ffn, wq, keys):
    L = SEQ_BLOCK
    r = np.arange(L)
    same = (r[:, None] // CHUNK) == (r[None, :] // CHUNK)
    cum = np.concatenate([same & (r[None, :] <= r[:, None]), same], axis=0).astype(np.float32)
    c = np.arange(D_CONV)
    gmean = ((c[:, None] // CONV_GROUP) == (c[None, :] // CONV_GROUP)).astype(np.float32) / CONV_GROUP
    hmean = ((c[:, None] // HG_VD) == (c[None, :] // HG_VD)).astype(np.float32) / HG_VD
    return (norm_mix.reshape(1, D_MODEL), w_in.astype(BF16), conv_w, conv_gain.reshape(1, D_CONV), lb_logits,
            jnp.tile(hg_gain.reshape(1, HG_VD), (1, HG_HEADS)), w_out.astype(BF16), norm_ffn.reshape(1, D_MODEL),
            wq.astype(BF16), keys.reshape(2 * PEER_HEADS, N_KEYS, HALF_Q).astype(BF16),
            jnp.asarray(cum, BF16), jnp.asarray(gmean, BF16), jnp.asarray(hmean, BF16))


def _mixer_call(x, params, b0, nbatch):
    S = x.shape[1]
    L = SEQ_BLOCK
    nsb = S // L
    const = lambda a: pl.BlockSpec(a.shape, lambda b, s: (0,) * a.ndim)
    return pl.pallas_call(
        _mixer_kernel,
        grid=(nbatch, nsb),
        in_specs=[pl.BlockSpec((1, L, D_MODEL), lambda b, s: (b + b0, s, 0))] + [const(a) for a in params],
        out_specs=[
            pl.BlockSpec((1, L, D_MODEL), lambda b, s: (b, s, 0)),
            pl.BlockSpec((1, L, D_MODEL), lambda b, s: (b, s, 0)),
            pl.BlockSpec((2 * PEER_HEADS, N_KEYS, L), lambda b, s: (0, 0, b * nsb + s)),
        ],
        out_shape=[
            jax.ShapeDtypeStruct((nbatch, S, D_MODEL), F32),
            jax.ShapeDtypeStruct((nbatch, S, D_MODEL), BF16),
            jax.ShapeDtypeStruct((2 * PEER_HEADS, N_KEYS, nbatch * S), F32),
        ],
        scratch_shapes=[
            pltpu.VMEM((L + 2 * SUBLANES, D_CONV), F32),
            pltpu.VMEM((HG_HEADS, HG_VD, HG_KD), F32),
        ],
        compiler_params=pltpu.CompilerParams(
            dimension_semantics=("arbitrary", "arbitrary"), vmem_limit_bytes=VMEM_LIMIT_MIXER),
        name="mixer",
    )(x, *params)


def _extract_topk(vals, ids, n, sentinel):
    tops, sel = [], []
    for _ in range(n):
        m = jnp.max(vals, axis=0, keepdims=True)
        i = jnp.min(jnp.where(vals == m, ids, sentinel), axis=0, keepdims=True)
        tops.append(m)
        sel.append(i)
        vals = jnp.where(ids == i, -jnp.inf, vals)
    return jnp.concatenate(tops, axis=0), jnp.concatenate(sel, axis=0)


def _topk_head(hd, sc_ref, gate_ref, exp_ref):
    TB = TOK_BLOCK
    K = PEER_TOPK
    iota_keys = lax.broadcasted_iota(I32, (N_KEYS, TB), 0).astype(F32)
    sub = lax.broadcasted_iota(I32, (SUBLANES, TB), 0)
    subf = sub.astype(F32)
    s1, i1t = _extract_topk(sc_ref[2 * hd], iota_keys, K, N_KEYS)
    s2, i2t = _extract_topk(sc_ref[2 * hd + 1], iota_keys, K, N_KEYS)
    vals, ids = [], []
    for b0 in range(0, K, SUBLANES):
        vals.append(s1[0:1, :] + s2[b0:b0 + SUBLANES, :])
        ids.append(subf + b0)
    for a in range(1, SUBLANES):
        v = s1[a:a + 1, :] + s2[0:SUBLANES, :]
        vals.append(jnp.where(sub < K // (a + 1), v, -jnp.inf))
        ids.append(subf + a * K)
    vals.append(s1[SUBLANES:K, :] + s2[0:1, :])
    ids.append((subf + SUBLANES) * K)
    c_score, c_idx = _extract_topk(jnp.concatenate(vals, axis=0), jnp.concatenate(ids, axis=0), K, K * K)
    c_idx = c_idx.astype(I32)
    i1t = i1t.astype(I32)
    i2t = i2t.astype(I32)
    a_sel = lax.shift_right_logical(c_idx, 4)
    b_sel = c_idx & (K - 1)
    i1 = jnp.zeros((K, TB), I32)
    i2 = jnp.zeros((K, TB), I32)
    for a in range(K):
        i1 = i1 + jnp.where(a_sel == a, i1t[a:a + 1, :], 0)
        i2 = i2 + jnp.where(b_sel == a, i2t[a:a + 1, :], 0)
    e = jnp.exp(c_score - jnp.max(c_score, axis=0, keepdims=True))
    row0 = pl.multiple_of(hd * K, K)
    gate_ref[0, pl.ds(row0, K), :] = e / jnp.sum(e, axis=0, keepdims=True)
    exp_ref[pl.ds(row0, K), :] = (i1 * N_KEYS + i2) * ROW_WORDS


def _pack_offsets(exp_ref):
    half = N_PAIRS // 2
    return exp_ref[0:half, :] | (exp_ref[half:N_PAIRS, :] << 16)


def _topk_kernel(sc_ref, idx_ref, gate_ref, exp_ref):
    def head(hd, c):
        _topk_head(hd, sc_ref, gate_ref, exp_ref)
        return c

    lax.fori_loop(0, PEER_HEADS, head, 0)
    idx_ref[0] = _pack_offsets(exp_ref)


def _topk_call(scores_t):
    nblk = scores_t.shape[-1] // TOK_BLOCK
    return pl.pallas_call(
        _topk_kernel,
        grid=(nblk,),
        in_specs=[pl.BlockSpec((2 * PEER_HEADS, N_KEYS, TOK_BLOCK), lambda i: (0, 0, i))],
        out_specs=[
            pl.BlockSpec((1, N_PAIRS // 2, TOK_BLOCK), lambda i: (i, 0, 0)),
            pl.BlockSpec((1, N_PAIRS, TOK_BLOCK), lambda i: (i, 0, 0)),
        ],
        out_shape=[
            jax.ShapeDtypeStruct((nblk, N_PAIRS // 2, TOK_BLOCK), I32),
            jax.ShapeDtypeStruct((nblk, N_PAIRS, TOK_BLOCK), F32),
        ],
        scratch_shapes=[pltpu.VMEM((N_PAIRS, TOK_BLOCK), I32)],
        compiler_params=pltpu.CompilerParams(dimension_semantics=("arbitrary",)),
        name="topk",
    )(scores_t)


def _pack_kernel(tab_ref, out_ref):
    rows = tab_ref.shape[0]
    bits = lambda v: lax.bitcast_convert_type(v.astype(BF16).astype(F32), jnp.uint32)
    for i in range(ROW_WORDS):
        lo = bits(tab_ref[:, (2 * i) * LANES:(2 * i + 1) * LANES]) >> 16
        hi = bits(tab_ref[:, (2 * i + 1) * LANES:(2 * i + 2) * LANES]) & jnp.uint32(0xFFFF0000)
        out_ref[pl.ds(i, rows, stride=ROW_WORDS), :] = lax.bitcast_convert_type(lo | hi, I32)


def _pack_table(tab):
    n = tab.shape[0]
    return pl.pallas_call(
        _pack_kernel,
        grid=(n // PACK_ROWS,),
        in_specs=[pl.BlockSpec((PACK_ROWS, D_MODEL), lambda i: (i, 0))],
        out_specs=pl.BlockSpec((PACK_ROWS * ROW_WORDS, LANES), lambda i: (i, 0)),
        out_shape=jax.ShapeDtypeStruct((n * ROW_WORDS, LANES), I32),
        compiler_params=pltpu.CompilerParams(dimension_semantics=("arbitrary",)),
        name="pack_table",
    )(tab)


def _load_table(tab_hbm, tab_ref, sem):
    @pl.when(pl.program_id(0) == 0)
    def _():
        copy = pltpu.make_async_copy(tab_hbm, tab_ref, sem)
        copy.start()
        copy.wait()


def _pair_offsets(idx_ref, w, j):
    word = idx_ref[w * TOK_BLOCK + j]
    lo = pl.multiple_of(word & 0xFFFF, ROW_WORDS)
    hi = pl.multiple_of(lax.shift_right_logical(word, 16), ROW_WORDS)
    return ((w, lo), (w + N_PAIRS // 2, hi))


def _gather_rows(idx_ref, tab_ref, g_ref, j):
    for w in range(N_PAIRS // 2):
        for k, off in _pair_offsets(idx_ref, w, j):
            g_ref[pl.ds(k * ROW_WORDS, ROW_WORDS), :] = tab_ref[pl.ds(off, ROW_WORDS), :]


def _two_token_step(consume, gather, g0_ref, g1_ref, j):
    consume(g0_ref, j)
    consume(g1_ref, j + 1)
    gather(g0_ref, jnp.minimum(j + 2, TOK_BLOCK - 1))
    gather(g1_ref, jnp.minimum(j + 3, TOK_BLOCK - 1))


def _expert_u_kernel(idx_ref, tab_hbm, h2_ref, gate_ref, gsum_ref, sc_ref,
                     w_ref, idx_next_ref, gate_next_ref,
                     tab_ref, tab_sem, g0_ref, g1_ref, xs_ref, araw_ref, exp_ref):
    TB = TOK_BLOCK
    nchunk = D_MODEL // LANES
    group = 2 * LANES // nchunk
    pitch = ARAW_PITCH
    _load_table(tab_hbm, tab_ref, tab_sem)

    for c in range(nchunk):
        xs_ref[pl.ds(c, TB, stride=nchunk), :] = h2_ref[:, c * LANES:(c + 1) * LANES].astype(F32)

    def dots(g_ref, j):
        xj = xs_ref[pl.ds(pl.multiple_of(j * nchunk, nchunk), nchunk), :].astype(BF16)
        for g in range(N_PAIRS // group):
            wgt = pltpu.bitcast(g_ref[pl.ds(g * LANES, LANES), :], BF16)
            r = lax.dot_general(xj, wgt, NT_DIMS, preferred_element_type=F32)
            for half in range(2):
                cc = 2 * g + half
                araw_ref[pl.ds(cc * nchunk * pitch + j, nchunk, stride=pitch), :] = r[:, half * LANES:(half + 1) * LANES]

    gather = functools.partial(_gather_rows, idx_ref, tab_ref)
    tokens_per_head = TB // PEER_HEADS
    gather(g0_ref, 0)
    gather(g1_ref, 1)

    def head_step(hd, c):
        _topk_head(hd, sc_ref, gate_next_ref, exp_ref)
        for s in range(tokens_per_head // 2):
            _two_token_step(dots, gather, g0_ref, g1_ref, hd * tokens_per_head + 2 * s)
        return c

    lax.fori_loop(0, PEER_HEADS, head_step, 0)
    idx_next_ref[0] = _pack_offsets(exp_ref)

    lane_chunk = lax.broadcasted_iota(I32, (TB, LANES), 1) % nchunk
    a = jnp.zeros((TB, N_PAIRS), F32)
    for cc in range(nchunk):
        part = jnp.zeros((TB, LANES), F32)
        for c in range(nchunk):
            part = part + jnp.where(lane_chunk == c, araw_ref[pl.ds((cc * nchunk + c) * pitch, TB), :], 0.0)
        a = a + _dot_sel_right(part, gsum_ref[cc * LANES:(cc + 1) * LANES, :])
    w_ref[...] = gate_ref[0].T * (0.5 * a * (1.0 + lax.erf(a * (2.0 ** -0.5))))


def _pair_expand():
    pair_of_col = np.arange(D_MODEL) // SUBLANES
    return (np.arange(N_PAIRS)[:, None] == pair_of_col[None, :]).astype(np.float32)


def _expert_u_call(idx_words, tab, h2, gates, scores_next):
    nchunk = D_MODEL // LANES
    nblk = h2.shape[0] // TOK_BLOCK
    return pl.pallas_call(
        _expert_u_kernel,
        grid=(nblk,),
        in_specs=[
            pl.BlockSpec((N_PAIRS // 2 * TOK_BLOCK,), lambda i: (i,), memory_space=pltpu.SMEM),
            pl.BlockSpec(memory_space=pl.ANY),
            pl.BlockSpec((TOK_BLOCK, D_MODEL), lambda i: (i, 0)),
            pl.BlockSpec((1, N_PAIRS, TOK_BLOCK), lambda i: (i, 0, 0)),
            pl.BlockSpec((D_MODEL, N_PAIRS), lambda i: (0, 0)),
            pl.BlockSpec((2 * PEER_HEADS, N_KEYS, TOK_BLOCK), lambda i: (0, 0, i)),
        ],
        out_specs=[
            pl.BlockSpec((TOK_BLOCK, N_PAIRS), lambda i: (i, 0)),
            pl.BlockSpec((1, N_PAIRS // 2, TOK_BLOCK), lambda i: (i, 0, 0)),
            pl.BlockSpec((1, N_PAIRS, TOK_BLOCK), lambda i: (i, 0, 0)),
        ],
        out_shape=[
            jax.ShapeDtypeStruct((nblk * TOK_BLOCK, N_PAIRS), F32),
            jax.ShapeDtypeStruct((nblk, N_PAIRS // 2, TOK_BLOCK), I32),
            jax.ShapeDtypeStruct((nblk, N_PAIRS, TOK_BLOCK), F32),
        ],
        scratch_shapes=[
            pltpu.VMEM(tab.shape, I32),
            pltpu.SemaphoreType.DMA,
            pltpu.VMEM((N_PAIRS * ROW_WORDS, LANES), I32),
            pltpu.VMEM((N_PAIRS * ROW_WORDS, LANES), I32),
            pltpu.VMEM((TOK_BLOCK * nchunk, LANES), F32),
            pltpu.VMEM((nchunk * nchunk * ARAW_PITCH, LANES), F32),
            pltpu.VMEM((N_PAIRS, TOK_BLOCK), I32),
        ],
        compiler_params=pltpu.CompilerParams(
            dimension_semantics=("arbitrary",), vmem_limit_bytes=VMEM_LIMIT_EXPERT),
        name="expert_u",
    )(idx_words, tab, h2, gates, jnp.asarray(_pair_expand().T, BF16), scores_next)


def _sc_weighted_rows_call(table_words, ids, w):
    ts = ids.shape[0]
    row_len = table_words.shape[1]
    workers = SC_CORES * SC_SUBCORES
    tpw = ts // workers
    nbatch = N_PAIRS // SC_ROWS
    groups = LANES // SC_LANES
    mesh = plsc.VectorSubcoreMesh(core_axis_name="c", subcore_axis_name="s")
    params = pltpu.CompilerParams()
    if "needs_layout_passes" in pltpu.CompilerParams.__dataclass_fields__:
        params = dataclasses.replace(params, needs_layout_passes=False)

    @functools.partial(
        pl.kernel, mesh=mesh, compiler_params=params,
        out_type=jax.ShapeDtypeStruct((ts, D_MODEL), F32),
        scratch_types=[
            pltpu.VMEM((N_PAIRS,), I32), pltpu.VMEM((N_PAIRS,), I32),
            pltpu.VMEM((N_PAIRS,), F32), pltpu.VMEM((N_PAIRS,), F32),
            pltpu.VMEM((SC_ROWS, row_len), I32), pltpu.VMEM((SC_ROWS, row_len), I32),
            pltpu.VMEM((D_MODEL,), F32),
            pltpu.SemaphoreType.DMA, pltpu.SemaphoreType.DMA, pltpu.SemaphoreType.DMA, pltpu.SemaphoreType.DMA,
        ],
    )
    def sc_kernel(table_hbm, ids_hbm, w_hbm, out_hbm, ids0, ids1, w0, w1, rows0, rows1, y_v,
                  sem0, sem1, sem_ids, sem_w):
        worker = lax.axis_index("s") * SC_CORES + lax.axis_index("c")
        base = worker * tpw
        ids_v, w_v, rows_v, sems = (ids0, ids1), (w0, w1), (rows0, rows1), (sem0, sem1)

        def gather(slot, b, buf):
            src = table_hbm.at[ids_v[slot].at[pl.ds(b * SC_ROWS, SC_ROWS)]]
            return pltpu.make_async_copy(src, rows_v[buf], sems[buf])

        def fetch(slot, tok):
            return (pltpu.make_async_copy(ids_hbm.at[tok], ids_v[slot], sem_ids),
                    pltpu.make_async_copy(w_hbm.at[tok], w_v[slot], sem_w))

        def accumulate(slot, b, buf):
            wvs = [w_v[slot][pl.ds(b * SC_ROWS + q * SC_LANES, SC_LANES)] for q in range(SC_ROWS // SC_LANES)]
            wrs = [jnp.broadcast_to(wvs[r // SC_LANES][r % SC_LANES], (SC_LANES,)) for r in range(SC_ROWS)]

            def group(g, c):
                off = pl.multiple_of(g * SC_LANES, SC_LANES)
                f_lo = pl.multiple_of((2 * (g // groups)) * LANES + (g % groups) * SC_LANES, SC_LANES)
                f_hi = pl.multiple_of(f_lo + LANES, SC_LANES)
                if b == 0:
                    acc_lo = jnp.zeros((SC_LANES,), F32)
                    acc_hi = jnp.zeros((SC_LANES,), F32)
                else:
                    acc_lo = y_v[pl.ds(f_lo, SC_LANES)]
                    acc_hi = y_v[pl.ds(f_hi, SC_LANES)]
                for r in range(SC_ROWS):
                    word = rows_v[buf][r, pl.ds(off, SC_LANES)]
                    acc_lo = acc_lo + wrs[r] * lax.bitcast_convert_type(word << 16, F32)
                    acc_hi = acc_hi + wrs[r] * lax.bitcast_convert_type(word & jnp.int32(-65536), F32)
                y_v[pl.ds(f_lo, SC_LANES)] = acc_lo
                y_v[pl.ds(f_hi, SC_LANES)] = acc_hi
                return c

            lax.fori_loop(0, row_len // SC_LANES, group, 0)

        for copy in fetch(0, base):
            copy.start()
        for copy in fetch(0, base):
            copy.wait()
        gather(0, 0, 0).start()

        def token_pair(p, c):
            t0 = base + 2 * p
            t_next = base + jnp.minimum(2 * p + 2, tpw - 1)
            for q in range(2 * nbatch):
                slot, b, buf = q // nbatch, q % nbatch, q % 2
                if q == 0:
                    for copy in fetch(1, t0 + 1):
                        copy.start()
                if q == nbatch:
                    for copy in fetch(0, t_next):
                        copy.start()
                if q == nbatch - 1:
                    for copy in fetch(1, t0 + 1):
                        copy.wait()
                if q == 2 * nbatch - 1:
                    for copy in fetch(0, t_next):
                        copy.wait()
                nq = (q + 1) % (2 * nbatch)
                gather(nq // nbatch, nq % nbatch, nq % 2).start()
                gather(slot, b, buf).wait()
                accumulate(slot, b, buf)
                if b == nbatch - 1:
                    pltpu.sync_copy(y_v, out_hbm.at[t0 + slot])
            return c

        lax.fori_loop(0, tpw // 2, token_pair, 0)
        gather(0, 0, 0).wait()

    return sc_kernel(table_words, ids, w)


def _residual_norm_kernel(*refs):
    x1_ref, y_ref, nf_ref, out_ref = refs[-4:]
    out_ref[...] = _rms(x1_ref[...] + y_ref[...], nf_ref[...])


def _residual_norm_call(out_prev, x1, y, norm_f, blk0, total_tokens):
    nblk = y.shape[0] // TOK_BLOCK
    here = pl.BlockSpec((TOK_BLOCK, D_MODEL), lambda i: (i + blk0, 0))
    local = pl.BlockSpec((TOK_BLOCK, D_MODEL), lambda i: (i, 0))
    in_specs = [local, local, pl.BlockSpec((1, D_MODEL), lambda i: (0, 0))]
    args = (x1, y, norm_f.reshape(1, D_MODEL))
    aliases = {}
    if out_prev is not None:
        in_specs = [pl.BlockSpec(memory_space=pl.ANY)] + in_specs
        args = (out_prev,) + args
        aliases = {0: 0}
    return pl.pallas_call(
        _residual_norm_kernel,
        grid=(nblk,),
        in_specs=in_specs,
        out_specs=here,
        out_shape=jax.ShapeDtypeStruct((total_tokens, D_MODEL), F32),
        input_output_aliases=aliases,
        compiler_params=pltpu.CompilerParams(dimension_semantics=("arbitrary",)),
        name="residual_norm",
    )(*args)


def _expert_ids(idx_words):
    lo = (idx_words & 0xFFFF) // ROW_WORDS
    hi = lax.shift_right_logical(idx_words, 16) // ROW_WORDS
    ids = jnp.concatenate([lo, hi], axis=1)
    return ids.transpose(0, 2, 1).reshape(-1, N_PAIRS)


def kernel(x, norm_mix, w_in, conv_w, conv_gain, hg_lb_logits, hg_gain, w_out, norm_ffn, peer_wq, peer_keys,
           peer_u, peer_v, norm_f):
    B, S, D = x.shape
    depth = norm_mix.shape[0]
    assert depth == 1 and D == D_MODEL and S % SEQ_BLOCK == 0 and B % N_CHUNKS == 0 and S % TOK_BLOCK == 0
    T = B * S
    bc = B // N_CHUNKS
    nb = bc * S // TOK_BLOCK
    params = _mixer_params(norm_mix[0], w_in[0], conv_w[0], conv_gain[0], hg_lb_logits, hg_gain[0],
                           w_out[0], norm_ffn[0], peer_wq[0], peer_keys[0])
    tab_u = _pack_table(peer_u[0])
    tab_v = _pack_table(peer_v[0]).reshape(peer_v.shape[1], D // 2)
    mixed = [_mixer_call(x, params, 0, bc)]
    idx_k, gates_k = None, None
    out = None
    for k in range(N_CHUNKS):
        if k + 1 < N_CHUNKS:
            mixed.append(_mixer_call(x, params, (k + 1) * bc, bc))
        x1_k, h2_k, scores_k = mixed[k]
        if k == 0:
            idx_k, gates_k = _topk_call(scores_k)
        scores_next = mixed[min(k + 1, N_CHUNKS - 1)][2]
        ids_k = _expert_ids(idx_k)
        w_k, idx_k, gates_k = _expert_u_call(idx_k.reshape(-1), tab_u, h2_k.reshape(bc * S, D), gates_k,
                                             scores_next)
        y_k = _sc_weighted_rows_call(tab_v, ids_k, w_k)
        out = _residual_norm_call(out, x1_k.reshape(bc * S, D), y_k, norm_f, k * nb, T)
    return out.reshape(B, S, D)
```

```python
import dataclasses
import functools

import numpy as np
import jax
import jax.numpy as jnp
from jax import lax
from jax.experimental import pallas as pl
from jax.experimental.pallas import tpu as pltpu
from jax.experimental.pallas import tpu_sc as plsc

F32 = jnp.float32
BF16 = jnp.bfloat16
I32 = jnp.int32

D_MODEL = 1024
D_CONV = 512
CONV_GROUP = 64
CONV_WIDTH = 3
HG_HEADS = 4
HG_KD = 128
HG_VD = 128
D_HG = HG_HEADS * HG_KD
D_IN = 3 * D_CONV + 4 * D_HG
CHUNK = 32
PEER_HEADS = 8
N_KEYS = 128
HALF_Q = 128
PEER_TOPK = 16
N_PAIRS = PEER_HEADS * PEER_TOPK
EPS = 1e-6

LANES = 128
SUBLANES = 8
SEQ_BLOCK = 256
TOK_BLOCK = 128
ROW_WORDS = D_MODEL // (2 * LANES)
ARAW_PITCH = TOK_BLOCK + 1
N_CHUNKS = 16
PACK_ROWS = 512
NORM_BLOCK = 512
SC_CORES = 2
SC_SUBCORES = 16
SC_LANES = 16
SC_ROWS = 32
VMEM_LIMIT_MIXER = 48 * 1024 * 1024
VMEM_LIMIT_EXPERT = 52 * 1024 * 1024

NT_DIMS = (((1,), (1,)), ((), ()))


def _bdot(a, b):
    return jnp.dot(a.astype(BF16), b.astype(BF16), preferred_element_type=F32)


def _bdot_nt(a, b):
    return lax.dot_general(a.astype(BF16), b.astype(BF16), NT_DIMS, preferred_element_type=F32)


def _split(a, pieces):
    out = []
    for _ in range(pieces - 1):
        p = a.astype(BF16)
        out.append(p)
        a = a - p.astype(F32)
    return out + [a.astype(BF16)]


def _dot_sel(sel, a, pieces=3):
    out = None
    for piece in _split(a, pieces):
        d = jnp.dot(sel, piece, preferred_element_type=F32)
        out = d if out is None else out + d
    return out


def _dot_sel_right(a, sel, pieces=3):
    out = None
    for piece in _split(a, pieces):
        d = jnp.dot(piece, sel, preferred_element_type=F32)
        out = d if out is None else out + d
    return out


def _silu(x):
    return x * (1.0 / (1.0 + jnp.exp(-x)))


def _sigmoid(x):
    return 1.0 / (1.0 + jnp.exp(-x))


def _rms(x, gain):
    return x * lax.rsqrt(jnp.mean(x * x, axis=-1, keepdims=True) + EPS) * gain


def _mixer_kernel(x_ref, nmix_ref, win_ref, convw_ref, cgain_ref, lbl_ref, hgain_ref, wout_ref,
                  nffn_ref, wq_ref, keys_ref, cum_ref, gmean_ref, hmean_ref,
                  x1_ref, h2_ref, sc_ref,
                  up_ref, st_ref):
    L = SEQ_BLOCK
    sb = pl.program_id(1)

    @pl.when(sb == 0)
    def _():
        up_ref[0:SUBLANES, :] = jnp.zeros((SUBLANES, D_CONV), F32)
        st_ref[...] = jnp.zeros(st_ref.shape, F32)

    x = x_ref[0]
    h = _rms(x, nmix_ref[...])
    proj = _bdot(h, win_ref[...])

    u = proj[:, 0:D_CONV] * proj[:, D_CONV:2 * D_CONV]
    up_ref[SUBLANES:SUBLANES + L, :] = u
    u1 = up_ref[SUBLANES - 1:SUBLANES - 1 + L, :]
    u2 = up_ref[SUBLANES - 2:SUBLANES - 2 + L, :]
    cw = convw_ref[...]
    y = cw[0:1, :] * u2 + cw[1:2, :] * u1 + cw[2:3, :] * u
    y = proj[:, 2 * D_CONV:3 * D_CONV] * y
    up_ref[0:SUBLANES, :] = up_ref[L:L + SUBLANES, :]
    gms = _dot_sel_right(y * y, gmean_ref[...], pieces=2)
    conv_out = y * lax.rsqrt(gms + EPS) * cgain_ref[...]

    o0 = 3 * D_CONV
    qp = proj[:, o0:o0 + D_HG]
    fp = proj[:, o0 + D_HG:o0 + 2 * D_HG]
    vv = proj[:, o0 + 2 * D_HG:o0 + 3 * D_HG]
    gg = proj[:, o0 + 3 * D_HG:o0 + 4 * D_HG]
    lbl = lbl_ref[...]
    lbe = jnp.exp(lbl - jnp.max(lbl, axis=0, keepdims=True))
    lb = lbe[0:1, :] / jnp.sum(lbe, axis=0, keepdims=True)
    qh = _silu(qp) * (HG_KD ** -0.5)
    f = lb + (1.0 - lb) * _sigmoid(fp)
    kh = 1.0 - f
    logf = jnp.log(f)
    cums = _dot_sel(cum_ref[...], logf)
    gcum = cums[0:L, :]
    glast = cums[L:2 * L, :]
    q_dec = qh * jnp.exp(gcum)
    k_dec = kh * jnp.exp(-gcum)
    k_end = kh * jnp.exp(glast - gcum)
    decay = jnp.exp(glast)

    row = lax.broadcasted_iota(I32, (L, L), 0)
    col = lax.broadcasted_iota(I32, (L, L), 1)
    causal = (row // CHUNK == col // CHUNK) & (col <= row)

    o_heads = []
    for hh in range(HG_HEADS):
        cs = slice(hh * HG_KD, (hh + 1) * HG_KD)
        qd, kd, ke, vh, dc = q_dec[:, cs], k_dec[:, cs], k_end[:, cs], vv[:, cs], decay[:, cs]
        a = _bdot_nt(qd, kd)
        a = jnp.where(causal, a, 0.0)
        o_intra = _bdot(a, vh)
        st = st_ref[hh]
        pieces = []
        for n in range(L // CHUNK):
            rs = slice(n * CHUNK, (n + 1) * CHUNK)
            pieces.append(_bdot_nt(qd[rs], st))
            st = dc[n * CHUNK:n * CHUNK + 1, :] * st + _bdot(vh[rs].T, ke[rs])
        st_ref[hh] = st
        o_heads.append(o_intra + jnp.concatenate(pieces, axis=0))
    o = jnp.concatenate(o_heads, axis=1)
    oms = _dot_sel_right(o * o, hmean_ref[...], pieces=2)
    o = o * lax.rsqrt(oms + EPS) * hgain_ref[...]
    o = o * _silu(gg)

    mix = jnp.concatenate([conv_out, o], axis=1)
    x1 = x + _bdot(mix, wout_ref[...])
    x1_ref[0] = x1
    h2 = _rms(x1, nffn_ref[...])
    h2_ref[0] = h2.astype(BF16)
    q2 = _bdot(h2, wq_ref[...])
    for hp in range(2 * PEER_HEADS):
        sc_ref[hp] = _bdot_nt(keys_ref[hp], q2[:, hp * HALF_Q:(hp + 1) * HALF_Q])


def _mixer_params(norm_mix, w_in, conv_w, conv_gain, lb_logits, hg_gain, w_out, norm_ffn, wq, keys):
    L = SEQ_BLOCK
    r = np.arange(L)
    same = (r[:, None] // CHUNK) == (r[None, :] // CHUNK)
    cum = np.concatenate([same & (r[None, :] <= r[:, None]), same], axis=0).astype(np.float32)
    c = np.arange(D_CONV)
    gmean = ((c[:, None] // CONV_GROUP) == (c[None, :] // CONV_GROUP)).astype(np.float32) / CONV_GROUP
    hmean = ((c[:, None] // HG_VD) == (c[None, :] // HG_VD)).astype(np.float32) / HG_VD
    return (norm_mix.reshape(1, D_MODEL), w_in.astype(BF16), conv_w, conv_gain.reshape(1, D_CONV), lb_logits,
            jnp.tile(hg_gain.reshape(1, HG_VD), (1, HG_HEADS)), w_out.astype(BF16), norm_ffn.reshape(1, D_MODEL),
            wq.astype(BF16), keys.reshape(2 * PEER_HEADS, N_KEYS, HALF_Q).astype(BF16),
            jnp.asarray(cum, BF16), jnp.asarray(gmean, BF16), jnp.asarray(hmean, BF16))


def _mixer_call(x, params, b0, nbatch):
    S = x.shape[1]
    L = SEQ_BLOCK
    nsb = S // L
    const = lambda a: pl.BlockSpec(a.shape, lambda b, s: (0,) * a.ndim)
    return pl.pallas_call(
        _mixer_kernel,
        grid=(nbatch, nsb),
        in_specs=[pl.BlockSpec((1, L, D_MODEL), lambda b, s: (b + b0, s, 0))] + [const(a) for a in params],
        out_specs=[
            pl.BlockSpec((1, L, D_MODEL), lambda b, s: (b, s, 0)),
            pl.BlockSpec((1, L, D_MODEL), lambda b, s: (b, s, 0)),
            pl.BlockSpec((2 * PEER_HEADS, N_KEYS, L), lambda b, s: (0, 0, b * nsb + s)),
        ],
        out_shape=[
            jax.ShapeDtypeStruct((nbatch, S, D_MODEL), F32),
            jax.ShapeDtypeStruct((nbatch, S, D_MODEL), BF16),
            jax.ShapeDtypeStruct((2 * PEER_HEADS, N_KEYS, nbatch * S), F32),
        ],
        scratch_shapes=[
            pltpu.VMEM((L + 2 * SUBLANES, D_CONV), F32),
            pltpu.VMEM((HG_HEADS, HG_VD, HG_KD), F32),
        ],
        compiler_params=pltpu.CompilerParams(
            dimension_semantics=("arbitrary", "arbitrary"), vmem_limit_bytes=VMEM_LIMIT_MIXER),
        name="mixer",
    )(x, *params)


def _extract_topk(vals, ids, n, sentinel):
    tops, sel = [], []
    for _ in range(n):
        m = jnp.max(vals, axis=0, keepdims=True)
        i = jnp.min(jnp.where(vals == m, ids, sentinel), axis=0, keepdims=True)
        tops.append(m)
        sel.append(i)
        vals = jnp.where(ids == i, -jnp.inf, vals)
    return jnp.concatenate(tops, axis=0), jnp.concatenate(sel, axis=0)


def _topk_head(hd, sc_ref, gate_ref, exp_ref):
    TB = TOK_BLOCK
    K = PEER_TOPK
    iota_keys = lax.broadcasted_iota(I32, (N_KEYS, TB), 0).astype(F32)
    sub = lax.broadcasted_iota(I32, (SUBLANES, TB), 0)
    subf = sub.astype(F32)
    s1, i1t = _extract_topk(sc_ref[2 * hd], iota_keys, K, N_KEYS)
    s2, i2t = _extract_topk(sc_ref[2 * hd + 1], iota_keys, K, N_KEYS)
    vals, ids = [], []
    for b0 in range(0, K, SUBLANES):
        vals.append(s1[0:1, :] + s2[b0:b0 + SUBLANES, :])
        ids.append(subf + b0)
    for a in range(1, SUBLANES):
        v = s1[a:a + 1, :] + s2[0:SUBLANES, :]
        vals.append(jnp.where(sub < K // (a + 1), v, -jnp.inf))
        ids.append(subf + a * K)
    vals.append(s1[SUBLANES:K, :] + s2[0:1, :])
    ids.append((subf + SUBLANES) * K)
    c_score, c_idx = _extract_topk(jnp.concatenate(vals, axis=0), jnp.concatenate(ids, axis=0), K, K * K)
    c_idx = c_idx.astype(I32)
    i1t = i1t.astype(I32)
    i2t = i2t.astype(I32)
    a_sel = lax.shift_right_logical(c_idx, 4)
    b_sel = c_idx & (K - 1)
    i1 = jnp.zeros((K, TB), I32)
    i2 = jnp.zeros((K, TB), I32)
    for a in range(K):
        i1 = i1 + jnp.where(a_sel == a, i1t[a:a + 1, :], 0)
        i2 = i2 + jnp.where(b_sel == a, i2t[a:a + 1, :], 0)
    e = jnp.exp(c_score - jnp.max(c_score, axis=0, keepdims=True))
    row0 = pl.multiple_of(hd * K, K)
    gate_ref[0, pl.ds(row0, K), :] = e / jnp.sum(e, axis=0, keepdims=True)
    exp_ref[pl.ds(row0, K), :] = (i1 * N_KEYS + i2) * ROW_WORDS


def _pack_offsets(exp_ref):
    half = N_PAIRS // 2
    return exp_ref[0:half, :] | (exp_ref[half:N_PAIRS, :] << 16)


def _topk_kernel(sc_ref, idx_ref, gate_ref, exp_ref):
    def head(hd, c):
        _topk_head(hd, sc_ref, gate_ref, exp_ref)
        return c

    lax.fori_loop(0, PEER_HEADS, head, 0)
    idx_ref[0] = _pack_offsets(exp_ref)


def _topk_call(scores_t):
    nblk = scores_t.shape[-1] // TOK_BLOCK
    return pl.pallas_call(
        _topk_kernel,
        grid=(nblk,),
        in_specs=[pl.BlockSpec((2 * PEER_HEADS, N_KEYS, TOK_BLOCK), lambda i: (0, 0, i))],
        out_specs=[
            pl.BlockSpec((1, N_PAIRS // 2, TOK_BLOCK), lambda i: (i, 0, 0)),
            pl.BlockSpec((1, N_PAIRS, TOK_BLOCK), lambda i: (i, 0, 0)),
        ],
        out_shape=[
            jax.ShapeDtypeStruct((nblk, N_PAIRS // 2, TOK_BLOCK), I32),
            jax.ShapeDtypeStruct((nblk, N_PAIRS, TOK_BLOCK), F32),
        ],
        scratch_shapes=[pltpu.VMEM((N_PAIRS, TOK_BLOCK), I32)],
        compiler_params=pltpu.CompilerParams(dimension_semantics=("arbitrary",)),
        name="topk",
    )(scores_t)


def _pack_kernel(tab_ref, out_ref, *, one_row_per_entry):
    rows = tab_ref.shape[0]
    bits = lambda v: lax.bitcast_convert_type(v.astype(BF16).astype(F32), jnp.uint32)
    for i in range(ROW_WORDS):
        lo = bits(tab_ref[:, (2 * i) * LANES:(2 * i + 1) * LANES]) >> 16
        hi = bits(tab_ref[:, (2 * i + 1) * LANES:(2 * i + 2) * LANES]) & jnp.uint32(0xFFFF0000)
        words = lax.bitcast_convert_type(lo | hi, I32)
        if one_row_per_entry:
            out_ref[:, i * LANES:(i + 1) * LANES] = words
        else:
            out_ref[pl.ds(i, rows, stride=ROW_WORDS), :] = words


def _pack_table(tab, one_row_per_entry):
    n = tab.shape[0]
    if one_row_per_entry:
        out_block, out_shape = (PACK_ROWS, ROW_WORDS * LANES), (n, ROW_WORDS * LANES)
    else:
        out_block, out_shape = (PACK_ROWS * ROW_WORDS, LANES), (n * ROW_WORDS, LANES)
    return pl.pallas_call(
        functools.partial(_pack_kernel, one_row_per_entry=one_row_per_entry),
        grid=(n // PACK_ROWS,),
        in_specs=[pl.BlockSpec((PACK_ROWS, D_MODEL), lambda i: (i, 0))],
        out_specs=pl.BlockSpec(out_block, lambda i: (i, 0)),
        out_shape=jax.ShapeDtypeStruct(out_shape, I32),
        compiler_params=pltpu.CompilerParams(dimension_semantics=("arbitrary",)),
        name="pack_table",
    )(tab)


def _load_table(tab_hbm, tab_ref, sem):
    @pl.when(pl.program_id(0) == 0)
    def _():
        copy = pltpu.make_async_copy(tab_hbm, tab_ref, sem)
        copy.start()
        copy.wait()


def _pair_offsets(idx_ref, w, j):
    word = idx_ref[w * TOK_BLOCK + j]
    lo = pl.multiple_of(word & 0xFFFF, ROW_WORDS)
    hi = pl.multiple_of(lax.shift_right_logical(word, 16), ROW_WORDS)
    return ((w, lo), (w + N_PAIRS // 2, hi))


def _gather_rows(idx_ref, tab_ref, g_ref, j):
    for w in range(N_PAIRS // 2):
        for k, off in _pair_offsets(idx_ref, w, j):
            g_ref[pl.ds(k * ROW_WORDS, ROW_WORDS), :] = tab_ref[pl.ds(off, ROW_WORDS), :]


def _two_token_step(consume, gather, g0_ref, g1_ref, j):
    consume(g0_ref, j)
    consume(g1_ref, j + 1)
    gather(g0_ref, jnp.minimum(j + 2, TOK_BLOCK - 1))
    gather(g1_ref, jnp.minimum(j + 3, TOK_BLOCK - 1))


def _expert_u_kernel(idx_ref, tab_hbm, h2_ref, gate_ref, gsum_ref, sc_ref,
                     w_ref, idx_next_ref, gate_next_ref,
                     tab_ref, tab_sem, g0_ref, g1_ref, xs_ref, araw_ref, exp_ref):
    TB = TOK_BLOCK
    nchunk = D_MODEL // LANES
    group = 2 * LANES // nchunk
    pitch = ARAW_PITCH
    _load_table(tab_hbm, tab_ref, tab_sem)

    for c in range(nchunk):
        xs_ref[pl.ds(c, TB, stride=nchunk), :] = h2_ref[:, c * LANES:(c + 1) * LANES].astype(F32)

    def dots(g_ref, j):
        xj = xs_ref[pl.ds(pl.multiple_of(j * nchunk, nchunk), nchunk), :].astype(BF16)
        for g in range(N_PAIRS // group):
            wgt = pltpu.bitcast(g_ref[pl.ds(g * LANES, LANES), :], BF16)
            r = lax.dot_general(xj, wgt, NT_DIMS, preferred_element_type=F32)
            for half in range(2):
                cc = 2 * g + half
                araw_ref[pl.ds(cc * nchunk * pitch + j, nchunk, stride=pitch), :] = r[:, half * LANES:(half + 1) * LANES]

    gather = functools.partial(_gather_rows, idx_ref, tab_ref)
    tokens_per_head = TB // PEER_HEADS
    gather(g0_ref, 0)
    gather(g1_ref, 1)

    def head_step(hd, c):
        _topk_head(hd, sc_ref, gate_next_ref, exp_ref)
        for s in range(tokens_per_head // 2):
            _two_token_step(dots, gather, g0_ref, g1_ref, hd * tokens_per_head + 2 * s)
        return c

    lax.fori_loop(0, PEER_HEADS, head_step, 0)
    idx_next_ref[0] = _pack_offsets(exp_ref)

    lane_chunk = lax.broadcasted_iota(I32, (TB, LANES), 1) % nchunk
    a = jnp.zeros((TB, N_PAIRS), F32)
    for cc in range(nchunk):
        part = jnp.zeros((TB, LANES), F32)
        for c in range(nchunk):
            part = part + jnp.where(lane_chunk == c, araw_ref[pl.ds((cc * nchunk + c) * pitch, TB), :], 0.0)
        a = a + _dot_sel_right(part, gsum_ref[cc * LANES:(cc + 1) * LANES, :])
    w_ref[...] = gate_ref[0].T * (0.5 * a * (1.0 + lax.erf(a * (2.0 ** -0.5))))


def _pair_expand():
    pair_of_col = np.arange(D_MODEL) // SUBLANES
    return (np.arange(N_PAIRS)[:, None] == pair_of_col[None, :]).astype(np.float32)


def _expert_u_call(idx_words, tab, h2, gates, scores_next):
    nchunk = D_MODEL // LANES
    nblk = h2.shape[0] // TOK_BLOCK
    return pl.pallas_call(
        _expert_u_kernel,
        grid=(nblk,),
        in_specs=[
            pl.BlockSpec((N_PAIRS // 2 * TOK_BLOCK,), lambda i: (i,), memory_space=pltpu.SMEM),
            pl.BlockSpec(memory_space=pl.ANY),
            pl.BlockSpec((TOK_BLOCK, D_MODEL), lambda i: (i, 0)),
            pl.BlockSpec((1, N_PAIRS, TOK_BLOCK), lambda i: (i, 0, 0)),
            pl.BlockSpec((D_MODEL, N_PAIRS), lambda i: (0, 0)),
            pl.BlockSpec((2 * PEER_HEADS, N_KEYS, TOK_BLOCK), lambda i: (0, 0, i)),
        ],
        out_specs=[
            pl.BlockSpec((TOK_BLOCK, N_PAIRS), lambda i: (i, 0)),
            pl.BlockSpec((1, N_PAIRS // 2, TOK_BLOCK), lambda i: (i, 0, 0)),
            pl.BlockSpec((1, N_PAIRS, TOK_BLOCK), lambda i: (i, 0, 0)),
        ],
        out_shape=[
            jax.ShapeDtypeStruct((nblk * TOK_BLOCK, N_PAIRS), F32),
            jax.ShapeDtypeStruct((nblk, N_PAIRS // 2, TOK_BLOCK), I32),
            jax.ShapeDtypeStruct((nblk, N_PAIRS, TOK_BLOCK), F32),
        ],
        scratch_shapes=[
            pltpu.VMEM(tab.shape, I32),
            pltpu.SemaphoreType.DMA,
            pltpu.VMEM((N_PAIRS * ROW_WORDS, LANES), I32),
            pltpu.VMEM((N_PAIRS * ROW_WORDS, LANES), I32),
            pltpu.VMEM((TOK_BLOCK * nchunk, LANES), F32),
            pltpu.VMEM((nchunk * nchunk * ARAW_PITCH, LANES), F32),
            pltpu.VMEM((N_PAIRS, TOK_BLOCK), I32),
        ],
        compiler_params=pltpu.CompilerParams(
            dimension_semantics=("arbitrary",), vmem_limit_bytes=VMEM_LIMIT_EXPERT),
        name="expert_u",
    )(idx_words, tab, h2, gates, jnp.asarray(_pair_expand().T, BF16), scores_next)


def _sc_weighted_rows_call(table_words, ids, w):
    ts = ids.shape[0]
    row_len = table_words.shape[1]
    workers = SC_CORES * SC_SUBCORES
    tpw = ts // workers
    nbatch = N_PAIRS // SC_ROWS
    groups = LANES // SC_LANES
    mesh = plsc.VectorSubcoreMesh(core_axis_name="c", subcore_axis_name="s")
    params = pltpu.CompilerParams()
    if "needs_layout_passes" in pltpu.CompilerParams.__dataclass_fields__:
        params = dataclasses.replace(params, needs_layout_passes=False)

    @functools.partial(
        pl.kernel, mesh=mesh, compiler_params=params,
        out_type=jax.ShapeDtypeStruct((ts, D_MODEL), F32),
        scratch_types=[
            pltpu.VMEM((N_PAIRS,), I32), pltpu.VMEM((N_PAIRS,), I32),
            pltpu.VMEM((N_PAIRS,), F32), pltpu.VMEM((N_PAIRS,), F32),
            pltpu.VMEM((SC_ROWS, row_len), I32), pltpu.VMEM((SC_ROWS, row_len), I32),
            pltpu.VMEM((D_MODEL,), F32),
            pltpu.SemaphoreType.DMA, pltpu.SemaphoreType.DMA, pltpu.SemaphoreType.DMA, pltpu.SemaphoreType.DMA,
        ],
    )
    def sc_kernel(table_hbm, ids_hbm, w_hbm, out_hbm, ids0, ids1, w0, w1, rows0, rows1, y_v,
                  sem0, sem1, sem_ids, sem_w):
        worker = lax.axis_index("s") * SC_CORES + lax.axis_index("c")
        base = worker * tpw
        ids_v, w_v, rows_v, sems = (ids0, ids1), (w0, w1), (rows0, rows1), (sem0, sem1)

        def gather(slot, b, buf):
            src = table_hbm.at[ids_v[slot].at[pl.ds(b * SC_ROWS, SC_ROWS)]]
            return pltpu.make_async_copy(src, rows_v[buf], sems[buf])

        def fetch(slot, tok):
            return (pltpu.make_async_copy(ids_hbm.at[tok], ids_v[slot], sem_ids),
                    pltpu.make_async_copy(w_hbm.at[tok], w_v[slot], sem_w))

        def accumulate(slot, b, buf):
            wvs = [w_v[slot][pl.ds(b * SC_ROWS + q * SC_LANES, SC_LANES)] for q in range(SC_ROWS // SC_LANES)]
            wrs = [jnp.broadcast_to(wvs[r // SC_LANES][r % SC_LANES], (SC_LANES,)) for r in range(SC_ROWS)]

            def group(g, c):
                off = pl.multiple_of(g * SC_LANES, SC_LANES)
                f_lo = pl.multiple_of((2 * (g // groups)) * LANES + (g % groups) * SC_LANES, SC_LANES)
                f_hi = pl.multiple_of(f_lo + LANES, SC_LANES)
                if b == 0:
                    acc_lo = jnp.zeros((SC_LANES,), F32)
                    acc_hi = jnp.zeros((SC_LANES,), F32)
                else:
                    acc_lo = y_v[pl.ds(f_lo, SC_LANES)]
                    acc_hi = y_v[pl.ds(f_hi, SC_LANES)]
                for r in range(SC_ROWS):
                    word = rows_v[buf][r, pl.ds(off, SC_LANES)]
                    acc_lo = acc_lo + wrs[r] * lax.bitcast_convert_type(word << 16, F32)
                    acc_hi = acc_hi + wrs[r] * lax.bitcast_convert_type(word & jnp.int32(-65536), F32)
                y_v[pl.ds(f_lo, SC_LANES)] = acc_lo
                y_v[pl.ds(f_hi, SC_LANES)] = acc_hi
                return c

            lax.fori_loop(0, row_len // SC_LANES, group, 0)

        for copy in fetch(0, base):
            copy.start()
        for copy in fetch(0, base):
            copy.wait()
        gather(0, 0, 0).start()

        def token_pair(p, c):
            t0 = base + 2 * p
            t_next = base + jnp.minimum(2 * p + 2, tpw - 1)
            for q in range(2 * nbatch):
                slot, b, buf = q // nbatch, q % nbatch, q % 2
                if q == 0:
                    for copy in fetch(1, t0 + 1):
                        copy.start()
                if q == nbatch:
                    for copy in fetch(0, t_next):
                        copy.start()
                if q == nbatch - 1:
                    for copy in fetch(1, t0 + 1):
                        copy.wait()
                if q == 2 * nbatch - 1:
                    for copy in fetch(0, t_next):
                        copy.wait()
                nq = (q + 1) % (2 * nbatch)
                gather(nq // nbatch, nq % nbatch, nq % 2).start()
                gather(slot, b, buf).wait()
                accumulate(slot, b, buf)
                if b == nbatch - 1:
                    pltpu.sync_copy(y_v, out_hbm.at[t0 + slot])
            return c

        lax.fori_loop(0, tpw // 2, token_pair, 0)
        gather(0, 0, 0).wait()

    return sc_kernel(table_words, ids, w)


def _residual_norm_kernel(*refs):
    x1_ref, y_ref, nf_ref, out_ref = refs[-4:]
    out_ref[...] = _rms(x1_ref[...] + y_ref[...], nf_ref[...])


def _residual_norm_call(out_prev, x1, y, norm_f, chunk, total_tokens):
    nblk = y.shape[0] // NORM_BLOCK
    here = pl.BlockSpec((NORM_BLOCK, D_MODEL), lambda i: (i + chunk * nblk, 0))
    local = pl.BlockSpec((NORM_BLOCK, D_MODEL), lambda i: (i, 0))
    in_specs = [local, local, pl.BlockSpec((1, D_MODEL), lambda i: (0, 0))]
    args = (x1, y, norm_f.reshape(1, D_MODEL))
    aliases = {}
    if out_prev is not None:
        in_specs = [pl.BlockSpec(memory_space=pl.ANY)] + in_specs
        args = (out_prev,) + args
        aliases = {0: 0}
    return pl.pallas_call(
        _residual_norm_kernel,
        grid=(nblk,),
        in_specs=in_specs,
        out_specs=here,
        out_shape=jax.ShapeDtypeStruct((total_tokens, D_MODEL), F32),
        input_output_aliases=aliases,
        compiler_params=pltpu.CompilerParams(dimension_semantics=("arbitrary",)),
        name="residual_norm",
    )(*args)


def _expert_ids(idx_words):
    lo = (idx_words & 0xFFFF) // ROW_WORDS
    hi = lax.shift_right_logical(idx_words, 16) // ROW_WORDS
    ids = jnp.concatenate([lo, hi], axis=1)
    return ids.transpose(0, 2, 1).reshape(-1, N_PAIRS)


def kernel(x, norm_mix, w_in, conv_w, conv_gain, hg_lb_logits, hg_gain, w_out, norm_ffn, peer_wq, peer_keys,
           peer_u, peer_v, norm_f):
    B, S, D = x.shape
    depth = norm_mix.shape[0]
    assert depth == 1 and D == D_MODEL and S % SEQ_BLOCK == 0 and B % N_CHUNKS == 0 and S % TOK_BLOCK == 0
    T = B * S
    bc = B // N_CHUNKS
    nb = bc * S // TOK_BLOCK
    params = _mixer_params(norm_mix[0], w_in[0], conv_w[0], conv_gain[0], hg_lb_logits, hg_gain[0],
                           w_out[0], norm_ffn[0], peer_wq[0], peer_keys[0])
    tab_u = _pack_table(peer_u[0], one_row_per_entry=False)
    tab_v = _pack_table(peer_v[0], one_row_per_entry=True)
    mixed = [_mixer_call(x, params, 0, bc)]
    idx_k, gates_k = None, None
    out = None
    for k in range(N_CHUNKS):
        if k + 1 < N_CHUNKS:
            mixed.append(_mixer_call(x, params, (k + 1) * bc, bc))
        x1_k, h2_k, scores_k = mixed[k]
        if k == 0:
            idx_k, gates_k = _topk_call(scores_k)
        scores_next = mixed[min(k + 1, N_CHUNKS - 1)][2]
        ids_k = _expert_ids(idx_k)
        w_k, idx_k, gates_k = _expert_u_call(idx_k.reshape(-1), tab_u, h2_k.reshape(bc * S, D), gates_k,
                                             scores_next)
        y_k = _sc_weighted_rows_call(tab_v, ids_k, w_k)
        out = _residual_norm_call(out, x1_k.reshape(bc * S, D), y_k, norm_f, k, T)
    return out.reshape(B, S, D)
```

```python
import dataclasses
import functools

import numpy as np
import jax
import jax.numpy as jnp
from jax import lax
from jax.experimental import pallas as pl
from jax.experimental.pallas import tpu as pltpu
from jax.experimental.pallas import tpu_sc as plsc

F32 = jnp.float32
BF16 = jnp.bfloat16
I32 = jnp.int32

D_MODEL = 1024
D_CONV = 512
CONV_GROUP = 64
CONV_WIDTH = 3
HG_HEADS = 4
HG_KD = 128
HG_VD = 128
D_HG = HG_HEADS * HG_KD
D_IN = 3 * D_CONV + 4 * D_HG
CHUNK = 32
PEER_HEADS = 8
N_KEYS = 128
HALF_Q = 128
PEER_TOPK = 16
N_PAIRS = PEER_HEADS * PEER_TOPK
EPS = 1e-6

LANES = 128
SUBLANES = 8
SEQ_BLOCK = 256
TOK_BLOCK = 128
ROW_WORDS = D_MODEL // (2 * LANES)
ARAW_PITCH = TOK_BLOCK + 1
N_CHUNKS = 16
PACK_ROWS = 512
NORM_BLOCK = 512
SC_CORES = 2
SC_SUBCORES = 16
SC_LANES = 16
SC_ROWS = 32
SC_U_CHUNKS = (5, 11)
VMEM_LIMIT_MIXER = 48 * 1024 * 1024
VMEM_LIMIT_EXPERT = 52 * 1024 * 1024

NT_DIMS = (((1,), (1,)), ((), ()))


def _bdot(a, b):
    return jnp.dot(a.astype(BF16), b.astype(BF16), preferred_element_type=F32)


def _bdot_nt(a, b):
    return lax.dot_general(a.astype(BF16), b.astype(BF16), NT_DIMS, preferred_element_type=F32)


def _split(a, pieces):
    out = []
    for _ in range(pieces - 1):
        p = a.astype(BF16)
        out.append(p)
        a = a - p.astype(F32)
    return out + [a.astype(BF16)]


def _dot_sel(sel, a, pieces=3):
    out = None
    for piece in _split(a, pieces):
        d = jnp.dot(sel, piece, preferred_element_type=F32)
        out = d if out is None else out + d
    return out


def _dot_sel_right(a, sel, pieces=3):
    out = None
    for piece in _split(a, pieces):
        d = jnp.dot(piece, sel, preferred_element_type=F32)
        out = d if out is None else out + d
    return out


def _silu(x):
    return x * (1.0 / (1.0 + jnp.exp(-x)))


def _sigmoid(x):
    return 1.0 / (1.0 + jnp.exp(-x))


def _rms(x, gain):
    return x * lax.rsqrt(jnp.mean(x * x, axis=-1, keepdims=True) + EPS) * gain


def _mixer_kernel(x_ref, nmix_ref, win_ref, convw_ref, cgain_ref, lbl_ref, hgain_ref, wout_ref,
                  nffn_ref, wq_ref, keys_ref, cum_ref, gmean_ref, hmean_ref,
                  x1_ref, h2_ref, sc_ref,
                  up_ref, st_ref):
    L = SEQ_BLOCK
    sb = pl.program_id(1)

    @pl.when(sb == 0)
    def _():
        up_ref[0:SUBLANES, :] = jnp.zeros((SUBLANES, D_CONV), F32)
        st_ref[...] = jnp.zeros(st_ref.shape, F32)

    x = x_ref[0]
    h = _rms(x, nmix_ref[...])
    proj = _bdot(h, win_ref[...])

    u = proj[:, 0:D_CONV] * proj[:, D_CONV:2 * D_CONV]
    up_ref[SUBLANES:SUBLANES + L, :] = u
    u1 = up_ref[SUBLANES - 1:SUBLANES - 1 + L, :]
    u2 = up_ref[SUBLANES - 2:SUBLANES - 2 + L, :]
    cw = convw_ref[...]
    y = cw[0:1, :] * u2 + cw[1:2, :] * u1 + cw[2:3, :] * u
    y = proj[:, 2 * D_CONV:3 * D_CONV] * y
    up_ref[0:SUBLANES, :] = up_ref[L:L + SUBLANES, :]
    gms = _dot_sel_right(y * y, gmean_ref[...], pieces=2)
    conv_out = y * lax.rsqrt(gms + EPS) * cgain_ref[...]

    o0 = 3 * D_CONV
    qp = proj[:, o0:o0 + D_HG]
    fp = proj[:, o0 + D_HG:o0 + 2 * D_HG]
    vv = proj[:, o0 + 2 * D_HG:o0 + 3 * D_HG]
    gg = proj[:, o0 + 3 * D_HG:o0 + 4 * D_HG]
    lbl = lbl_ref[...]
    lbe = jnp.exp(lbl - jnp.max(lbl, axis=0, keepdims=True))
    lb = lbe[0:1, :] / jnp.sum(lbe, axis=0, keepdims=True)
    qh = _silu(qp) * (HG_KD ** -0.5)
    f = lb + (1.0 - lb) * _sigmoid(fp)
    kh = 1.0 - f
    logf = jnp.log(f)
    cums = _dot_sel(cum_ref[...], logf)
    gcum = cums[0:L, :]
    glast = cums[L:2 * L, :]
    q_dec = qh * jnp.exp(gcum)
    k_dec = kh * jnp.exp(-gcum)
    k_end = kh * jnp.exp(glast - gcum)
    decay = jnp.exp(glast)

    row = lax.broadcasted_iota(I32, (L, L), 0)
    col = lax.broadcasted_iota(I32, (L, L), 1)
    causal = (row // CHUNK == col // CHUNK) & (col <= row)

    o_heads = []
    for hh in range(HG_HEADS):
        cs = slice(hh * HG_KD, (hh + 1) * HG_KD)
        qd, kd, ke, vh, dc = q_dec[:, cs], k_dec[:, cs], k_end[:, cs], vv[:, cs], decay[:, cs]
        a = _bdot_nt(qd, kd)
        a = jnp.where(causal, a, 0.0)
        o_intra = _bdot(a, vh)
        st = st_ref[hh]
        pieces = []
        for n in range(L // CHUNK):
            rs = slice(n * CHUNK, (n + 1) * CHUNK)
            pieces.append(_bdot_nt(qd[rs], st))
            st = dc[n * CHUNK:n * CHUNK + 1, :] * st + _bdot(vh[rs].T, ke[rs])
        st_ref[hh] = st
        o_heads.append(o_intra + jnp.concatenate(pieces, axis=0))
    o = jnp.concatenate(o_heads, axis=1)
    oms = _dot_sel_right(o * o, hmean_ref[...], pieces=2)
    o = o * lax.rsqrt(oms + EPS) * hgain_ref[...]
    o = o * _silu(gg)

    mix = jnp.concatenate([conv_out, o], axis=1)
    x1 = x + _bdot(mix, wout_ref[...])
    x1_ref[0] = x1
    h2 = _rms(x1, nffn_ref[...])
    h2_ref[0] = h2.astype(BF16)
    q2 = _bdot(h2, wq_ref[...])
    for hp in range(2 * PEER_HEADS):
        sc_ref[hp] = _bdot_nt(keys_ref[hp], q2[:, hp * HALF_Q:(hp + 1) * HALF_Q])


def _mixer_params(norm_mix, w_in, conv_w, conv_gain, lb_logits, hg_gain, w_out, norm_ffn, wq, keys):
    L = SEQ_BLOCK
    r = np.arange(L)
    same = (r[:, None] // CHUNK) == (r[None, :] // CHUNK)
    cum = np.concatenate([same & (r[None, :] <= r[:, None]), same], axis=0).astype(np.float32)
    c = np.arange(D_CONV)
    gmean = ((c[:, None] // CONV_GROUP) == (c[None, :] // CONV_GROUP)).astype(np.float32) / CONV_GROUP
    hmean = ((c[:, None] // HG_VD) == (c[None, :] // HG_VD)).astype(np.float32) / HG_VD
    return (norm_mix.reshape(1, D_MODEL), w_in.astype(BF16), conv_w, conv_gain.reshape(1, D_CONV), lb_logits,
            jnp.tile(hg_gain.reshape(1, HG_VD), (1, HG_HEADS)), w_out.astype(BF16), norm_ffn.reshape(1, D_MODEL),
            wq.astype(BF16), keys.reshape(2 * PEER_HEADS, N_KEYS, HALF_Q).astype(BF16),
            jnp.asarray(cum, BF16), jnp.asarray(gmean, BF16), jnp.asarray(hmean, BF16))


def _mixer_call(x, params, b0, nbatch):
    S = x.shape[1]
    L = SEQ_BLOCK
    nsb = S // L
    const = lambda a: pl.BlockSpec(a.shape, lambda b, s: (0,) * a.ndim)
    return pl.pallas_call(
        _mixer_kernel,
        grid=(nbatch, nsb),
        in_specs=[pl.BlockSpec((1, L, D_MODEL), lambda b, s: (b + b0, s, 0))] + [const(a) for a in params],
        out_specs=[
            pl.BlockSpec((1, L, D_MODEL), lambda b, s: (b, s, 0)),
            pl.BlockSpec((1, L, D_MODEL), lambda b, s: (b, s, 0)),
            pl.BlockSpec((2 * PEER_HEADS, N_KEYS, L), lambda b, s: (0, 0, b * nsb + s)),
        ],
        out_shape=[
            jax.ShapeDtypeStruct((nbatch, S, D_MODEL), F32),
            jax.ShapeDtypeStruct((nbatch, S, D_MODEL), BF16),
            jax.ShapeDtypeStruct((2 * PEER_HEADS, N_KEYS, nbatch * S), F32),
        ],
        scratch_shapes=[
            pltpu.VMEM((L + 2 * SUBLANES, D_CONV), F32),
            pltpu.VMEM((HG_HEADS, HG_VD, HG_KD), F32),
        ],
        compiler_params=pltpu.CompilerParams(
            dimension_semantics=("arbitrary", "arbitrary"), vmem_limit_bytes=VMEM_LIMIT_MIXER),
        name="mixer",
    )(x, *params)


def _extract_topk(vals, ids, n, sentinel):
    tops, sel = [], []
    for _ in range(n):
        m = jnp.max(vals, axis=0, keepdims=True)
        i = jnp.min(jnp.where(vals == m, ids, sentinel), axis=0, keepdims=True)
        tops.append(m)
        sel.append(i)
        vals = jnp.where(ids == i, -jnp.inf, vals)
    return jnp.concatenate(tops, axis=0), jnp.concatenate(sel, axis=0)


def _topk_head(hd, sc_ref, gate_ref, exp_ref):
    TB = TOK_BLOCK
    K = PEER_TOPK
    iota_keys = lax.broadcasted_iota(I32, (N_KEYS, TB), 0).astype(F32)
    sub = lax.broadcasted_iota(I32, (SUBLANES, TB), 0)
    subf = sub.astype(F32)
    s1, i1t = _extract_topk(sc_ref[2 * hd], iota_keys, K, N_KEYS)
    s2, i2t = _extract_topk(sc_ref[2 * hd + 1], iota_keys, K, N_KEYS)
    vals, ids = [], []
    for b0 in range(0, K, SUBLANES):
        vals.append(s1[0:1, :] + s2[b0:b0 + SUBLANES, :])
        ids.append(subf + b0)
    for a in range(1, SUBLANES):
        v = s1[a:a + 1, :] + s2[0:SUBLANES, :]
        vals.append(jnp.where(sub < K // (a + 1), v, -jnp.inf))
        ids.append(subf + a * K)
    vals.append(s1[SUBLANES:K, :] + s2[0:1, :])
    ids.append((subf + SUBLANES) * K)
    c_score, c_idx = _extract_topk(jnp.concatenate(vals, axis=0), jnp.concatenate(ids, axis=0), K, K * K)
    c_idx = c_idx.astype(I32)
    i1t = i1t.astype(I32)
    i2t = i2t.astype(I32)
    a_sel = lax.shift_right_logical(c_idx, 4)
    b_sel = c_idx & (K - 1)
    i1 = jnp.zeros((K, TB), I32)
    i2 = jnp.zeros((K, TB), I32)
    for a in range(K):
        i1 = i1 + jnp.where(a_sel == a, i1t[a:a + 1, :], 0)
        i2 = i2 + jnp.where(b_sel == a, i2t[a:a + 1, :], 0)
    e = jnp.exp(c_score - jnp.max(c_score, axis=0, keepdims=True))
    row0 = pl.multiple_of(hd * K, K)
    gate_ref[0, pl.ds(row0, K), :] = e / jnp.sum(e, axis=0, keepdims=True)
    exp_ref[pl.ds(row0, K), :] = (i1 * N_KEYS + i2) * ROW_WORDS


def _pack_offsets(exp_ref):
    half = N_PAIRS // 2
    return exp_ref[0:half, :] | (exp_ref[half:N_PAIRS, :] << 16)


def _topk_kernel(sc_ref, idx_ref, gate_ref, exp_ref):
    def head(hd, c):
        _topk_head(hd, sc_ref, gate_ref, exp_ref)
        return c

    lax.fori_loop(0, PEER_HEADS, head, 0)
    idx_ref[0] = _pack_offsets(exp_ref)


def _topk_call(scores_t):
    nblk = scores_t.shape[-1] // TOK_BLOCK
    return pl.pallas_call(
        _topk_kernel,
        grid=(nblk,),
        in_specs=[pl.BlockSpec((2 * PEER_HEADS, N_KEYS, TOK_BLOCK), lambda i: (0, 0, i))],
        out_specs=[
            pl.BlockSpec((1, N_PAIRS // 2, TOK_BLOCK), lambda i: (i, 0, 0)),
            pl.BlockSpec((1, N_PAIRS, TOK_BLOCK), lambda i: (i, 0, 0)),
        ],
        out_shape=[
            jax.ShapeDtypeStruct((nblk, N_PAIRS // 2, TOK_BLOCK), I32),
            jax.ShapeDtypeStruct((nblk, N_PAIRS, TOK_BLOCK), F32),
        ],
        scratch_shapes=[pltpu.VMEM((N_PAIRS, TOK_BLOCK), I32)],
        compiler_params=pltpu.CompilerParams(dimension_semantics=("arbitrary",)),
        name="topk",
    )(scores_t)


def _pack_kernel(tab_ref, out_ref, *, one_row_per_entry):
    rows = tab_ref.shape[0]
    bits = lambda v: lax.bitcast_convert_type(v.astype(BF16).astype(F32), jnp.uint32)
    for i in range(ROW_WORDS):
        lo = bits(tab_ref[:, (2 * i) * LANES:(2 * i + 1) * LANES]) >> 16
        hi = bits(tab_ref[:, (2 * i + 1) * LANES:(2 * i + 2) * LANES]) & jnp.uint32(0xFFFF0000)
        words = lax.bitcast_convert_type(lo | hi, I32)
        if one_row_per_entry:
            out_ref[:, i * LANES:(i + 1) * LANES] = words
        else:
            out_ref[pl.ds(i, rows, stride=ROW_WORDS), :] = words


def _pack_table(tab, one_row_per_entry):
    n = tab.shape[0]
    if one_row_per_entry:
        out_block, out_shape = (PACK_ROWS, ROW_WORDS * LANES), (n, ROW_WORDS * LANES)
    else:
        out_block, out_shape = (PACK_ROWS * ROW_WORDS, LANES), (n * ROW_WORDS, LANES)
    return pl.pallas_call(
        functools.partial(_pack_kernel, one_row_per_entry=one_row_per_entry),
        grid=(n // PACK_ROWS,),
        in_specs=[pl.BlockSpec((PACK_ROWS, D_MODEL), lambda i: (i, 0))],
        out_specs=pl.BlockSpec(out_block, lambda i: (i, 0)),
        out_shape=jax.ShapeDtypeStruct(out_shape, I32),
        compiler_params=pltpu.CompilerParams(dimension_semantics=("arbitrary",)),
        name="pack_table",
    )(tab)


def _load_table(tab_hbm, tab_ref, sem):
    @pl.when(pl.program_id(0) == 0)
    def _():
        copy = pltpu.make_async_copy(tab_hbm, tab_ref, sem)
        copy.start()
        copy.wait()


def _pair_offsets(idx_ref, w, j):
    word = idx_ref[w * TOK_BLOCK + j]
    lo = pl.multiple_of(word & 0xFFFF, ROW_WORDS)
    hi = pl.multiple_of(lax.shift_right_logical(word, 16), ROW_WORDS)
    return ((w, lo), (w + N_PAIRS // 2, hi))


def _gather_rows(idx_ref, tab_ref, g_ref, j):
    for w in range(N_PAIRS // 2):
        for k, off in _pair_offsets(idx_ref, w, j):
            g_ref[pl.ds(k * ROW_WORDS, ROW_WORDS), :] = tab_ref[pl.ds(off, ROW_WORDS), :]


def _two_token_step(consume, gather, g0_ref, g1_ref, j):
    consume(g0_ref, j)
    consume(g1_ref, j + 1)
    gather(g0_ref, jnp.minimum(j + 2, TOK_BLOCK - 1))
    gather(g1_ref, jnp.minimum(j + 3, TOK_BLOCK - 1))


def _expert_u_kernel(idx_ref, tab_hbm, h2_ref, gate_ref, gsum_ref, sc_ref,
                     w_ref, idx_next_ref, gate_next_ref,
                     tab_ref, tab_sem, g0_ref, g1_ref, xs_ref, araw_ref, exp_ref):
    TB = TOK_BLOCK
    nchunk = D_MODEL // LANES
    group = 2 * LANES // nchunk
    pitch = ARAW_PITCH
    _load_table(tab_hbm, tab_ref, tab_sem)

    for c in range(nchunk):
        xs_ref[pl.ds(c, TB, stride=nchunk), :] = h2_ref[:, c * LANES:(c + 1) * LANES].astype(F32)

    def dots(g_ref, j):
        xj = xs_ref[pl.ds(pl.multiple_of(j * nchunk, nchunk), nchunk), :].astype(BF16)
        for g in range(N_PAIRS // group):
            wgt = pltpu.bitcast(g_ref[pl.ds(g * LANES, LANES), :], BF16)
            r = lax.dot_general(xj, wgt, NT_DIMS, preferred_element_type=F32)
            for half in range(2):
                cc = 2 * g + half
                araw_ref[pl.ds(cc * nchunk * pitch + j, nchunk, stride=pitch), :] = r[:, half * LANES:(half + 1) * LANES]

    gather = functools.partial(_gather_rows, idx_ref, tab_ref)
    tokens_per_head = TB // PEER_HEADS
    gather(g0_ref, 0)
    gather(g1_ref, 1)

    def head_step(hd, c):
        _topk_head(hd, sc_ref, gate_next_ref, exp_ref)
        for s in range(tokens_per_head // 2):
            _two_token_step(dots, gather, g0_ref, g1_ref, hd * tokens_per_head + 2 * s)
        return c

    lax.fori_loop(0, PEER_HEADS, head_step, 0)
    idx_next_ref[0] = _pack_offsets(exp_ref)

    lane_chunk = lax.broadcasted_iota(I32, (TB, LANES), 1) % nchunk
    a = jnp.zeros((TB, N_PAIRS), F32)
    for cc in range(nchunk):
        part = jnp.zeros((TB, LANES), F32)
        for c in range(nchunk):
            part = part + jnp.where(lane_chunk == c, araw_ref[pl.ds((cc * nchunk + c) * pitch, TB), :], 0.0)
        a = a + _dot_sel_right(part, gsum_ref[cc * LANES:(cc + 1) * LANES, :])
    w_ref[...] = _gated_gelu(a, gate_ref[0])


def _gated_gelu(a, gates):
    return gates.T * (0.5 * a * (1.0 + lax.erf(a * (2.0 ** -0.5))))


def _gated_gelu_kernel(a_ref, gate_ref, w_ref):
    w_ref[...] = _gated_gelu(a_ref[...], gate_ref[0])


def _gated_gelu_call(a, gates):
    nblk = gates.shape[0]
    return pl.pallas_call(
        _gated_gelu_kernel,
        grid=(nblk,),
        in_specs=[pl.BlockSpec((TOK_BLOCK, N_PAIRS), lambda i: (i, 0)),
                  pl.BlockSpec((1, N_PAIRS, TOK_BLOCK), lambda i: (i, 0, 0))],
        out_specs=pl.BlockSpec((TOK_BLOCK, N_PAIRS), lambda i: (i, 0)),
        out_shape=jax.ShapeDtypeStruct(a.shape, F32),
        compiler_params=pltpu.CompilerParams(dimension_semantics=("arbitrary",)),
        name="gated_gelu",
    )(a, gates)


def _pair_expand():
    pair_of_col = np.arange(D_MODEL) // SUBLANES
    return (np.arange(N_PAIRS)[:, None] == pair_of_col[None, :]).astype(np.float32)


def _expert_u_call(idx_words, tab, h2, gates, scores_next):
    nchunk = D_MODEL // LANES
    nblk = h2.shape[0] // TOK_BLOCK
    return pl.pallas_call(
        _expert_u_kernel,
        grid=(nblk,),
        in_specs=[
            pl.BlockSpec((N_PAIRS // 2 * TOK_BLOCK,), lambda i: (i,), memory_space=pltpu.SMEM),
            pl.BlockSpec(memory_space=pl.ANY),
            pl.BlockSpec((TOK_BLOCK, D_MODEL), lambda i: (i, 0)),
            pl.BlockSpec((1, N_PAIRS, TOK_BLOCK), lambda i: (i, 0, 0)),
            pl.BlockSpec((D_MODEL, N_PAIRS), lambda i: (0, 0)),
            pl.BlockSpec((2 * PEER_HEADS, N_KEYS, TOK_BLOCK), lambda i: (0, 0, i)),
        ],
        out_specs=[
            pl.BlockSpec((TOK_BLOCK, N_PAIRS), lambda i: (i, 0)),
            pl.BlockSpec((1, N_PAIRS // 2, TOK_BLOCK), lambda i: (i, 0, 0)),
            pl.BlockSpec((1, N_PAIRS, TOK_BLOCK), lambda i: (i, 0, 0)),
        ],
        out_shape=[
            jax.ShapeDtypeStruct((nblk * TOK_BLOCK, N_PAIRS), F32),
            jax.ShapeDtypeStruct((nblk, N_PAIRS // 2, TOK_BLOCK), I32),
            jax.ShapeDtypeStruct((nblk, N_PAIRS, TOK_BLOCK), F32),
        ],
        scratch_shapes=[
            pltpu.VMEM(tab.shape, I32),
            pltpu.SemaphoreType.DMA,
            pltpu.VMEM((N_PAIRS * ROW_WORDS, LANES), I32),
            pltpu.VMEM((N_PAIRS * ROW_WORDS, LANES), I32),
            pltpu.VMEM((TOK_BLOCK * nchunk, LANES), F32),
            pltpu.VMEM((nchunk * nchunk * ARAW_PITCH, LANES), F32),
            pltpu.VMEM((N_PAIRS, TOK_BLOCK), I32),
        ],
        compiler_params=pltpu.CompilerParams(
            dimension_semantics=("arbitrary",), vmem_limit_bytes=VMEM_LIMIT_EXPERT),
        name="expert_u",
    )(idx_words, tab, h2, gates, jnp.asarray(_pair_expand().T, BF16), scores_next)


def _sc_rows_call(table_words, ids, vec, dot):
    ts = ids.shape[0]
    row_len = table_words.shape[1]
    vec_len = vec.shape[1]
    out_len = N_PAIRS if dot else D_MODEL
    workers = SC_CORES * SC_SUBCORES
    tpw = ts // workers
    nbatch = N_PAIRS // SC_ROWS
    groups = LANES // SC_LANES
    mesh = plsc.VectorSubcoreMesh(core_axis_name="c", subcore_axis_name="s")
    params = pltpu.CompilerParams()
    if "needs_layout_passes" in pltpu.CompilerParams.__dataclass_fields__:
        params = dataclasses.replace(params, needs_layout_passes=False)

    @functools.partial(
        pl.kernel, mesh=mesh, compiler_params=params,
        out_type=jax.ShapeDtypeStruct((ts, out_len), F32),
        scratch_types=[
            pltpu.VMEM((N_PAIRS,), I32), pltpu.VMEM((N_PAIRS,), I32),
            pltpu.VMEM((vec_len,), F32), pltpu.VMEM((vec_len,), F32),
            pltpu.VMEM((SC_ROWS, row_len), I32), pltpu.VMEM((SC_ROWS, row_len), I32),
            pltpu.VMEM((out_len,), F32),
            pltpu.SemaphoreType.DMA, pltpu.SemaphoreType.DMA, pltpu.SemaphoreType.DMA, pltpu.SemaphoreType.DMA,
        ],
    )
    def sc_kernel(table_hbm, ids_hbm, vec_hbm, out_hbm, ids0, ids1, vec0, vec1, rows0, rows1, out_v,
                  sem0, sem1, sem_ids, sem_vec):
        worker = lax.axis_index("s") * SC_CORES + lax.axis_index("c")
        base = worker * tpw
        ids_v, vec_v, rows_v, sems = (ids0, ids1), (vec0, vec1), (rows0, rows1), (sem0, sem1)

        def gather(slot, b, buf):
            src = table_hbm.at[ids_v[slot].at[pl.ds(b * SC_ROWS, SC_ROWS)]]
            return pltpu.make_async_copy(src, rows_v[buf], sems[buf])

        def fetch(slot, tok):
            return (pltpu.make_async_copy(ids_hbm.at[tok], ids_v[slot], sem_ids),
                    pltpu.make_async_copy(vec_hbm.at[tok], vec_v[slot], sem_vec))

        def halves(buf, r, off):
            word = rows_v[buf][r, pl.ds(off, SC_LANES)]
            return (lax.bitcast_convert_type(word << 16, F32),
                    lax.bitcast_convert_type(word & jnp.int32(-65536), F32))

        def feature_offsets(g):
            f_lo = pl.multiple_of((2 * (g // groups)) * LANES + (g % groups) * SC_LANES, SC_LANES)
            return pl.multiple_of(g * SC_LANES, SC_LANES), f_lo, pl.multiple_of(f_lo + LANES, SC_LANES)

        def weighted_sum(slot, b, buf):
            wvs = [vec_v[slot][pl.ds(b * SC_ROWS + q * SC_LANES, SC_LANES)] for q in range(SC_ROWS // SC_LANES)]
            wrs = [jnp.broadcast_to(wvs[r // SC_LANES][r % SC_LANES], (SC_LANES,)) for r in range(SC_ROWS)]

            def group(g, c):
                off, f_lo, f_hi = feature_offsets(g)
                if b == 0:
                    acc_lo = jnp.zeros((SC_LANES,), F32)
                    acc_hi = jnp.zeros((SC_LANES,), F32)
                else:
                    acc_lo = out_v[pl.ds(f_lo, SC_LANES)]
                    acc_hi = out_v[pl.ds(f_hi, SC_LANES)]
                for r in range(SC_ROWS):
                    lo, hi = halves(buf, r, off)
                    acc_lo = acc_lo + wrs[r] * lo
                    acc_hi = acc_hi + wrs[r] * hi
                out_v[pl.ds(f_lo, SC_LANES)] = acc_lo
                out_v[pl.ds(f_hi, SC_LANES)] = acc_hi
                return c

            lax.fori_loop(0, row_len // SC_LANES, group, 0)

        def row_dots(slot, b, buf):
            def group(g, accs):
                off, f_lo, f_hi = feature_offsets(g)
                x_lo = vec_v[slot][pl.ds(f_lo, SC_LANES)]
                x_hi = vec_v[slot][pl.ds(f_hi, SC_LANES)]
                new = []
                for r in range(SC_ROWS):
                    lo, hi = halves(buf, r, off)
                    new.append(accs[r] + lo * x_lo + hi * x_hi)
                return tuple(new)

            zero = jnp.zeros((SC_LANES,), F32)
            accs = lax.fori_loop(0, row_len // SC_LANES, group, (zero,) * SC_ROWS)
            lane = lax.broadcasted_iota(I32, (SC_LANES,), 0)
            for q in range(SC_ROWS // SC_LANES):
                res = zero
                for r in range(SC_LANES):
                    res = jnp.where(lane == r, jnp.sum(accs[q * SC_LANES + r]), res)
                out_v[pl.ds(b * SC_ROWS + q * SC_LANES, SC_LANES)] = res

        accumulate = row_dots if dot else weighted_sum

        for copy in fetch(0, base):
            copy.start()
        for copy in fetch(0, base):
            copy.wait()
        gather(0, 0, 0).start()

        def token_pair(p, c):
            t0 = base + 2 * p
            t_next = base + jnp.minimum(2 * p + 2, tpw - 1)
            for q in range(2 * nbatch):
                slot, b, buf = q // nbatch, q % nbatch, q % 2
                if q == 0:
                    for copy in fetch(1, t0 + 1):
                        copy.start()
                if q == nbatch:
                    for copy in fetch(0, t_next):
                        copy.start()
                if q == nbatch - 1:
                    for copy in fetch(1, t0 + 1):
                        copy.wait()
                if q == 2 * nbatch - 1:
                    for copy in fetch(0, t_next):
                        copy.wait()
                nq = (q + 1) % (2 * nbatch)
                gather(nq // nbatch, nq % nbatch, nq % 2).start()
                gather(slot, b, buf).wait()
                accumulate(slot, b, buf)
                if b == nbatch - 1:
                    pltpu.sync_copy(out_v, out_hbm.at[t0 + slot])
            return c

        lax.fori_loop(0, tpw // 2, token_pair, 0)
        gather(0, 0, 0).wait()

    return sc_kernel(table_words, ids, vec)


def _residual_norm_kernel(*refs):
    x1_ref, y_ref, nf_ref, out_ref = refs[-4:]
    out_ref[...] = _rms(x1_ref[...] + y_ref[...], nf_ref[...])


def _residual_norm_call(out_prev, x1, y, norm_f, chunk, total_tokens):
    nblk = y.shape[0] // NORM_BLOCK
    here = pl.BlockSpec((NORM_BLOCK, D_MODEL), lambda i: (i + chunk * nblk, 0))
    local = pl.BlockSpec((NORM_BLOCK, D_MODEL), lambda i: (i, 0))
    in_specs = [local, local, pl.BlockSpec((1, D_MODEL), lambda i: (0, 0))]
    args = (x1, y, norm_f.reshape(1, D_MODEL))
    aliases = {}
    if out_prev is not None:
        in_specs = [pl.BlockSpec(memory_space=pl.ANY)] + in_specs
        args = (out_prev,) + args
        aliases = {0: 0}
    return pl.pallas_call(
        _residual_norm_kernel,
        grid=(nblk,),
        in_specs=in_specs,
        out_specs=here,
        out_shape=jax.ShapeDtypeStruct((total_tokens, D_MODEL), F32),
        input_output_aliases=aliases,
        compiler_params=pltpu.CompilerParams(dimension_semantics=("arbitrary",)),
        name="residual_norm",
    )(*args)


def _expert_ids(idx_words):
    lo = (idx_words & 0xFFFF) // ROW_WORDS
    hi = lax.shift_right_logical(idx_words, 16) // ROW_WORDS
    ids = jnp.concatenate([lo, hi], axis=1)
    return ids.transpose(0, 2, 1).reshape(-1, N_PAIRS)


def kernel(x, norm_mix, w_in, conv_w, conv_gain, hg_lb_logits, hg_gain, w_out, norm_ffn, peer_wq, peer_keys,
           peer_u, peer_v, norm_f):
    B, S, D = x.shape
    depth = norm_mix.shape[0]
    assert depth == 1 and D == D_MODEL and S % SEQ_BLOCK == 0 and B % N_CHUNKS == 0 and S % TOK_BLOCK == 0
    T = B * S
    bc = B // N_CHUNKS
    nb = bc * S // TOK_BLOCK
    params = _mixer_params(norm_mix[0], w_in[0], conv_w[0], conv_gain[0], hg_lb_logits, hg_gain[0],
                           w_out[0], norm_ffn[0], peer_wq[0], peer_keys[0])
    tab_u = _pack_table(peer_u[0], one_row_per_entry=False)
    tab_u_rows = _pack_table(peer_u[0], one_row_per_entry=True)
    tab_v = _pack_table(peer_v[0], one_row_per_entry=True)
    mixed = [_mixer_call(x, params, 0, bc)]
    idx_k, gates_k = None, None
    out = None
    for k in range(N_CHUNKS):
        if k + 1 < N_CHUNKS:
            mixed.append(_mixer_call(x, params, (k + 1) * bc, bc))
        x1_k, h2_k, scores_k = mixed[k]
        if k == 0:
            idx_k, gates_k = _topk_call(scores_k)
        scores_next = mixed[min(k + 1, N_CHUNKS - 1)][2]
        ids_k = _expert_ids(idx_k)
        h2_k = h2_k.reshape(bc * S, D)
        if k in SC_U_CHUNKS:
            a_k = _sc_rows_call(tab_u_rows, ids_k, h2_k.astype(F32), dot=True)
            w_k = _gated_gelu_call(a_k, gates_k)
            idx_k, gates_k = _topk_call(scores_next)
        else:
            w_k, idx_k, gates_k = _expert_u_call(idx_k.reshape(-1), tab_u, h2_k, gates_k, scores_next)
        y_k = _sc_rows_call(tab_v, ids_k, w_k, dot=False)
        out = _residual_norm_call(out, x1_k.reshape(bc * S, D), y_k, norm_f, k, T)
    return out.reshape(B, S, D)
```

```python
import dataclasses
import functools

import numpy as np
import jax
import jax.numpy as jnp
from jax import lax
from jax.experimental import pallas as pl
from jax.experimental.pallas import tpu as pltpu
from jax.experimental.pallas import tpu_sc as plsc

F32 = jnp.float32
BF16 = jnp.bfloat16
I32 = jnp.int32

D_MODEL = 1024
D_CONV = 512
CONV_GROUP = 64
CONV_WIDTH = 3
HG_HEADS = 4
HG_KD = 128
HG_VD = 128
D_HG = HG_HEADS * HG_KD
D_IN = 3 * D_CONV + 4 * D_HG
CHUNK = 32
PEER_HEADS = 8
N_KEYS = 128
HALF_Q = 128
PEER_TOPK = 16
N_PAIRS = PEER_HEADS * PEER_TOPK
EPS = 1e-6

LANES = 128
SUBLANES = 8
SEQ_BLOCK = 256
TOK_BLOCK = 128
ROW_WORDS = D_MODEL // (2 * LANES)
ARAW_PITCH = TOK_BLOCK + 1
N_CHUNKS = 16
PACK_ROWS = 512
NORM_BLOCK = 512
SC_CORES = 2
SC_SUBCORES = 16
SC_LANES = 16
SC_ROWS = 32
SC_U_CHUNKS = (5, 11)
VMEM_LIMIT_MIXER = 48 * 1024 * 1024
VMEM_LIMIT_EXPERT = 52 * 1024 * 1024

NT_DIMS = (((1,), (1,)), ((), ()))


def _bdot(a, b):
    return jnp.dot(a.astype(BF16), b.astype(BF16), preferred_element_type=F32)


def _bdot_nt(a, b):
    return lax.dot_general(a.astype(BF16), b.astype(BF16), NT_DIMS, preferred_element_type=F32)


def _split(a, pieces):
    out = []
    for _ in range(pieces - 1):
        p = a.astype(BF16)
        out.append(p)
        a = a - p.astype(F32)
    return out + [a.astype(BF16)]


def _dot_sel(sel, a, pieces=3):
    out = None
    for piece in _split(a, pieces):
        d = jnp.dot(sel, piece, preferred_element_type=F32)
        out = d if out is None else out + d
    return out


def _dot_sel_right(a, sel, pieces=3):
    out = None
    for piece in _split(a, pieces):
        d = jnp.dot(piece, sel, preferred_element_type=F32)
        out = d if out is None else out + d
    return out


def _silu(x):
    return x * (1.0 / (1.0 + jnp.exp(-x)))


def _sigmoid(x):
    return 1.0 / (1.0 + jnp.exp(-x))


def _rms(x, gain):
    return x * lax.rsqrt(jnp.mean(x * x, axis=-1, keepdims=True) + EPS) * gain


def _mixer_kernel(x_ref, nmix_ref, win_ref, convw_ref, cgain_ref, lbl_ref, hgain_ref, wout_ref,
                  nffn_ref, wq_ref, keys_ref, cum_ref, gmean_ref, hmean_ref,
                  x1_ref, h2_ref, sc_ref,
                  up_ref, st_ref):
    L = SEQ_BLOCK
    sb = pl.program_id(1)

    @pl.when(sb == 0)
    def _():
        up_ref[0:SUBLANES, :] = jnp.zeros((SUBLANES, D_CONV), F32)
        st_ref[...] = jnp.zeros(st_ref.shape, F32)

    x = x_ref[0]
    h = _rms(x, nmix_ref[...])
    proj = _bdot(h, win_ref[...])

    u = proj[:, 0:D_CONV] * proj[:, D_CONV:2 * D_CONV]
    up_ref[SUBLANES:SUBLANES + L, :] = u
    u1 = up_ref[SUBLANES - 1:SUBLANES - 1 + L, :]
    u2 = up_ref[SUBLANES - 2:SUBLANES - 2 + L, :]
    cw = convw_ref[...]
    y = cw[0:1, :] * u2 + cw[1:2, :] * u1 + cw[2:3, :] * u
    y = proj[:, 2 * D_CONV:3 * D_CONV] * y
    up_ref[0:SUBLANES, :] = up_ref[L:L + SUBLANES, :]
    gms = _dot_sel_right(y * y, gmean_ref[...], pieces=2)
    conv_out = y * lax.rsqrt(gms + EPS) * cgain_ref[...]

    o0 = 3 * D_CONV
    qp = proj[:, o0:o0 + D_HG]
    fp = proj[:, o0 + D_HG:o0 + 2 * D_HG]
    vv = proj[:, o0 + 2 * D_HG:o0 + 3 * D_HG]
    gg = proj[:, o0 + 3 * D_HG:o0 + 4 * D_HG]
    lbl = lbl_ref[...]
    lbe = jnp.exp(lbl - jnp.max(lbl, axis=0, keepdims=True))
    lb = lbe[0:1, :] / jnp.sum(lbe, axis=0, keepdims=True)
    qh = _silu(qp) * (HG_KD ** -0.5)
    f = lb + (1.0 - lb) * _sigmoid(fp)
    kh = 1.0 - f
    logf = jnp.log(f)
    cums = _dot_sel(cum_ref[...], logf)
    gcum = cums[0:L, :]
    glast = cums[L:2 * L, :]
    q_dec = qh * jnp.exp(gcum)
    k_dec = kh * jnp.exp(-gcum)
    k_end = kh * jnp.exp(glast - gcum)
    decay = jnp.exp(glast)

    row = lax.broadcasted_iota(I32, (L, L), 0)
    col = lax.broadcasted_iota(I32, (L, L), 1)
    causal = (row // CHUNK == col // CHUNK) & (col <= row)

    o_heads = []
    for hh in range(HG_HEADS):
        cs = slice(hh * HG_KD, (hh + 1) * HG_KD)
        qd, kd, ke, vh, dc = q_dec[:, cs], k_dec[:, cs], k_end[:, cs], vv[:, cs], decay[:, cs]
        a = _bdot_nt(qd, kd)
        a = jnp.where(causal, a, 0.0)
        o_intra = _bdot(a, vh)
        st = st_ref[hh]
        pieces = []
        for n in range(L // CHUNK):
            rs = slice(n * CHUNK, (n + 1) * CHUNK)
            pieces.append(_bdot_nt(qd[rs], st))
            st = dc[n * CHUNK:n * CHUNK + 1, :] * st + _bdot(vh[rs].T, ke[rs])
        st_ref[hh] = st
        o_heads.append(o_intra + jnp.concatenate(pieces, axis=0))
    o = jnp.concatenate(o_heads, axis=1)
    oms = _dot_sel_right(o * o, hmean_ref[...], pieces=2)
    o = o * lax.rsqrt(oms + EPS) * hgain_ref[...]
    o = o * _silu(gg)

    mix = jnp.concatenate([conv_out, o], axis=1)
    x1 = x + _bdot(mix, wout_ref[...])
    x1_ref[0] = x1
    h2 = _rms(x1, nffn_ref[...])
    h2_ref[0] = h2.astype(BF16)
    q2 = _bdot(h2, wq_ref[...])
    for hp in range(2 * PEER_HEADS):
        sc_ref[hp] = _bdot_nt(keys_ref[hp], q2[:, hp * HALF_Q:(hp + 1) * HALF_Q])


def _mixer_params(norm_mix, w_in, conv_w, conv_gain, lb_logits, hg_gain, w_out, norm_ffn, wq, keys):
    L = SEQ_BLOCK
    r = np.arange(L)
    same = (r[:, None] // CHUNK) == (r[None, :] // CHUNK)
    cum = np.concatenate([same & (r[None, :] <= r[:, None]), same], axis=0).astype(np.float32)
    c = np.arange(D_CONV)
    gmean = ((c[:, None] // CONV_GROUP) == (c[None, :] // CONV_GROUP)).astype(np.float32) / CONV_GROUP
    hmean = ((c[:, None] // HG_VD) == (c[None, :] // HG_VD)).astype(np.float32) / HG_VD
    return (norm_mix.reshape(1, D_MODEL), w_in.astype(BF16), conv_w, conv_gain.reshape(1, D_CONV), lb_logits,
            jnp.tile(hg_gain.reshape(1, HG_VD), (1, HG_HEADS)), w_out.astype(BF16), norm_ffn.reshape(1, D_MODEL),
            wq.astype(BF16), keys.reshape(2 * PEER_HEADS, N_KEYS, HALF_Q).astype(BF16),
            jnp.asarray(cum, BF16), jnp.asarray(gmean, BF16), jnp.asarray(hmean, BF16))


def _mixer_call(x, params, b0, nbatch):
    S = x.shape[1]
    L = SEQ_BLOCK
    nsb = S // L
    const = lambda a: pl.BlockSpec(a.shape, lambda b, s: (0,) * a.ndim)
    return pl.pallas_call(
        _mixer_kernel,
        grid=(nbatch, nsb),
        in_specs=[pl.BlockSpec((1, L, D_MODEL), lambda b, s: (b + b0, s, 0))] + [const(a) for a in params],
        out_specs=[
            pl.BlockSpec((1, L, D_MODEL), lambda b, s: (b, s, 0)),
            pl.BlockSpec((1, L, D_MODEL), lambda b, s: (b, s, 0)),
            pl.BlockSpec((2 * PEER_HEADS, N_KEYS, L), lambda b, s: (0, 0, b * nsb + s)),
        ],
        out_shape=[
            jax.ShapeDtypeStruct((nbatch, S, D_MODEL), F32),
            jax.ShapeDtypeStruct((nbatch, S, D_MODEL), BF16),
            jax.ShapeDtypeStruct((2 * PEER_HEADS, N_KEYS, nbatch * S), F32),
        ],
        scratch_shapes=[
            pltpu.VMEM((L + 2 * SUBLANES, D_CONV), F32),
            pltpu.VMEM((HG_HEADS, HG_VD, HG_KD), F32),
        ],
        compiler_params=pltpu.CompilerParams(
            dimension_semantics=("arbitrary", "arbitrary"), vmem_limit_bytes=VMEM_LIMIT_MIXER),
        name="mixer",
    )(x, *params)


def _extract_topk(vals, ids, n, sentinel):
    tops, sel = [], []
    for _ in range(n):
        m = jnp.max(vals, axis=0, keepdims=True)
        i = jnp.min(jnp.where(vals == m, ids, sentinel), axis=0, keepdims=True)
        tops.append(m)
        sel.append(i)
        vals = jnp.where(ids == i, -jnp.inf, vals)
    return jnp.concatenate(tops, axis=0), jnp.concatenate(sel, axis=0)


def _topk_head(hd, sc_ref, gate_ref, exp_ref):
    TB = TOK_BLOCK
    K = PEER_TOPK
    iota_keys = lax.broadcasted_iota(I32, (N_KEYS, TB), 0).astype(F32)
    sub = lax.broadcasted_iota(I32, (SUBLANES, TB), 0)
    subf = sub.astype(F32)
    s1, i1t = _extract_topk(sc_ref[2 * hd], iota_keys, K, N_KEYS)
    s2, i2t = _extract_topk(sc_ref[2 * hd + 1], iota_keys, K, N_KEYS)
    vals, ids = [], []
    for b0 in range(0, K, SUBLANES):
        vals.append(s1[0:1, :] + s2[b0:b0 + SUBLANES, :])
        ids.append(subf + b0)
    for a in range(1, SUBLANES):
        v = s1[a:a + 1, :] + s2[0:SUBLANES, :]
        vals.append(jnp.where(sub < K // (a + 1), v, -jnp.inf))
        ids.append(subf + a * K)
    vals.append(s1[SUBLANES:K, :] + s2[0:1, :])
    ids.append((subf + SUBLANES) * K)
    c_score, c_idx = _extract_topk(jnp.concatenate(vals, axis=0), jnp.concatenate(ids, axis=0), K, K * K)
    c_idx = c_idx.astype(I32)
    i1t = i1t.astype(I32)
    i2t = i2t.astype(I32)
    a_sel = lax.shift_right_logical(c_idx, 4)
    b_sel = c_idx & (K - 1)
    i1 = jnp.zeros((K, TB), I32)
    i2 = jnp.zeros((K, TB), I32)
    for a in range(K):
        i1 = i1 + jnp.where(a_sel == a, i1t[a:a + 1, :], 0)
        i2 = i2 + jnp.where(b_sel == a, i2t[a:a + 1, :], 0)
    e = jnp.exp(c_score - jnp.max(c_score, axis=0, keepdims=True))
    row0 = pl.multiple_of(hd * K, K)
    gate_ref[0, pl.ds(row0, K), :] = e / jnp.sum(e, axis=0, keepdims=True)
    exp_ref[pl.ds(row0, K), :] = (i1 * N_KEYS + i2) * ROW_WORDS


def _pack_offsets(exp_ref):
    half = N_PAIRS // 2
    return exp_ref[0:half, :] | (exp_ref[half:N_PAIRS, :] << 16)


def _topk_kernel(sc_ref, idx_ref, gate_ref, exp_ref):
    def head(hd, c):
        _topk_head(hd, sc_ref, gate_ref, exp_ref)
        return c

    lax.fori_loop(0, PEER_HEADS, head, 0)
    idx_ref[0] = _pack_offsets(exp_ref)


def _topk_call(scores_t):
    nblk = scores_t.shape[-1] // TOK_BLOCK
    return pl.pallas_call(
        _topk_kernel,
        grid=(nblk,),
        in_specs=[pl.BlockSpec((2 * PEER_HEADS, N_KEYS, TOK_BLOCK), lambda i: (0, 0, i))],
        out_specs=[
            pl.BlockSpec((1, N_PAIRS // 2, TOK_BLOCK), lambda i: (i, 0, 0)),
            pl.BlockSpec((1, N_PAIRS, TOK_BLOCK), lambda i: (i, 0, 0)),
        ],
        out_shape=[
            jax.ShapeDtypeStruct((nblk, N_PAIRS // 2, TOK_BLOCK), I32),
            jax.ShapeDtypeStruct((nblk, N_PAIRS, TOK_BLOCK), F32),
        ],
        scratch_shapes=[pltpu.VMEM((N_PAIRS, TOK_BLOCK), I32)],
        compiler_params=pltpu.CompilerParams(dimension_semantics=("arbitrary",)),
        name="topk",
    )(scores_t)


def _pack_kernel(tab_ref, out_ref, *, one_row_per_entry):
    rows = tab_ref.shape[0]
    bits = lambda v: lax.bitcast_convert_type(v.astype(BF16).astype(F32), jnp.uint32)
    for i in range(ROW_WORDS):
        lo = bits(tab_ref[:, (2 * i) * LANES:(2 * i + 1) * LANES]) >> 16
        hi = bits(tab_ref[:, (2 * i + 1) * LANES:(2 * i + 2) * LANES]) & jnp.uint32(0xFFFF0000)
        words = lax.bitcast_convert_type(lo | hi, I32)
        if one_row_per_entry:
            out_ref[:, i * LANES:(i + 1) * LANES] = words
        else:
            out_ref[pl.ds(i, rows, stride=ROW_WORDS), :] = words


def _pack_table(tab, one_row_per_entry):
    n = tab.shape[0]
    if one_row_per_entry:
        out_block, out_shape = (PACK_ROWS, ROW_WORDS * LANES), (n, ROW_WORDS * LANES)
    else:
        out_block, out_shape = (PACK_ROWS * ROW_WORDS, LANES), (n * ROW_WORDS, LANES)
    return pl.pallas_call(
        functools.partial(_pack_kernel, one_row_per_entry=one_row_per_entry),
        grid=(n // PACK_ROWS,),
        in_specs=[pl.BlockSpec((PACK_ROWS, D_MODEL), lambda i: (i, 0))],
        out_specs=pl.BlockSpec(out_block, lambda i: (i, 0)),
        out_shape=jax.ShapeDtypeStruct(out_shape, I32),
        compiler_params=pltpu.CompilerParams(dimension_semantics=("arbitrary",)),
        name="pack_table",
    )(tab)


def _load_table(tab_hbm, tab_ref, sem):
    @pl.when(pl.program_id(0) == 0)
    def _():
        copy = pltpu.make_async_copy(tab_hbm, tab_ref, sem)
        copy.start()
        copy.wait()


def _pair_offsets(idx_ref, w, j):
    word = idx_ref[w * TOK_BLOCK + j]
    lo = pl.multiple_of(word & 0xFFFF, ROW_WORDS)
    hi = pl.multiple_of(lax.shift_right_logical(word, 16), ROW_WORDS)
    return ((w, lo), (w + N_PAIRS // 2, hi))


def _gather_rows(idx_ref, tab_ref, g_ref, j):
    for w in range(N_PAIRS // 2):
        for k, off in _pair_offsets(idx_ref, w, j):
            g_ref[pl.ds(k * ROW_WORDS, ROW_WORDS), :] = tab_ref[pl.ds(off, ROW_WORDS), :]


def _two_token_step(consume, gather, g0_ref, g1_ref, j):
    consume(g0_ref, j)
    consume(g1_ref, j + 1)
    gather(g0_ref, jnp.minimum(j + 2, TOK_BLOCK - 1))
    gather(g1_ref, jnp.minimum(j + 3, TOK_BLOCK - 1))


def _expert_u_kernel(idx_ref, tab_hbm, h2_ref, gate_ref, gsum_ref, sc_ref,
                     w_ref, idx_next_ref, gate_next_ref,
                     tab_ref, tab_sem, g0_ref, g1_ref, xs_ref, araw_ref, exp_ref):
    TB = TOK_BLOCK
    nchunk = D_MODEL // LANES
    group = 2 * LANES // nchunk
    pitch = ARAW_PITCH
    _load_table(tab_hbm, tab_ref, tab_sem)

    for c in range(nchunk):
        xs_ref[pl.ds(c, TB, stride=nchunk), :] = h2_ref[:, c * LANES:(c + 1) * LANES].astype(F32)

    def dots(g_ref, j):
        xj = xs_ref[pl.ds(pl.multiple_of(j * nchunk, nchunk), nchunk), :].astype(BF16)
        for g in range(N_PAIRS // group):
            wgt = pltpu.bitcast(g_ref[pl.ds(g * LANES, LANES), :], BF16)
            r = lax.dot_general(xj, wgt, NT_DIMS, preferred_element_type=F32)
            for half in range(2):
                cc = 2 * g + half
                araw_ref[pl.ds(cc * nchunk * pitch + j, nchunk, stride=pitch), :] = r[:, half * LANES:(half + 1) * LANES]

    gather = functools.partial(_gather_rows, idx_ref, tab_ref)
    tokens_per_head = TB // PEER_HEADS
    gather(g0_ref, 0)
    gather(g1_ref, 1)

    def head_step(hd, c):
        _topk_head(hd, sc_ref, gate_next_ref, exp_ref)
        for s in range(tokens_per_head // 2):
            _two_token_step(dots, gather, g0_ref, g1_ref, hd * tokens_per_head + 2 * s)
        return c

    lax.fori_loop(0, PEER_HEADS, head_step, 0)
    idx_next_ref[0] = _pack_offsets(exp_ref)

    lane_chunk = lax.broadcasted_iota(I32, (TB, LANES), 1) % nchunk
    a = jnp.zeros((TB, N_PAIRS), F32)
    for cc in range(nchunk):
        part = jnp.zeros((TB, LANES), F32)
        for c in range(nchunk):
            part = part + jnp.where(lane_chunk == c, araw_ref[pl.ds((cc * nchunk + c) * pitch, TB), :], 0.0)
        a = a + _dot_sel_right(part, gsum_ref[cc * LANES:(cc + 1) * LANES, :])
    w_ref[...] = _gated_gelu(a, gate_ref[0])


def _gated_gelu(a, gates):
    return gates.T * (0.5 * a * (1.0 + lax.erf(a * (2.0 ** -0.5))))


def _gated_gelu_kernel(a_ref, gate_ref, w_ref):
    w_ref[...] = _gated_gelu(a_ref[...], gate_ref[0])


def _gated_gelu_call(a, gates):
    nblk = gates.shape[0]
    return pl.pallas_call(
        _gated_gelu_kernel,
        grid=(nblk,),
        in_specs=[pl.BlockSpec((TOK_BLOCK, N_PAIRS), lambda i: (i, 0)),
                  pl.BlockSpec((1, N_PAIRS, TOK_BLOCK), lambda i: (i, 0, 0))],
        out_specs=pl.BlockSpec((TOK_BLOCK, N_PAIRS), lambda i: (i, 0)),
        out_shape=jax.ShapeDtypeStruct(a.shape, F32),
        compiler_params=pltpu.CompilerParams(dimension_semantics=("arbitrary",)),
        name="gated_gelu",
    )(a, gates)


def _pair_expand():
    pair_of_col = np.arange(D_MODEL) // SUBLANES
    return (np.arange(N_PAIRS)[:, None] == pair_of_col[None, :]).astype(np.float32)


def _expert_u_call(idx_words, tab, h2, gates, scores_next):
    nchunk = D_MODEL // LANES
    nblk = h2.shape[0] // TOK_BLOCK
    return pl.pallas_call(
        _expert_u_kernel,
        grid=(nblk,),
        in_specs=[
            pl.BlockSpec((N_PAIRS // 2 * TOK_BLOCK,), lambda i: (i,), memory_space=pltpu.SMEM),
            pl.BlockSpec(memory_space=pl.ANY),
            pl.BlockSpec((TOK_BLOCK, D_MODEL), lambda i: (i, 0)),
            pl.BlockSpec((1, N_PAIRS, TOK_BLOCK), lambda i: (i, 0, 0)),
            pl.BlockSpec((D_MODEL, N_PAIRS), lambda i: (0, 0)),
            pl.BlockSpec((2 * PEER_HEADS, N_KEYS, TOK_BLOCK), lambda i: (0, 0, i)),
        ],
        out_specs=[
            pl.BlockSpec((TOK_BLOCK, N_PAIRS), lambda i: (i, 0)),
            pl.BlockSpec((1, N_PAIRS // 2, TOK_BLOCK), lambda i: (i, 0, 0)),
            pl.BlockSpec((1, N_PAIRS, TOK_BLOCK), lambda i: (i, 0, 0)),
        ],
        out_shape=[
            jax.ShapeDtypeStruct((nblk * TOK_BLOCK, N_PAIRS), F32),
            jax.ShapeDtypeStruct((nblk, N_PAIRS // 2, TOK_BLOCK), I32),
            jax.ShapeDtypeStruct((nblk, N_PAIRS, TOK_BLOCK), F32),
        ],
        scratch_shapes=[
            pltpu.VMEM(tab.shape, I32),
            pltpu.SemaphoreType.DMA,
            pltpu.VMEM((N_PAIRS * ROW_WORDS, LANES), I32),
            pltpu.VMEM((N_PAIRS * ROW_WORDS, LANES), I32),
            pltpu.VMEM((TOK_BLOCK * nchunk, LANES), F32),
            pltpu.VMEM((nchunk * nchunk * ARAW_PITCH, LANES), F32),
            pltpu.VMEM((N_PAIRS, TOK_BLOCK), I32),
        ],
        compiler_params=pltpu.CompilerParams(
            dimension_semantics=("arbitrary",), vmem_limit_bytes=VMEM_LIMIT_EXPERT),
        name="expert_u",
    )(idx_words, tab, h2, gates, jnp.asarray(_pair_expand().T, BF16), scores_next)


def _sc_rows_call(table_words, ids, vec, dot, after):
    ts = ids.shape[0]
    row_len = table_words.shape[1]
    vec_len = vec.shape[1]
    out_len = N_PAIRS if dot else D_MODEL
    workers = SC_CORES * SC_SUBCORES
    tpw = ts // workers
    nbatch = N_PAIRS // SC_ROWS
    groups = LANES // SC_LANES
    mesh = plsc.VectorSubcoreMesh(core_axis_name="c", subcore_axis_name="s")
    params = pltpu.CompilerParams()
    if "needs_layout_passes" in pltpu.CompilerParams.__dataclass_fields__:
        params = dataclasses.replace(params, needs_layout_passes=False)

    @functools.partial(
        pl.kernel, mesh=mesh, compiler_params=params,
        out_type=jax.ShapeDtypeStruct((ts, out_len), F32),
        scratch_types=[
            pltpu.VMEM((N_PAIRS,), I32), pltpu.VMEM((N_PAIRS,), I32),
            pltpu.VMEM((vec_len,), F32), pltpu.VMEM((vec_len,), F32),
            pltpu.VMEM((SC_ROWS, row_len), I32), pltpu.VMEM((SC_ROWS, row_len), I32),
            pltpu.VMEM((out_len,), F32),
            pltpu.SemaphoreType.DMA, pltpu.SemaphoreType.DMA, pltpu.SemaphoreType.DMA, pltpu.SemaphoreType.DMA,
        ],
    )
    def sc_kernel(table_hbm, ids_hbm, vec_hbm, after_hbm, out_hbm, ids0, ids1, vec0, vec1, rows0, rows1, out_v,
                  sem0, sem1, sem_ids, sem_vec):
        del after_hbm
        worker = lax.axis_index("s") * SC_CORES + lax.axis_index("c")
        base = worker * tpw
        ids_v, vec_v, rows_v, sems = (ids0, ids1), (vec0, vec1), (rows0, rows1), (sem0, sem1)

        def gather(slot, b, buf):
            src = table_hbm.at[ids_v[slot].at[pl.ds(b * SC_ROWS, SC_ROWS)]]
            return pltpu.make_async_copy(src, rows_v[buf], sems[buf])

        def fetch(slot, tok):
            return (pltpu.make_async_copy(ids_hbm.at[tok], ids_v[slot], sem_ids),
                    pltpu.make_async_copy(vec_hbm.at[tok], vec_v[slot], sem_vec))

        def halves(buf, r, off):
            word = rows_v[buf][r, pl.ds(off, SC_LANES)]
            return (lax.bitcast_convert_type(word << 16, F32),
                    lax.bitcast_convert_type(word & jnp.int32(-65536), F32))

        def feature_offsets(g):
            f_lo = pl.multiple_of((2 * (g // groups)) * LANES + (g % groups) * SC_LANES, SC_LANES)
            return pl.multiple_of(g * SC_LANES, SC_LANES), f_lo, pl.multiple_of(f_lo + LANES, SC_LANES)

        def weighted_sum(slot, b, buf):
            wvs = [vec_v[slot][pl.ds(b * SC_ROWS + q * SC_LANES, SC_LANES)] for q in range(SC_ROWS // SC_LANES)]
            wrs = [jnp.broadcast_to(wvs[r // SC_LANES][r % SC_LANES], (SC_LANES,)) for r in range(SC_ROWS)]

            def group(g, c):
                off, f_lo, f_hi = feature_offsets(g)
                if b == 0:
                    acc_lo = jnp.zeros((SC_LANES,), F32)
                    acc_hi = jnp.zeros((SC_LANES,), F32)
                else:
                    acc_lo = out_v[pl.ds(f_lo, SC_LANES)]
                    acc_hi = out_v[pl.ds(f_hi, SC_LANES)]
                for r in range(SC_ROWS):
                    lo, hi = halves(buf, r, off)
                    acc_lo = acc_lo + wrs[r] * lo
                    acc_hi = acc_hi + wrs[r] * hi
                out_v[pl.ds(f_lo, SC_LANES)] = acc_lo
                out_v[pl.ds(f_hi, SC_LANES)] = acc_hi
                return c

            lax.fori_loop(0, row_len // SC_LANES, group, 0)

        def row_dots(slot, b, buf):
            def group(g, accs):
                off, f_lo, f_hi = feature_offsets(g)
                x_lo = vec_v[slot][pl.ds(f_lo, SC_LANES)]
                x_hi = vec_v[slot][pl.ds(f_hi, SC_LANES)]
                new = []
                for r in range(SC_ROWS):
                    lo, hi = halves(buf, r, off)
                    new.append(accs[r] + lo * x_lo + hi * x_hi)
                return tuple(new)

            zero = jnp.zeros((SC_LANES,), F32)
            accs = lax.fori_loop(0, row_len // SC_LANES, group, (zero,) * SC_ROWS)
            lane = lax.broadcasted_iota(I32, (SC_LANES,), 0)
            for q in range(SC_ROWS // SC_LANES):
                res = zero
                for r in range(SC_LANES):
                    res = jnp.where(lane == r, jnp.sum(accs[q * SC_LANES + r]), res)
                out_v[pl.ds(b * SC_ROWS + q * SC_LANES, SC_LANES)] = res

        accumulate = row_dots if dot else weighted_sum

        for copy in fetch(0, base):
            copy.start()
        for copy in fetch(0, base):
            copy.wait()
        gather(0, 0, 0).start()

        def token_pair(p, c):
            t0 = base + 2 * p
            t_next = base + jnp.minimum(2 * p + 2, tpw - 1)
            for q in range(2 * nbatch):
                slot, b, buf = q // nbatch, q % nbatch, q % 2
                if q == 0:
                    for copy in fetch(1, t0 + 1):
                        copy.start()
                if q == nbatch:
                    for copy in fetch(0, t_next):
                        copy.start()
                if q == nbatch - 1:
                    for copy in fetch(1, t0 + 1):
                        copy.wait()
                if q == 2 * nbatch - 1:
                    for copy in fetch(0, t_next):
                        copy.wait()
                nq = (q + 1) % (2 * nbatch)
                gather(nq // nbatch, nq % nbatch, nq % 2).start()
                gather(slot, b, buf).wait()
                accumulate(slot, b, buf)
                if b == nbatch - 1:
                    pltpu.sync_copy(out_v, out_hbm.at[t0 + slot])
            return c

        lax.fori_loop(0, tpw // 2, token_pair, 0)
        gather(0, 0, 0).wait()

    return sc_kernel(table_words, ids, vec, after)


def _residual_norm_kernel(*refs):
    x1_ref, y_ref, nf_ref, out_ref = refs[-4:]
    out_ref[...] = _rms(x1_ref[...] + y_ref[...], nf_ref[...])


def _residual_norm_call(out_prev, x1, y, norm_f, chunk, total_tokens):
    nblk = y.shape[0] // NORM_BLOCK
    here = pl.BlockSpec((NORM_BLOCK, D_MODEL), lambda i: (i + chunk * nblk, 0))
    local = pl.BlockSpec((NORM_BLOCK, D_MODEL), lambda i: (i, 0))
    in_specs = [local, local, pl.BlockSpec((1, D_MODEL), lambda i: (0, 0))]
    args = (x1, y, norm_f.reshape(1, D_MODEL))
    aliases = {}
    if out_prev is not None:
        in_specs = [pl.BlockSpec(memory_space=pl.ANY)] + in_specs
        args = (out_prev,) + args
        aliases = {0: 0}
    return pl.pallas_call(
        _residual_norm_kernel,
        grid=(nblk,),
        in_specs=in_specs,
        out_specs=here,
        out_shape=jax.ShapeDtypeStruct((total_tokens, D_MODEL), F32),
        input_output_aliases=aliases,
        compiler_params=pltpu.CompilerParams(dimension_semantics=("arbitrary",)),
        name="residual_norm",
    )(*args)


def _expert_ids(idx_words):
    lo = (idx_words & 0xFFFF) // ROW_WORDS
    hi = lax.shift_right_logical(idx_words, 16) // ROW_WORDS
    ids = jnp.concatenate([lo, hi], axis=1)
    return ids.transpose(0, 2, 1).reshape(-1, N_PAIRS)


def kernel(x, norm_mix, w_in, conv_w, conv_gain, hg_lb_logits, hg_gain, w_out, norm_ffn, peer_wq, peer_keys,
           peer_u, peer_v, norm_f):
    B, S, D = x.shape
    depth = norm_mix.shape[0]
    assert depth == 1 and D == D_MODEL and S % SEQ_BLOCK == 0 and B % N_CHUNKS == 0 and S % TOK_BLOCK == 0
    T = B * S
    bc = B // N_CHUNKS
    nb = bc * S // TOK_BLOCK
    params = _mixer_params(norm_mix[0], w_in[0], conv_w[0], conv_gain[0], hg_lb_logits, hg_gain[0],
                           w_out[0], norm_ffn[0], peer_wq[0], peer_keys[0])
    tab_u = _pack_table(peer_u[0], one_row_per_entry=False)
    tab_u_rows = _pack_table(peer_u[0], one_row_per_entry=True)
    tab_v = _pack_table(peer_v[0], one_row_per_entry=True)
    mixed = [_mixer_call(x, params, 0, bc)]
    idx_k, gates_k = None, None
    out = None
    sc_prev = jnp.zeros((SUBLANES, LANES), F32)
    for k in range(N_CHUNKS):
        if k + 1 < N_CHUNKS:
            mixed.append(_mixer_call(x, params, (k + 1) * bc, bc))
        x1_k, h2_k, scores_k = mixed[k]
        if k == 0:
            idx_k, gates_k = _topk_call(scores_k)
        scores_next = mixed[min(k + 1, N_CHUNKS - 1)][2]
        ids_k = _expert_ids(idx_k)
        h2_k = h2_k.reshape(bc * S, D)
        if k in SC_U_CHUNKS:
            a_k = _sc_rows_call(tab_u_rows, ids_k, h2_k.astype(F32), dot=True, after=sc_prev)
            w_k = _gated_gelu_call(a_k, gates_k)
            idx_k, gates_k = _topk_call(scores_next)
            sc_prev = a_k
        else:
            w_k, idx_k, gates_k = _expert_u_call(idx_k.reshape(-1), tab_u, h2_k, gates_k, scores_next)
        y_k = sc_prev = _sc_rows_call(tab_v, ids_k, w_k, dot=False, after=sc_prev)
        out = _residual_norm_call(out, x1_k.reshape(bc * S, D), y_k, norm_f, k, T)
    return out.reshape(B, S, D)
```

```python
import dataclasses
import functools

import numpy as np
import jax
import jax.numpy as jnp
from jax import lax
from jax.experimental import pallas as pl
from jax.experimental.pallas import tpu as pltpu
from jax.experimental.pallas import tpu_sc as plsc

F32 = jnp.float32
BF16 = jnp.bfloat16
I32 = jnp.int32

D_MODEL = 1024
D_CONV = 512
CONV_GROUP = 64
CONV_WIDTH = 3
HG_HEADS = 4
HG_KD = 128
HG_VD = 128
D_HG = HG_HEADS * HG_KD
D_IN = 3 * D_CONV + 4 * D_HG
CHUNK = 32
PEER_HEADS = 8
N_KEYS = 128
HALF_Q = 128
PEER_TOPK = 16
N_PAIRS = PEER_HEADS * PEER_TOPK
EPS = 1e-6

LANES = 128
SUBLANES = 8
SEQ_BLOCK = 256
TOK_BLOCK = 128
ROW_WORDS = D_MODEL // (2 * LANES)
ARAW_PITCH = TOK_BLOCK + 1
CHUNK_ROWS = 2
TAIL_SPLIT = 2
PACK_ROWS = 512
NORM_BLOCK = 512
SC_CORES = 2
SC_SUBCORES = 16
SC_LANES = 16
SC_ROWS = 32
VMEM_LIMIT_MIXER = 48 * 1024 * 1024
VMEM_LIMIT_EXPERT = 52 * 1024 * 1024

NT_DIMS = (((1,), (1,)), ((), ()))


def _bdot(a, b):
    return jnp.dot(a.astype(BF16), b.astype(BF16), preferred_element_type=F32)


def _bdot_nt(a, b):
    return lax.dot_general(a.astype(BF16), b.astype(BF16), NT_DIMS, preferred_element_type=F32)


def _split(a, pieces):
    out = []
    for _ in range(pieces - 1):
        p = a.astype(BF16)
        out.append(p)
        a = a - p.astype(F32)
    return out + [a.astype(BF16)]


def _dot_sel(sel, a, pieces=3):
    out = None
    for piece in _split(a, pieces):
        d = jnp.dot(sel, piece, preferred_element_type=F32)
        out = d if out is None else out + d
    return out


def _dot_sel_right(a, sel, pieces=3):
    out = None
    for piece in _split(a, pieces):
        d = jnp.dot(piece, sel, preferred_element_type=F32)
        out = d if out is None else out + d
    return out


def _silu(x):
    return x * (1.0 / (1.0 + jnp.exp(-x)))


def _sigmoid(x):
    return 1.0 / (1.0 + jnp.exp(-x))


def _rms(x, gain):
    return x * lax.rsqrt(jnp.mean(x * x, axis=-1, keepdims=True) + EPS) * gain


def _mixer_kernel(x_ref, nmix_ref, win_ref, convw_ref, cgain_ref, lbl_ref, hgain_ref, wout_ref,
                  nffn_ref, wq_ref, keys_ref, cum_ref, gmean_ref, hmean_ref,
                  x1_ref, h2_ref, sc_ref,
                  up_ref, st_ref):
    L = SEQ_BLOCK
    sb = pl.program_id(1)

    @pl.when(sb == 0)
    def _():
        up_ref[0:SUBLANES, :] = jnp.zeros((SUBLANES, D_CONV), F32)
        st_ref[...] = jnp.zeros(st_ref.shape, F32)

    x = x_ref[0]
    h = _rms(x, nmix_ref[...])
    proj = _bdot(h, win_ref[...])

    u = proj[:, 0:D_CONV] * proj[:, D_CONV:2 * D_CONV]
    up_ref[SUBLANES:SUBLANES + L, :] = u
    u1 = up_ref[SUBLANES - 1:SUBLANES - 1 + L, :]
    u2 = up_ref[SUBLANES - 2:SUBLANES - 2 + L, :]
    cw = convw_ref[...]
    y = cw[0:1, :] * u2 + cw[1:2, :] * u1 + cw[2:3, :] * u
    y = proj[:, 2 * D_CONV:3 * D_CONV] * y
    up_ref[0:SUBLANES, :] = up_ref[L:L + SUBLANES, :]
    gms = _dot_sel_right(y * y, gmean_ref[...], pieces=2)
    conv_out = y * lax.rsqrt(gms + EPS) * cgain_ref[...]

    o0 = 3 * D_CONV
    qp = proj[:, o0:o0 + D_HG]
    fp = proj[:, o0 + D_HG:o0 + 2 * D_HG]
    vv = proj[:, o0 + 2 * D_HG:o0 + 3 * D_HG]
    gg = proj[:, o0 + 3 * D_HG:o0 + 4 * D_HG]
    lbl = lbl_ref[...]
    lbe = jnp.exp(lbl - jnp.max(lbl, axis=0, keepdims=True))
    lb = lbe[0:1, :] / jnp.sum(lbe, axis=0, keepdims=True)
    qh = _silu(qp) * (HG_KD ** -0.5)
    f = lb + (1.0 - lb) * _sigmoid(fp)
    kh = 1.0 - f
    logf = jnp.log(f)
    cums = _dot_sel(cum_ref[...], logf)
    gcum = cums[0:L, :]
    glast = cums[L:2 * L, :]
    q_dec = qh * jnp.exp(gcum)
    k_dec = kh * jnp.exp(-gcum)
    k_end = kh * jnp.exp(glast - gcum)
    decay = jnp.exp(glast)

    row = lax.broadcasted_iota(I32, (L, L), 0)
    col = lax.broadcasted_iota(I32, (L, L), 1)
    causal = (row // CHUNK == col // CHUNK) & (col <= row)

    o_heads = []
    for hh in range(HG_HEADS):
        cs = slice(hh * HG_KD, (hh + 1) * HG_KD)
        qd, kd, ke, vh, dc = q_dec[:, cs], k_dec[:, cs], k_end[:, cs], vv[:, cs], decay[:, cs]
        a = _bdot_nt(qd, kd)
        a = jnp.where(causal, a, 0.0)
        o_intra = _bdot(a, vh)
        st = st_ref[hh]
        pieces = []
        for n in range(L // CHUNK):
            rs = slice(n * CHUNK, (n + 1) * CHUNK)
            pieces.append(_bdot_nt(qd[rs], st))
            st = dc[n * CHUNK:n * CHUNK + 1, :] * st + _bdot(vh[rs].T, ke[rs])
        st_ref[hh] = st
        o_heads.append(o_intra + jnp.concatenate(pieces, axis=0))
    o = jnp.concatenate(o_heads, axis=1)
    oms = _dot_sel_right(o * o, hmean_ref[...], pieces=2)
    o = o * lax.rsqrt(oms + EPS) * hgain_ref[...]
    o = o * _silu(gg)

    mix = jnp.concatenate([conv_out, o], axis=1)
    x1 = x + _bdot(mix, wout_ref[...])
    x1_ref[0] = x1
    h2 = _rms(x1, nffn_ref[...])
    h2_ref[0] = h2.astype(BF16)
    q2 = _bdot(h2, wq_ref[...])
    for hp in range(2 * PEER_HEADS):
        sc_ref[hp] = _bdot_nt(keys_ref[hp], q2[:, hp * HALF_Q:(hp + 1) * HALF_Q])


def _mixer_params(norm_mix, w_in, conv_w, conv_gain, lb_logits, hg_gain, w_out, norm_ffn, wq, keys):
    L = SEQ_BLOCK
    r = np.arange(L)
    same = (r[:, None] // CHUNK) == (r[None, :] // CHUNK)
    cum = np.concatenate([same & (r[None, :] <= r[:, None]), same], axis=0).astype(np.float32)
    c = np.arange(D_CONV)
    gmean = ((c[:, None] // CONV_GROUP) == (c[None, :] // CONV_GROUP)).astype(np.float32) / CONV_GROUP
    hmean = ((c[:, None] // HG_VD) == (c[None, :] // HG_VD)).astype(np.float32) / HG_VD
    return (norm_mix.reshape(1, D_MODEL), w_in.astype(BF16), conv_w, conv_gain.reshape(1, D_CONV), lb_logits,
            jnp.tile(hg_gain.reshape(1, HG_VD), (1, HG_HEADS)), w_out.astype(BF16), norm_ffn.reshape(1, D_MODEL),
            wq.astype(BF16), keys.reshape(2 * PEER_HEADS, N_KEYS, HALF_Q).astype(BF16),
            jnp.asarray(cum, BF16), jnp.asarray(gmean, BF16), jnp.asarray(hmean, BF16))


def _mixer_call(x, params, b0, nbatch):
    S = x.shape[1]
    L = SEQ_BLOCK
    nsb = S // L
    const = lambda a: pl.BlockSpec(a.shape, lambda b, s: (0,) * a.ndim)
    return pl.pallas_call(
        _mixer_kernel,
        grid=(nbatch, nsb),
        in_specs=[pl.BlockSpec((1, L, D_MODEL), lambda b, s: (b + b0, s, 0))] + [const(a) for a in params],
        out_specs=[
            pl.BlockSpec((1, L, D_MODEL), lambda b, s: (b, s, 0)),
            pl.BlockSpec((1, L, D_MODEL), lambda b, s: (b, s, 0)),
            pl.BlockSpec((2 * PEER_HEADS, N_KEYS, L), lambda b, s: (0, 0, b * nsb + s)),
        ],
        out_shape=[
            jax.ShapeDtypeStruct((nbatch, S, D_MODEL), F32),
            jax.ShapeDtypeStruct((nbatch, S, D_MODEL), BF16),
            jax.ShapeDtypeStruct((2 * PEER_HEADS, N_KEYS, nbatch * S), F32),
        ],
        scratch_shapes=[
            pltpu.VMEM((L + 2 * SUBLANES, D_CONV), F32),
            pltpu.VMEM((HG_HEADS, HG_VD, HG_KD), F32),
        ],
        compiler_params=pltpu.CompilerParams(
            dimension_semantics=("arbitrary", "arbitrary"), vmem_limit_bytes=VMEM_LIMIT_MIXER),
        name="mixer",
    )(x, *params)


def _extract_topk(vals, ids, n, sentinel):
    tops, sel = [], []
    for _ in range(n):
        m = jnp.max(vals, axis=0, keepdims=True)
        i = jnp.min(jnp.where(vals == m, ids, sentinel), axis=0, keepdims=True)
        tops.append(m)
        sel.append(i)
        vals = jnp.where(ids == i, -jnp.inf, vals)
    return jnp.concatenate(tops, axis=0), jnp.concatenate(sel, axis=0)


def _topk_head(hd, sc_ref, gate_ref, exp_ref):
    TB = TOK_BLOCK
    K = PEER_TOPK
    iota_keys = lax.broadcasted_iota(I32, (N_KEYS, TB), 0).astype(F32)
    sub = lax.broadcasted_iota(I32, (SUBLANES, TB), 0)
    subf = sub.astype(F32)
    s1, i1t = _extract_topk(sc_ref[2 * hd], iota_keys, K, N_KEYS)
    s2, i2t = _extract_topk(sc_ref[2 * hd + 1], iota_keys, K, N_KEYS)
    vals, ids = [], []
    for b0 in range(0, K, SUBLANES):
        vals.append(s1[0:1, :] + s2[b0:b0 + SUBLANES, :])
        ids.append(subf + b0)
    for a in range(1, SUBLANES):
        v = s1[a:a + 1, :] + s2[0:SUBLANES, :]
        vals.append(jnp.where(sub < K // (a + 1), v, -jnp.inf))
        ids.append(subf + a * K)
    vals.append(s1[SUBLANES:K, :] + s2[0:1, :])
    ids.append((subf + SUBLANES) * K)
    c_score, c_idx = _extract_topk(jnp.concatenate(vals, axis=0), jnp.concatenate(ids, axis=0), K, K * K)
    c_idx = c_idx.astype(I32)
    i1t = i1t.astype(I32)
    i2t = i2t.astype(I32)
    a_sel = lax.shift_right_logical(c_idx, 4)
    b_sel = c_idx & (K - 1)
    i1 = jnp.zeros((K, TB), I32)
    i2 = jnp.zeros((K, TB), I32)
    for a in range(K):
        i1 = i1 + jnp.where(a_sel == a, i1t[a:a + 1, :], 0)
        i2 = i2 + jnp.where(b_sel == a, i2t[a:a + 1, :], 0)
    e = jnp.exp(c_score - jnp.max(c_score, axis=0, keepdims=True))
    row0 = pl.multiple_of(hd * K, K)
    gate_ref[0, pl.ds(row0, K), :] = e / jnp.sum(e, axis=0, keepdims=True)
    exp_ref[pl.ds(row0, K), :] = (i1 * N_KEYS + i2) * ROW_WORDS


def _pack_offsets(exp_ref):
    half = N_PAIRS // 2
    return exp_ref[0:half, :] | (exp_ref[half:N_PAIRS, :] << 16)


def _topk_kernel(sc_ref, idx_ref, gate_ref, exp_ref):
    def head(hd, c):
        _topk_head(hd, sc_ref, gate_ref, exp_ref)
        return c

    lax.fori_loop(0, PEER_HEADS, head, 0)
    idx_ref[0] = _pack_offsets(exp_ref)


def _topk_call(scores_t):
    nblk = scores_t.shape[-1] // TOK_BLOCK
    return pl.pallas_call(
        _topk_kernel,
        grid=(nblk,),
        in_specs=[pl.BlockSpec((2 * PEER_HEADS, N_KEYS, TOK_BLOCK), lambda i: (0, 0, i))],
        out_specs=[
            pl.BlockSpec((1, N_PAIRS // 2, TOK_BLOCK), lambda i: (i, 0, 0)),
            pl.BlockSpec((1, N_PAIRS, TOK_BLOCK), lambda i: (i, 0, 0)),
        ],
        out_shape=[
            jax.ShapeDtypeStruct((nblk, N_PAIRS // 2, TOK_BLOCK), I32),
            jax.ShapeDtypeStruct((nblk, N_PAIRS, TOK_BLOCK), F32),
        ],
        scratch_shapes=[pltpu.VMEM((N_PAIRS, TOK_BLOCK), I32)],
        compiler_params=pltpu.CompilerParams(dimension_semantics=("arbitrary",)),
        name="topk",
    )(scores_t)


def _pack_kernel(tab_ref, out_ref, *, one_row_per_entry):
    rows = tab_ref.shape[0]
    bits = lambda v: lax.bitcast_convert_type(v.astype(BF16).astype(F32), jnp.uint32)
    for i in range(ROW_WORDS):
        lo = bits(tab_ref[:, (2 * i) * LANES:(2 * i + 1) * LANES]) >> 16
        hi = bits(tab_ref[:, (2 * i + 1) * LANES:(2 * i + 2) * LANES]) & jnp.uint32(0xFFFF0000)
        words = lax.bitcast_convert_type(lo | hi, I32)
        if one_row_per_entry:
            out_ref[:, i * LANES:(i + 1) * LANES] = words
        else:
            out_ref[pl.ds(i, rows, stride=ROW_WORDS), :] = words


def _pack_table(tab, one_row_per_entry):
    n = tab.shape[0]
    if one_row_per_entry:
        out_block, out_shape = (PACK_ROWS, ROW_WORDS * LANES), (n, ROW_WORDS * LANES)
    else:
        out_block, out_shape = (PACK_ROWS * ROW_WORDS, LANES), (n * ROW_WORDS, LANES)
    return pl.pallas_call(
        functools.partial(_pack_kernel, one_row_per_entry=one_row_per_entry),
        grid=(n // PACK_ROWS,),
        in_specs=[pl.BlockSpec((PACK_ROWS, D_MODEL), lambda i: (i, 0))],
        out_specs=pl.BlockSpec(out_block, lambda i: (i, 0)),
        out_shape=jax.ShapeDtypeStruct(out_shape, I32),
        compiler_params=pltpu.CompilerParams(dimension_semantics=("arbitrary",)),
        name="pack_table",
    )(tab)


def _load_table(tab_hbm, tab_ref, sem):
    @pl.when(pl.program_id(0) == 0)
    def _():
        copy = pltpu.make_async_copy(tab_hbm, tab_ref, sem)
        copy.start()
        copy.wait()


def _pair_offsets(idx_ref, w, j):
    word = idx_ref[w * TOK_BLOCK + j]
    lo = pl.multiple_of(word & 0xFFFF, ROW_WORDS)
    hi = pl.multiple_of(lax.shift_right_logical(word, 16), ROW_WORDS)
    return ((w, lo), (w + N_PAIRS // 2, hi))


def _gather_rows(idx_ref, tab_ref, g_ref, j):
    for w in range(N_PAIRS // 2):
        for k, off in _pair_offsets(idx_ref, w, j):
            g_ref[pl.ds(k * ROW_WORDS, ROW_WORDS), :] = tab_ref[pl.ds(off, ROW_WORDS), :]


def _two_token_step(consume, gather, g0_ref, g1_ref, j):
    consume(g0_ref, j)
    consume(g1_ref, j + 1)
    gather(g0_ref, jnp.minimum(j + 2, TOK_BLOCK - 1))
    gather(g1_ref, jnp.minimum(j + 3, TOK_BLOCK - 1))


def _expert_u_kernel(idx_ref, tab_hbm, h2_ref, gate_ref, gsum_ref, sc_ref,
                     w_ref, idx_next_ref, gate_next_ref,
                     tab_ref, tab_sem, g0_ref, g1_ref, xs_ref, araw_ref, exp_ref):
    TB = TOK_BLOCK
    nchunk = D_MODEL // LANES
    group = 2 * LANES // nchunk
    pitch = ARAW_PITCH
    _load_table(tab_hbm, tab_ref, tab_sem)

    for c in range(nchunk):
        xs_ref[pl.ds(c, TB, stride=nchunk), :] = h2_ref[:, c * LANES:(c + 1) * LANES].astype(F32)

    def dots(g_ref, j):
        xj = xs_ref[pl.ds(pl.multiple_of(j * nchunk, nchunk), nchunk), :].astype(BF16)
        for g in range(N_PAIRS // group):
            wgt = pltpu.bitcast(g_ref[pl.ds(g * LANES, LANES), :], BF16)
            r = lax.dot_general(xj, wgt, NT_DIMS, preferred_element_type=F32)
            for half in range(2):
                cc = 2 * g + half
                araw_ref[pl.ds(cc * nchunk * pitch + j, nchunk, stride=pitch), :] = r[:, half * LANES:(half + 1) * LANES]

    gather = functools.partial(_gather_rows, idx_ref, tab_ref)
    tokens_per_head = TB // PEER_HEADS
    gather(g0_ref, 0)
    gather(g1_ref, 1)

    def head_step(hd, c):
        _topk_head(hd, sc_ref, gate_next_ref, exp_ref)
        for s in range(tokens_per_head // 2):
            _two_token_step(dots, gather, g0_ref, g1_ref, hd * tokens_per_head + 2 * s)
        return c

    lax.fori_loop(0, PEER_HEADS, head_step, 0)
    idx_next_ref[0] = _pack_offsets(exp_ref)

    lane_chunk = lax.broadcasted_iota(I32, (TB, LANES), 1) % nchunk
    a = jnp.zeros((TB, N_PAIRS), F32)
    for cc in range(nchunk):
        part = jnp.zeros((TB, LANES), F32)
        for c in range(nchunk):
            part = part + jnp.where(lane_chunk == c, araw_ref[pl.ds((cc * nchunk + c) * pitch, TB), :], 0.0)
        a = a + _dot_sel_right(part, gsum_ref[cc * LANES:(cc + 1) * LANES, :])
    w_ref[...] = gate_ref[0].T * (0.5 * a * (1.0 + lax.erf(a * (2.0 ** -0.5))))


def _pair_expand():
    pair_of_col = np.arange(D_MODEL) // SUBLANES
    return (np.arange(N_PAIRS)[:, None] == pair_of_col[None, :]).astype(np.float32)


def _expert_u_call(idx_words, tab, h2, gates, scores_next):
    nchunk = D_MODEL // LANES
    nblk = h2.shape[0] // TOK_BLOCK
    return pl.pallas_call(
        _expert_u_kernel,
        grid=(nblk,),
        in_specs=[
            pl.BlockSpec((N_PAIRS // 2 * TOK_BLOCK,), lambda i: (i,), memory_space=pltpu.SMEM),
            pl.BlockSpec(memory_space=pl.ANY),
            pl.BlockSpec((TOK_BLOCK, D_MODEL), lambda i: (i, 0)),
            pl.BlockSpec((1, N_PAIRS, TOK_BLOCK), lambda i: (i, 0, 0)),
            pl.BlockSpec((D_MODEL, N_PAIRS), lambda i: (0, 0)),
            pl.BlockSpec((2 * PEER_HEADS, N_KEYS, TOK_BLOCK), lambda i: (0, 0, i)),
        ],
        out_specs=[
            pl.BlockSpec((TOK_BLOCK, N_PAIRS), lambda i: (i, 0)),
            pl.BlockSpec((1, N_PAIRS // 2, TOK_BLOCK), lambda i: (i, 0, 0)),
            pl.BlockSpec((1, N_PAIRS, TOK_BLOCK), lambda i: (i, 0, 0)),
        ],
        out_shape=[
            jax.ShapeDtypeStruct((nblk * TOK_BLOCK, N_PAIRS), F32),
            jax.ShapeDtypeStruct((nblk, N_PAIRS // 2, TOK_BLOCK), I32),
            jax.ShapeDtypeStruct((nblk, N_PAIRS, TOK_BLOCK), F32),
        ],
        scratch_shapes=[
            pltpu.VMEM(tab.shape, I32),
            pltpu.SemaphoreType.DMA,
            pltpu.VMEM((N_PAIRS * ROW_WORDS, LANES), I32),
            pltpu.VMEM((N_PAIRS * ROW_WORDS, LANES), I32),
            pltpu.VMEM((TOK_BLOCK * nchunk, LANES), F32),
            pltpu.VMEM((nchunk * nchunk * ARAW_PITCH, LANES), F32),
            pltpu.VMEM((N_PAIRS, TOK_BLOCK), I32),
        ],
        compiler_params=pltpu.CompilerParams(
            dimension_semantics=("arbitrary",), vmem_limit_bytes=VMEM_LIMIT_EXPERT),
        name="expert_u",
    )(idx_words, tab, h2, gates, jnp.asarray(_pair_expand().T, BF16), scores_next)


def _sc_weighted_rows_call(table_words, ids, w):
    ts = ids.shape[0]
    row_len = table_words.shape[1]
    workers = SC_CORES * SC_SUBCORES
    tpw = ts // workers
    nbatch = N_PAIRS // SC_ROWS
    groups = LANES // SC_LANES
    mesh = plsc.VectorSubcoreMesh(core_axis_name="c", subcore_axis_name="s")
    params = pltpu.CompilerParams()
    if "needs_layout_passes" in pltpu.CompilerParams.__dataclass_fields__:
        params = dataclasses.replace(params, needs_layout_passes=False)

    @functools.partial(
        pl.kernel, mesh=mesh, compiler_params=params,
        out_type=jax.ShapeDtypeStruct((ts, D_MODEL), F32),
        scratch_types=[
            pltpu.VMEM((N_PAIRS,), I32), pltpu.VMEM((N_PAIRS,), I32),
            pltpu.VMEM((N_PAIRS,), F32), pltpu.VMEM((N_PAIRS,), F32),
            pltpu.VMEM((SC_ROWS, row_len), I32), pltpu.VMEM((SC_ROWS, row_len), I32),
            pltpu.VMEM((D_MODEL,), F32),
            pltpu.SemaphoreType.DMA, pltpu.SemaphoreType.DMA, pltpu.SemaphoreType.DMA, pltpu.SemaphoreType.DMA,
        ],
    )
    def sc_kernel(table_hbm, ids_hbm, w_hbm, out_hbm, ids0, ids1, w0, w1, rows0, rows1, y_v,
                  sem0, sem1, sem_ids, sem_w):
        worker = lax.axis_index("s") * SC_CORES + lax.axis_index("c")
        base = worker * tpw
        ids_v, w_v, rows_v, sems = (ids0, ids1), (w0, w1), (rows0, rows1), (sem0, sem1)

        def gather(slot, b, buf):
            src = table_hbm.at[ids_v[slot].at[pl.ds(b * SC_ROWS, SC_ROWS)]]
            return pltpu.make_async_copy(src, rows_v[buf], sems[buf])

        def fetch(slot, tok):
            return (pltpu.make_async_copy(ids_hbm.at[tok], ids_v[slot], sem_ids),
                    pltpu.make_async_copy(w_hbm.at[tok], w_v[slot], sem_w))

        def accumulate(slot, b, buf):
            wvs = [w_v[slot][pl.ds(b * SC_ROWS + q * SC_LANES, SC_LANES)] for q in range(SC_ROWS // SC_LANES)]
            wrs = [jnp.broadcast_to(wvs[r // SC_LANES][r % SC_LANES], (SC_LANES,)) for r in range(SC_ROWS)]

            def group(g, c):
                off = pl.multiple_of(g * SC_LANES, SC_LANES)
                f_lo = pl.multiple_of((2 * (g // groups)) * LANES + (g % groups) * SC_LANES, SC_LANES)
                f_hi = pl.multiple_of(f_lo + LANES, SC_LANES)
                if b == 0:
                    acc_lo = jnp.zeros((SC_LANES,), F32)
                    acc_hi = jnp.zeros((SC_LANES,), F32)
                else:
                    acc_lo = y_v[pl.ds(f_lo, SC_LANES)]
                    acc_hi = y_v[pl.ds(f_hi, SC_LANES)]
                for r in range(SC_ROWS):
                    word = rows_v[buf][r, pl.ds(off, SC_LANES)]
                    acc_lo = acc_lo + wrs[r] * lax.bitcast_convert_type(word << 16, F32)
                    acc_hi = acc_hi + wrs[r] * lax.bitcast_convert_type(word & jnp.int32(-65536), F32)
                y_v[pl.ds(f_lo, SC_LANES)] = acc_lo
                y_v[pl.ds(f_hi, SC_LANES)] = acc_hi
                return c

            lax.fori_loop(0, row_len // SC_LANES, group, 0)

        for copy in fetch(0, base):
            copy.start()
        for copy in fetch(0, base):
            copy.wait()
        gather(0, 0, 0).start()

        def token_pair(p, c):
            t0 = base + 2 * p
            t_next = base + jnp.minimum(2 * p + 2, tpw - 1)
            for q in range(2 * nbatch):
                slot, b, buf = q // nbatch, q % nbatch, q % 2
                if q == 0:
                    for copy in fetch(1, t0 + 1):
                        copy.start()
                if q == nbatch:
                    for copy in fetch(0, t_next):
                        copy.start()
                if q == nbatch - 1:
                    for copy in fetch(1, t0 + 1):
                        copy.wait()
                if q == 2 * nbatch - 1:
                    for copy in fetch(0, t_next):
                        copy.wait()
                nq = (q + 1) % (2 * nbatch)
                gather(nq // nbatch, nq % nbatch, nq % 2).start()
                gather(slot, b, buf).wait()
                accumulate(slot, b, buf)
                if b == nbatch - 1:
                    pltpu.sync_copy(y_v, out_hbm.at[t0 + slot])
            return c

        lax.fori_loop(0, tpw // 2, token_pair, 0)
        gather(0, 0, 0).wait()

    return sc_kernel(table_words, ids, w)


def _residual_norm_kernel(*refs):
    x1_ref, y_ref, nf_ref, out_ref = refs[-4:]
    out_ref[...] = _rms(x1_ref[...] + y_ref[...], nf_ref[...])


def _residual_norm_call(out_prev, x1, y, norm_f, blk0, total_tokens):
    nblk = y.shape[0] // NORM_BLOCK
    here = pl.BlockSpec((NORM_BLOCK, D_MODEL), lambda i: (i + blk0, 0))
    local = pl.BlockSpec((NORM_BLOCK, D_MODEL), lambda i: (i, 0))
    in_specs = [local, local, pl.BlockSpec((1, D_MODEL), lambda i: (0, 0))]
    args = (x1, y, norm_f.reshape(1, D_MODEL))
    aliases = {}
    if out_prev is not None:
        in_specs = [pl.BlockSpec(memory_space=pl.ANY)] + in_specs
        args = (out_prev,) + args
        aliases = {0: 0}
    return pl.pallas_call(
        _residual_norm_kernel,
        grid=(nblk,),
        in_specs=in_specs,
        out_specs=here,
        out_shape=jax.ShapeDtypeStruct((total_tokens, D_MODEL), F32),
        input_output_aliases=aliases,
        compiler_params=pltpu.CompilerParams(dimension_semantics=("arbitrary",)),
        name="residual_norm",
    )(*args)


def _expert_ids(idx_words):
    lo = (idx_words & 0xFFFF) // ROW_WORDS
    hi = lax.shift_right_logical(idx_words, 16) // ROW_WORDS
    ids = jnp.concatenate([lo, hi], axis=1)
    return ids.transpose(0, 2, 1).reshape(-1, N_PAIRS)


def kernel(x, norm_mix, w_in, conv_w, conv_gain, hg_lb_logits, hg_gain, w_out, norm_ffn, peer_wq, peer_keys,
           peer_u, peer_v, norm_f):
    B, S, D = x.shape
    depth = norm_mix.shape[0]
    assert depth == 1 and D == D_MODEL and S % SEQ_BLOCK == 0 and B % CHUNK_ROWS == 0
    tc = CHUNK_ROWS * S
    assert tc % (TAIL_SPLIT * NORM_BLOCK) == 0
    T = B * S
    nchunks = B // CHUNK_ROWS
    params = _mixer_params(norm_mix[0], w_in[0], conv_w[0], conv_gain[0], hg_lb_logits, hg_gain[0],
                           w_out[0], norm_ffn[0], peer_wq[0], peer_keys[0])
    tab_u = _pack_table(peer_u[0], one_row_per_entry=False)
    tab_v = _pack_table(peer_v[0], one_row_per_entry=True)
    mixed = [_mixer_call(x, params, 0, CHUNK_ROWS)]
    idx_k, gates_k = None, None
    out = None
    for k in range(nchunks):
        if k + 1 < nchunks:
            mixed.append(_mixer_call(x, params, (k + 1) * CHUNK_ROWS, CHUNK_ROWS))
        x1_k, h2_k, scores_k = mixed[k]
        x1_k, h2_k = x1_k.reshape(tc, D), h2_k.reshape(tc, D)
        if k == 0:
            idx_k, gates_k = _topk_call(scores_k)
        ids_k = _expert_ids(idx_k)
        if k + 1 < nchunks:
            w_k, idx_next, gates_next = _expert_u_call(idx_k.reshape(-1), tab_u, h2_k, gates_k, mixed[k + 1][2])
            y_k = _sc_weighted_rows_call(tab_v, ids_k, w_k)
            out = _residual_norm_call(out, x1_k, y_k, norm_f, k * tc // NORM_BLOCK, T)
            idx_k, gates_k = idx_next, gates_next
        else:
            tp = tc // TAIL_SPLIT
            bp = tp // TOK_BLOCK
            for p in range(TAIL_SPLIT):
                tok = slice(p * tp, (p + 1) * tp)
                blk = slice(p * bp, (p + 1) * bp)
                w_p, _, _ = _expert_u_call(idx_k[blk].reshape(-1), tab_u, h2_k[tok], gates_k[blk],
                                           scores_k[:, :, tok])
                y_p = _sc_weighted_rows_call(tab_v, ids_k[tok], w_p)
                out = _residual_norm_call(out, x1_k[tok], y_p, norm_f, (k * tc + p * tp) // NORM_BLOCK, T)
    return out.reshape(B, S, D)
```

```python
import dataclasses
import functools

import numpy as np
import jax
import jax.numpy as jnp
from jax import lax
from jax.experimental import pallas as pl
from jax.experimental.pallas import tpu as pltpu
from jax.experimental.pallas import tpu_sc as plsc

F32 = jnp.float32
BF16 = jnp.bfloat16
I32 = jnp.int32

D_MODEL = 1024
D_CONV = 512
CONV_GROUP = 64
CONV_WIDTH = 3
HG_HEADS = 4
HG_KD = 128
HG_VD = 128
D_HG = HG_HEADS * HG_KD
D_IN = 3 * D_CONV + 4 * D_HG
CHUNK = 32
PEER_HEADS = 8
N_KEYS = 128
HALF_Q = 128
PEER_TOPK = 16
N_PAIRS = PEER_HEADS * PEER_TOPK
EPS = 1e-6

LANES = 128
SUBLANES = 8
SEQ_BLOCK = 256
TOK_BLOCK = 128
ROW_WORDS = D_MODEL // (2 * LANES)
ARAW_PITCH = TOK_BLOCK + 1
N_CHUNKS = 32
PACK_ROWS = 512
NORM_BLOCK = 512
SC_CORES = 2
SC_SUBCORES = 16
SC_LANES = 16
SC_ROWS = 32
VMEM_LIMIT_MIXER = 48 * 1024 * 1024
VMEM_LIMIT_EXPERT = 52 * 1024 * 1024

NT_DIMS = (((1,), (1,)), ((), ()))


def _bdot(a, b):
    return jnp.dot(a.astype(BF16), b.astype(BF16), preferred_element_type=F32)


def _bdot_nt(a, b):
    return lax.dot_general(a.astype(BF16), b.astype(BF16), NT_DIMS, preferred_element_type=F32)


def _split(a, pieces):
    out = []
    for _ in range(pieces - 1):
        p = a.astype(BF16)
        out.append(p)
        a = a - p.astype(F32)
    return out + [a.astype(BF16)]


def _dot_sel(sel, a, pieces=3):
    out = None
    for piece in _split(a, pieces):
        d = jnp.dot(sel, piece, preferred_element_type=F32)
        out = d if out is None else out + d
    return out


def _dot_sel_right(a, sel, pieces=3):
    out = None
    for piece in _split(a, pieces):
        d = jnp.dot(piece, sel, preferred_element_type=F32)
        out = d if out is None else out + d
    return out


def _silu(x):
    return x * (1.0 / (1.0 + jnp.exp(-x)))


def _sigmoid(x):
    return 1.0 / (1.0 + jnp.exp(-x))


def _rms(x, gain):
    return x * lax.rsqrt(jnp.mean(x * x, axis=-1, keepdims=True) + EPS) * gain


def _mixer_kernel(x_ref, nmix_ref, win_ref, convw_ref, cgain_ref, lbl_ref, hgain_ref, wout_ref,
                  nffn_ref, wq_ref, keys_ref, cum_ref, gmean_ref, hmean_ref,
                  x1_ref, h2_ref, sc_ref,
                  up_ref, st_ref):
    L = SEQ_BLOCK
    sb = pl.program_id(1)

    @pl.when(sb == 0)
    def _():
        up_ref[0:SUBLANES, :] = jnp.zeros((SUBLANES, D_CONV), F32)
        st_ref[...] = jnp.zeros(st_ref.shape, F32)

    x = x_ref[0]
    h = _rms(x, nmix_ref[...])
    proj = _bdot(h, win_ref[...])

    u = proj[:, 0:D_CONV] * proj[:, D_CONV:2 * D_CONV]
    up_ref[SUBLANES:SUBLANES + L, :] = u
    u1 = up_ref[SUBLANES - 1:SUBLANES - 1 + L, :]
    u2 = up_ref[SUBLANES - 2:SUBLANES - 2 + L, :]
    cw = convw_ref[...]
    y = cw[0:1, :] * u2 + cw[1:2, :] * u1 + cw[2:3, :] * u
    y = proj[:, 2 * D_CONV:3 * D_CONV] * y
    up_ref[0:SUBLANES, :] = up_ref[L:L + SUBLANES, :]
    gms = _dot_sel_right(y * y, gmean_ref[...], pieces=2)
    conv_out = y * lax.rsqrt(gms + EPS) * cgain_ref[...]

    o0 = 3 * D_CONV
    qp = proj[:, o0:o0 + D_HG]
    fp = proj[:, o0 + D_HG:o0 + 2 * D_HG]
    vv = proj[:, o0 + 2 * D_HG:o0 + 3 * D_HG]
    gg = proj[:, o0 + 3 * D_HG:o0 + 4 * D_HG]
    lbl = lbl_ref[...]
    lbe = jnp.exp(lbl - jnp.max(lbl, axis=0, keepdims=True))
    lb = lbe[0:1, :] / jnp.sum(lbe, axis=0, keepdims=True)
    qh = _silu(qp) * (HG_KD ** -0.5)
    f = lb + (1.0 - lb) * _sigmoid(fp)
    kh = 1.0 - f
    logf = jnp.log(f)
    cums = _dot_sel(cum_ref[...], logf)
    gcum = cums[0:L, :]
    glast = cums[L:2 * L, :]
    q_dec = qh * jnp.exp(gcum)
    k_dec = kh * jnp.exp(-gcum)
    k_end = kh * jnp.exp(glast - gcum)
    decay = jnp.exp(glast)

    row = lax.broadcasted_iota(I32, (L, L), 0)
    col = lax.broadcasted_iota(I32, (L, L), 1)
    causal = (row // CHUNK == col // CHUNK) & (col <= row)

    o_heads = []
    for hh in range(HG_HEADS):
        cs = slice(hh * HG_KD, (hh + 1) * HG_KD)
        qd, kd, ke, vh, dc = q_dec[:, cs], k_dec[:, cs], k_end[:, cs], vv[:, cs], decay[:, cs]
        a = _bdot_nt(qd, kd)
        a = jnp.where(causal, a, 0.0)
        o_intra = _bdot(a, vh)
        st = st_ref[hh]
        pieces = []
        for n in range(L // CHUNK):
            rs = slice(n * CHUNK, (n + 1) * CHUNK)
            pieces.append(_bdot_nt(qd[rs], st))
            st = dc[n * CHUNK:n * CHUNK + 1, :] * st + _bdot(vh[rs].T, ke[rs])
        st_ref[hh] = st
        o_heads.append(o_intra + jnp.concatenate(pieces, axis=0))
    o = jnp.concatenate(o_heads, axis=1)
    oms = _dot_sel_right(o * o, hmean_ref[...], pieces=2)
    o = o * lax.rsqrt(oms + EPS) * hgain_ref[...]
    o = o * _silu(gg)

    mix = jnp.concatenate([conv_out, o], axis=1)
    x1 = x + _bdot(mix, wout_ref[...])
    x1_ref[0] = x1
    h2 = _rms(x1, nffn_ref[...])
    h2_ref[0] = h2.astype(BF16)
    q2 = _bdot(h2, wq_ref[...])
    for hp in range(2 * PEER_HEADS):
        sc_ref[hp] = _bdot_nt(keys_ref[hp], q2[:, hp * HALF_Q:(hp + 1) * HALF_Q])


def _mixer_params(norm_mix, w_in, conv_w, conv_gain, lb_logits, hg_gain, w_out, norm_ffn, wq, keys):
    L = SEQ_BLOCK
    r = np.arange(L)
    same = (r[:, None] // CHUNK) == (r[None, :] // CHUNK)
    cum = np.concatenate([same & (r[None, :] <= r[:, None]), same], axis=0).astype(np.float32)
    c = np.arange(D_CONV)
    gmean = ((c[:, None] // CONV_GROUP) == (c[None, :] // CONV_GROUP)).astype(np.float32) / CONV_GROUP
    hmean = ((c[:, None] // HG_VD) == (c[None, :] // HG_VD)).astype(np.float32) / HG_VD
    return (norm_mix.reshape(1, D_MODEL), w_in.astype(BF16), conv_w, conv_gain.reshape(1, D_CONV), lb_logits,
            jnp.tile(hg_gain.reshape(1, HG_VD), (1, HG_HEADS)), w_out.astype(BF16), norm_ffn.reshape(1, D_MODEL),
            wq.astype(BF16), keys.reshape(2 * PEER_HEADS, N_KEYS, HALF_Q).astype(BF16),
            jnp.asarray(cum, BF16), jnp.asarray(gmean, BF16), jnp.asarray(hmean, BF16))


def _mixer_call(x, params, b0, nbatch):
    S = x.shape[1]
    L = SEQ_BLOCK
    nsb = S // L
    const = lambda a: pl.BlockSpec(a.shape, lambda b, s: (0,) * a.ndim)
    return pl.pallas_call(
        _mixer_kernel,
        grid=(nbatch, nsb),
        in_specs=[pl.BlockSpec((1, L, D_MODEL), lambda b, s: (b + b0, s, 0))] + [const(a) for a in params],
        out_specs=[
            pl.BlockSpec((1, L, D_MODEL), lambda b, s: (b, s, 0)),
            pl.BlockSpec((1, L, D_MODEL), lambda b, s: (b, s, 0)),
            pl.BlockSpec((2 * PEER_HEADS, N_KEYS, L), lambda b, s: (0, 0, b * nsb + s)),
        ],
        out_shape=[
            jax.ShapeDtypeStruct((nbatch, S, D_MODEL), F32),
            jax.ShapeDtypeStruct((nbatch, S, D_MODEL), BF16),
            jax.ShapeDtypeStruct((2 * PEER_HEADS, N_KEYS, nbatch * S), F32),
        ],
        scratch_shapes=[
            pltpu.VMEM((L + 2 * SUBLANES, D_CONV), F32),
            pltpu.VMEM((HG_HEADS, HG_VD, HG_KD), F32),
        ],
        compiler_params=pltpu.CompilerParams(
            dimension_semantics=("arbitrary", "arbitrary"), vmem_limit_bytes=VMEM_LIMIT_MIXER),
        name="mixer",
    )(x, *params)


def _extract_topk(vals, ids, n, sentinel):
    tops, sel = [], []
    for _ in range(n):
        m = jnp.max(vals, axis=0, keepdims=True)
        i = jnp.min(jnp.where(vals == m, ids, sentinel), axis=0, keepdims=True)
        tops.append(m)
        sel.append(i)
        vals = jnp.where(ids == i, -jnp.inf, vals)
    return jnp.concatenate(tops, axis=0), jnp.concatenate(sel, axis=0)


def _topk_head(hd, sc_ref, gate_ref, exp_ref):
    TB = TOK_BLOCK
    K = PEER_TOPK
    iota_keys = lax.broadcasted_iota(I32, (N_KEYS, TB), 0).astype(F32)
    sub = lax.broadcasted_iota(I32, (SUBLANES, TB), 0)
    subf = sub.astype(F32)
    s1, i1t = _extract_topk(sc_ref[2 * hd], iota_keys, K, N_KEYS)
    s2, i2t = _extract_topk(sc_ref[2 * hd + 1], iota_keys, K, N_KEYS)
    vals, ids = [], []
    for b0 in range(0, K, SUBLANES):
        vals.append(s1[0:1, :] + s2[b0:b0 + SUBLANES, :])
        ids.append(subf + b0)
    for a in range(1, SUBLANES):
        v = s1[a:a + 1, :] + s2[0:SUBLANES, :]
        vals.append(jnp.where(sub < K // (a + 1), v, -jnp.inf))
        ids.append(subf + a * K)
    vals.append(s1[SUBLANES:K, :] + s2[0:1, :])
    ids.append((subf + SUBLANES) * K)
    c_score, c_idx = _extract_topk(jnp.concatenate(vals, axis=0), jnp.concatenate(ids, axis=0), K, K * K)
    c_idx = c_idx.astype(I32)
    i1t = i1t.astype(I32)
    i2t = i2t.astype(I32)
    a_sel = lax.shift_right_logical(c_idx, 4)
    b_sel = c_idx & (K - 1)
    i1 = jnp.zeros((K, TB), I32)
    i2 = jnp.zeros((K, TB), I32)
    for a in range(K):
        i1 = i1 + jnp.where(a_sel == a, i1t[a:a + 1, :], 0)
        i2 = i2 + jnp.where(b_sel == a, i2t[a:a + 1, :], 0)
    e = jnp.exp(c_score - jnp.max(c_score, axis=0, keepdims=True))
    row0 = pl.multiple_of(hd * K, K)
    gate_ref[0, pl.ds(row0, K), :] = e / jnp.sum(e, axis=0, keepdims=True)
    exp_ref[pl.ds(row0, K), :] = (i1 * N_KEYS + i2) * ROW_WORDS


def _pack_offsets(exp_ref):
    half = N_PAIRS // 2
    return exp_ref[0:half, :] | (exp_ref[half:N_PAIRS, :] << 16)


def _topk_kernel(sc_ref, idx_ref, gate_ref, exp_ref):
    def head(hd, c):
        _topk_head(hd, sc_ref, gate_ref, exp_ref)
        return c

    lax.fori_loop(0, PEER_HEADS, head, 0)
    idx_ref[0] = _pack_offsets(exp_ref)


def _topk_call(scores_t):
    nblk = scores_t.shape[-1] // TOK_BLOCK
    return pl.pallas_call(
        _topk_kernel,
        grid=(nblk,),
        in_specs=[pl.BlockSpec((2 * PEER_HEADS, N_KEYS, TOK_BLOCK), lambda i: (0, 0, i))],
        out_specs=[
            pl.BlockSpec((1, N_PAIRS // 2, TOK_BLOCK), lambda i: (i, 0, 0)),
            pl.BlockSpec((1, N_PAIRS, TOK_BLOCK), lambda i: (i, 0, 0)),
        ],
        out_shape=[
            jax.ShapeDtypeStruct((nblk, N_PAIRS // 2, TOK_BLOCK), I32),
            jax.ShapeDtypeStruct((nblk, N_PAIRS, TOK_BLOCK), F32),
        ],
        scratch_shapes=[pltpu.VMEM((N_PAIRS, TOK_BLOCK), I32)],
        compiler_params=pltpu.CompilerParams(dimension_semantics=("arbitrary",)),
        name="topk",
    )(scores_t)


def _pack_kernel(tab_ref, out_ref, *, one_row_per_entry):
    rows = tab_ref.shape[0]
    bits = lambda v: lax.bitcast_convert_type(v.astype(BF16).astype(F32), jnp.uint32)
    for i in range(ROW_WORDS):
        lo = bits(tab_ref[:, (2 * i) * LANES:(2 * i + 1) * LANES]) >> 16
        hi = bits(tab_ref[:, (2 * i + 1) * LANES:(2 * i + 2) * LANES]) & jnp.uint32(0xFFFF0000)
        words = lax.bitcast_convert_type(lo | hi, I32)
        if one_row_per_entry:
            out_ref[:, i * LANES:(i + 1) * LANES] = words
        else:
            out_ref[pl.ds(i, rows, stride=ROW_WORDS), :] = words


def _pack_table(tab, one_row_per_entry):
    n = tab.shape[0]
    if one_row_per_entry:
        out_block, out_shape = (PACK_ROWS, ROW_WORDS * LANES), (n, ROW_WORDS * LANES)
    else:
        out_block, out_shape = (PACK_ROWS * ROW_WORDS, LANES), (n * ROW_WORDS, LANES)
    return pl.pallas_call(
        functools.partial(_pack_kernel, one_row_per_entry=one_row_per_entry),
        grid=(n // PACK_ROWS,),
        in_specs=[pl.BlockSpec((PACK_ROWS, D_MODEL), lambda i: (i, 0))],
        out_specs=pl.BlockSpec(out_block, lambda i: (i, 0)),
        out_shape=jax.ShapeDtypeStruct(out_shape, I32),
        compiler_params=pltpu.CompilerParams(dimension_semantics=("arbitrary",)),
        name="pack_table",
    )(tab)


def _load_table(tab_hbm, tab_ref, sem):
    @pl.when(pl.program_id(0) == 0)
    def _():
        copy = pltpu.make_async_copy(tab_hbm, tab_ref, sem)
        copy.start()
        copy.wait()


def _pair_offsets(idx_ref, w, j):
    word = idx_ref[w * TOK_BLOCK + j]
    lo = pl.multiple_of(word & 0xFFFF, ROW_WORDS)
    hi = pl.multiple_of(lax.shift_right_logical(word, 16), ROW_WORDS)
    return ((w, lo), (w + N_PAIRS // 2, hi))


def _gather_rows(idx_ref, tab_ref, g_ref, j):
    for w in range(N_PAIRS // 2):
        for k, off in _pair_offsets(idx_ref, w, j):
            g_ref[pl.ds(k * ROW_WORDS, ROW_WORDS), :] = tab_ref[pl.ds(off, ROW_WORDS), :]


def _two_token_step(consume, gather, g0_ref, g1_ref, j):
    consume(g0_ref, j)
    consume(g1_ref, j + 1)
    gather(g0_ref, jnp.minimum(j + 2, TOK_BLOCK - 1))
    gather(g1_ref, jnp.minimum(j + 3, TOK_BLOCK - 1))


def _expert_u_kernel(idx_ref, tab_hbm, h2_ref, gate_ref, gsum_ref, sc_ref,
                     w_ref, idx_next_ref, gate_next_ref,
                     tab_ref, tab_sem, g0_ref, g1_ref, xs_ref, araw_ref, exp_ref):
    TB = TOK_BLOCK
    nchunk = D_MODEL // LANES
    group = 2 * LANES // nchunk
    pitch = ARAW_PITCH
    _load_table(tab_hbm, tab_ref, tab_sem)

    for c in range(nchunk):
        xs_ref[pl.ds(c, TB, stride=nchunk), :] = h2_ref[:, c * LANES:(c + 1) * LANES].astype(F32)

    def dots(g_ref, j):
        xj = xs_ref[pl.ds(pl.multiple_of(j * nchunk, nchunk), nchunk), :].astype(BF16)
        for g in range(N_PAIRS // group):
            wgt = pltpu.bitcast(g_ref[pl.ds(g * LANES, LANES), :], BF16)
            r = lax.dot_general(xj, wgt, NT_DIMS, preferred_element_type=F32)
            for half in range(2):
                cc = 2 * g + half
                araw_ref[pl.ds(cc * nchunk * pitch + j, nchunk, stride=pitch), :] = r[:, half * LANES:(half + 1) * LANES]

    gather = functools.partial(_gather_rows, idx_ref, tab_ref)
    tokens_per_head = TB // PEER_HEADS
    gather(g0_ref, 0)
    gather(g1_ref, 1)

    def head_step(hd, c):
        _topk_head(hd, sc_ref, gate_next_ref, exp_ref)
        for s in range(tokens_per_head // 2):
            _two_token_step(dots, gather, g0_ref, g1_ref, hd * tokens_per_head + 2 * s)
        return c

    lax.fori_loop(0, PEER_HEADS, head_step, 0)
    idx_next_ref[0] = _pack_offsets(exp_ref)

    lane_chunk = lax.broadcasted_iota(I32, (TB, LANES), 1) % nchunk
    a = jnp.zeros((TB, N_PAIRS), F32)
    for cc in range(nchunk):
        part = jnp.zeros((TB, LANES), F32)
        for c in range(nchunk):
            part = part + jnp.where(lane_chunk == c, araw_ref[pl.ds((cc * nchunk + c) * pitch, TB), :], 0.0)
        a = a + _dot_sel_right(part, gsum_ref[cc * LANES:(cc + 1) * LANES, :])
    w_ref[...] = gate_ref[0].T * (0.5 * a * (1.0 + lax.erf(a * (2.0 ** -0.5))))


def _pair_expand():
    pair_of_col = np.arange(D_MODEL) // SUBLANES
    return (np.arange(N_PAIRS)[:, None] == pair_of_col[None, :]).astype(np.float32)


def _expert_u_call(idx_words, tab, h2, gates, scores_next):
    nchunk = D_MODEL // LANES
    nblk = h2.shape[0] // TOK_BLOCK
    return pl.pallas_call(
        _expert_u_kernel,
        grid=(nblk,),
        in_specs=[
            pl.BlockSpec((N_PAIRS // 2 * TOK_BLOCK,), lambda i: (i,), memory_space=pltpu.SMEM),
            pl.BlockSpec(memory_space=pl.ANY),
            pl.BlockSpec((TOK_BLOCK, D_MODEL), lambda i: (i, 0)),
            pl.BlockSpec((1, N_PAIRS, TOK_BLOCK), lambda i: (i, 0, 0)),
            pl.BlockSpec((D_MODEL, N_PAIRS), lambda i: (0, 0)),
            pl.BlockSpec((2 * PEER_HEADS, N_KEYS, TOK_BLOCK), lambda i: (0, 0, i)),
        ],
        out_specs=[
            pl.BlockSpec((TOK_BLOCK, N_PAIRS), lambda i: (i, 0)),
            pl.BlockSpec((1, N_PAIRS // 2, TOK_BLOCK), lambda i: (i, 0, 0)),
            pl.BlockSpec((1, N_PAIRS, TOK_BLOCK), lambda i: (i, 0, 0)),
        ],
        out_shape=[
            jax.ShapeDtypeStruct((nblk * TOK_BLOCK, N_PAIRS), F32),
            jax.ShapeDtypeStruct((nblk, N_PAIRS // 2, TOK_BLOCK), I32),
            jax.ShapeDtypeStruct((nblk, N_PAIRS, TOK_BLOCK), F32),
        ],
        scratch_shapes=[
            pltpu.VMEM(tab.shape, I32),
            pltpu.SemaphoreType.DMA,
            pltpu.VMEM((N_PAIRS * ROW_WORDS, LANES), I32),
            pltpu.VMEM((N_PAIRS * ROW_WORDS, LANES), I32),
            pltpu.VMEM((TOK_BLOCK * nchunk, LANES), F32),
            pltpu.VMEM((nchunk * nchunk * ARAW_PITCH, LANES), F32),
            pltpu.VMEM((N_PAIRS, TOK_BLOCK), I32),
        ],
        compiler_params=pltpu.CompilerParams(
            dimension_semantics=("arbitrary",), vmem_limit_bytes=VMEM_LIMIT_EXPERT),
        name="expert_u",
    )(idx_words, tab, h2, gates, jnp.asarray(_pair_expand().T, BF16), scores_next)


def _sc_weighted_rows_call(table_words, ids, w):
    ts = ids.shape[0]
    row_len = table_words.shape[1]
    workers = SC_CORES * SC_SUBCORES
    tpw = ts // workers
    nbatch = N_PAIRS // SC_ROWS
    groups = LANES // SC_LANES
    mesh = plsc.VectorSubcoreMesh(core_axis_name="c", subcore_axis_name="s")
    params = pltpu.CompilerParams()
    if "needs_layout_passes" in pltpu.CompilerParams.__dataclass_fields__:
        params = dataclasses.replace(params, needs_layout_passes=False)

    @functools.partial(
        pl.kernel, mesh=mesh, compiler_params=params,
        out_type=jax.ShapeDtypeStruct((ts, D_MODEL), F32),
        scratch_types=[
            pltpu.VMEM((N_PAIRS,), I32), pltpu.VMEM((N_PAIRS,), I32),
            pltpu.VMEM((N_PAIRS,), F32), pltpu.VMEM((N_PAIRS,), F32),
            pltpu.VMEM((SC_ROWS, row_len), I32), pltpu.VMEM((SC_ROWS, row_len), I32),
            pltpu.VMEM((D_MODEL,), F32),
            pltpu.SemaphoreType.DMA, pltpu.SemaphoreType.DMA, pltpu.SemaphoreType.DMA, pltpu.SemaphoreType.DMA,
        ],
    )
    def sc_kernel(table_hbm, ids_hbm, w_hbm, out_hbm, ids0, ids1, w0, w1, rows0, rows1, y_v,
                  sem0, sem1, sem_ids, sem_w):
        worker = lax.axis_index("s") * SC_CORES + lax.axis_index("c")
        base = worker * tpw
        ids_v, w_v, rows_v, sems = (ids0, ids1), (w0, w1), (rows0, rows1), (sem0, sem1)

        def gather(slot, b, buf):
            src = table_hbm.at[ids_v[slot].at[pl.ds(b * SC_ROWS, SC_ROWS)]]
            return pltpu.make_async_copy(src, rows_v[buf], sems[buf])

        def fetch(slot, tok):
            return (pltpu.make_async_copy(ids_hbm.at[tok], ids_v[slot], sem_ids),
                    pltpu.make_async_copy(w_hbm.at[tok], w_v[slot], sem_w))

        def accumulate(slot, b, buf):
            wvs = [w_v[slot][pl.ds(b * SC_ROWS + q * SC_LANES, SC_LANES)] for q in range(SC_ROWS // SC_LANES)]
            wrs = [jnp.broadcast_to(wvs[r // SC_LANES][r % SC_LANES], (SC_LANES,)) for r in range(SC_ROWS)]

            def group(g, c):
                off = pl.multiple_of(g * SC_LANES, SC_LANES)
                f_lo = pl.multiple_of((2 * (g // groups)) * LANES + (g % groups) * SC_LANES, SC_LANES)
                f_hi = pl.multiple_of(f_lo + LANES, SC_LANES)
                if b == 0:
                    acc_lo = jnp.zeros((SC_LANES,), F32)
                    acc_hi = jnp.zeros((SC_LANES,), F32)
                else:
                    acc_lo = y_v[pl.ds(f_lo, SC_LANES)]
                    acc_hi = y_v[pl.ds(f_hi, SC_LANES)]
                for r in range(SC_ROWS):
                    word = rows_v[buf][r, pl.ds(off, SC_LANES)]
                    acc_lo = acc_lo + wrs[r] * lax.bitcast_convert_type(word << 16, F32)
                    acc_hi = acc_hi + wrs[r] * lax.bitcast_convert_type(word & jnp.int32(-65536), F32)
                y_v[pl.ds(f_lo, SC_LANES)] = acc_lo
                y_v[pl.ds(f_hi, SC_LANES)] = acc_hi
                return c

            lax.fori_loop(0, row_len // SC_LANES, group, 0)

        for copy in fetch(0, base):
            copy.start()
        for copy in fetch(0, base):
            copy.wait()
        gather(0, 0, 0).start()

        def token_pair(p, c):
            t0 = base + 2 * p
            t_next = base + jnp.minimum(2 * p + 2, tpw - 1)
            for q in range(2 * nbatch):
                slot, b, buf = q // nbatch, q % nbatch, q % 2
                if q == 0:
                    for copy in fetch(1, t0 + 1):
                        copy.start()
                if q == nbatch:
                    for copy in fetch(0, t_next):
                        copy.start()
                if q == nbatch - 1:
                    for copy in fetch(1, t0 + 1):
                        copy.wait()
                if q == 2 * nbatch - 1:
                    for copy in fetch(0, t_next):
                        copy.wait()
                nq = (q + 1) % (2 * nbatch)
                gather(nq // nbatch, nq % nbatch, nq % 2).start()
                gather(slot, b, buf).wait()
                accumulate(slot, b, buf)
                if b == nbatch - 1:
                    pltpu.sync_copy(y_v, out_hbm.at[t0 + slot])
            return c

        lax.fori_loop(0, tpw // 2, token_pair, 0)
        gather(0, 0, 0).wait()

    return sc_kernel(table_words, ids, w)


def _residual_norm_kernel(*refs):
    x1_ref, y_ref, nf_ref, out_ref = refs[-4:]
    out_ref[...] = _rms(x1_ref[...] + y_ref[...], nf_ref[...])


def _residual_norm_call(out_prev, x1, y, norm_f, chunk, total_tokens):
    nblk = y.shape[0] // NORM_BLOCK
    here = pl.BlockSpec((NORM_BLOCK, D_MODEL), lambda i: (i + chunk * nblk, 0))
    local = pl.BlockSpec((NORM_BLOCK, D_MODEL), lambda i: (i, 0))
    in_specs = [local, local, pl.BlockSpec((1, D_MODEL), lambda i: (0, 0))]
    args = (x1, y, norm_f.reshape(1, D_MODEL))
    aliases = {}
    if out_prev is not None:
        in_specs = [pl.BlockSpec(memory_space=pl.ANY)] + in_specs
        args = (out_prev,) + args
        aliases = {0: 0}
    return pl.pallas_call(
        _residual_norm_kernel,
        grid=(nblk,),
        in_specs=in_specs,
        out_specs=here,
        out_shape=jax.ShapeDtypeStruct((total_tokens, D_MODEL), F32),
        input_output_aliases=aliases,
        compiler_params=pltpu.CompilerParams(dimension_semantics=("arbitrary",)),
        name="residual_norm",
    )(*args)


def _expert_ids(idx_words):
    lo = (idx_words & 0xFFFF) // ROW_WORDS
    hi = lax.shift_right_logical(idx_words, 16) // ROW_WORDS
    ids = jnp.concatenate([lo, hi], axis=1)
    return ids.transpose(0, 2, 1).reshape(-1, N_PAIRS)


def kernel(x, norm_mix, w_in, conv_w, conv_gain, hg_lb_logits, hg_gain, w_out, norm_ffn, peer_wq, peer_keys,
           peer_u, peer_v, norm_f):
    B, S, D = x.shape
    depth = norm_mix.shape[0]
    assert depth == 1 and D == D_MODEL and S % SEQ_BLOCK == 0 and B % N_CHUNKS == 0 and S % TOK_BLOCK == 0
    T = B * S
    bc = B // N_CHUNKS
    nb = bc * S // TOK_BLOCK
    params = _mixer_params(norm_mix[0], w_in[0], conv_w[0], conv_gain[0], hg_lb_logits, hg_gain[0],
                           w_out[0], norm_ffn[0], peer_wq[0], peer_keys[0])
    tab_u = _pack_table(peer_u[0], one_row_per_entry=False)
    tab_v = _pack_table(peer_v[0], one_row_per_entry=True)
    mixed = [_mixer_call(x, params, 0, bc)]
    idx_k, gates_k = None, None
    out = None
    for k in range(N_CHUNKS):
        if k + 1 < N_CHUNKS:
            mixed.append(_mixer_call(x, params, (k + 1) * bc, bc))
        x1_k, h2_k, scores_k = mixed[k]
        if k == 0:
            idx_k, gates_k = _topk_call(scores_k)
        scores_next = mixed[min(k + 1, N_CHUNKS - 1)][2]
        ids_k = _expert_ids(idx_k)
        w_k, idx_k, gates_k = _expert_u_call(idx_k.reshape(-1), tab_u, h2_k.reshape(bc * S, D), gates_k,
                                             scores_next)
        y_k = _sc_weighted_rows_call(tab_v, ids_k, w_k)
        out = _residual_norm_call(out, x1_k.reshape(bc * S, D), y_k, norm_f, k, T)
    return out.reshape(B, S, D)
```

```python
import dataclasses
import functools

import numpy as np
import jax
import jax.numpy as jnp
from jax import lax
from jax.experimental import pallas as pl
from jax.experimental.pallas import tpu as pltpu
from jax.experimental.pallas import tpu_sc as plsc

F32 = jnp.float32
BF16 = jnp.bfloat16
I32 = jnp.int32

D_MODEL = 1024
D_CONV = 512
CONV_GROUP = 64
CONV_WIDTH = 3
HG_HEADS = 4
HG_KD = 128
HG_VD = 128
D_HG = HG_HEADS * HG_KD
D_IN = 3 * D_CONV + 4 * D_HG
CHUNK = 32
PEER_HEADS = 8
N_KEYS = 128
HALF_Q = 128
PEER_TOPK = 16
N_PAIRS = PEER_HEADS * PEER_TOPK
EPS = 1e-6

LANES = 128
SUBLANES = 8
SEQ_BLOCK = 256
TOK_BLOCK = 128
ROW_WORDS = D_MODEL // (2 * LANES)
ARAW_PITCH = TOK_BLOCK + 1
N_CHUNKS = 32
PACK_ROWS = 512
NORM_BLOCK = 512
SC_CORES = 2
SC_SUBCORES = 16
SC_LANES = 16
SC_ROWS = 32
VMEM_LIMIT_MIXER = 48 * 1024 * 1024
VMEM_LIMIT_EXPERT = 52 * 1024 * 1024

NT_DIMS = (((1,), (1,)), ((), ()))


def _bdot(a, b):
    return jnp.dot(a.astype(BF16), b.astype(BF16), preferred_element_type=F32)


def _bdot_nt(a, b):
    return lax.dot_general(a.astype(BF16), b.astype(BF16), NT_DIMS, preferred_element_type=F32)


def _split(a, pieces):
    out = []
    for _ in range(pieces - 1):
        p = a.astype(BF16)
        out.append(p)
        a = a - p.astype(F32)
    return out + [a.astype(BF16)]


def _dot_sel(sel, a, pieces=3):
    out = None
    for piece in _split(a, pieces):
        d = jnp.dot(sel, piece, preferred_element_type=F32)
        out = d if out is None else out + d
    return out


def _dot_sel_right(a, sel, pieces=3):
    out = None
    for piece in _split(a, pieces):
        d = jnp.dot(piece, sel, preferred_element_type=F32)
        out = d if out is None else out + d
    return out


def _silu(x):
    return x * (1.0 / (1.0 + jnp.exp(-x)))


def _sigmoid(x):
    return 1.0 / (1.0 + jnp.exp(-x))


def _rms(x, gain):
    return x * lax.rsqrt(jnp.mean(x * x, axis=-1, keepdims=True) + EPS) * gain


def _mixer_kernel(x_ref, nmix_ref, win_ref, convw_ref, cgain_ref, lbl_ref, hgain_ref, wout_ref,
                  nffn_ref, wq_ref, keys_ref, cum_ref, gmean_ref, hmean_ref,
                  x1_ref, h2_ref, sc_ref,
                  up_ref, st_ref):
    L = SEQ_BLOCK
    sb = pl.program_id(1)

    @pl.when(sb == 0)
    def _():
        up_ref[0:SUBLANES, :] = jnp.zeros((SUBLANES, D_CONV), F32)
        st_ref[...] = jnp.zeros(st_ref.shape, F32)

    x = x_ref[0]
    h = _rms(x, nmix_ref[...])
    proj = _bdot(h, win_ref[...])

    u = proj[:, 0:D_CONV] * proj[:, D_CONV:2 * D_CONV]
    up_ref[SUBLANES:SUBLANES + L, :] = u
    u1 = up_ref[SUBLANES - 1:SUBLANES - 1 + L, :]
    u2 = up_ref[SUBLANES - 2:SUBLANES - 2 + L, :]
    cw = convw_ref[...]
    y = cw[0:1, :] * u2 + cw[1:2, :] * u1 + cw[2:3, :] * u
    y = proj[:, 2 * D_CONV:3 * D_CONV] * y
    up_ref[0:SUBLANES, :] = up_ref[L:L + SUBLANES, :]
    gms = _dot_sel_right(y * y, gmean_ref[...], pieces=2)
    conv_out = y * lax.rsqrt(gms + EPS) * cgain_ref[...]

    o0 = 3 * D_CONV
    qp = proj[:, o0:o0 + D_HG]
    fp = proj[:, o0 + D_HG:o0 + 2 * D_HG]
    vv = proj[:, o0 + 2 * D_HG:o0 + 3 * D_HG]
    gg = proj[:, o0 + 3 * D_HG:o0 + 4 * D_HG]
    lbl = lbl_ref[...]
    lbe = jnp.exp(lbl - jnp.max(lbl, axis=0, keepdims=True))
    lb = lbe[0:1, :] / jnp.sum(lbe, axis=0, keepdims=True)
    qh = _silu(qp) * (HG_KD ** -0.5)
    f = lb + (1.0 - lb) * _sigmoid(fp)
    kh = 1.0 - f
    logf = jnp.log(f)
    cums = _dot_sel(cum_ref[...], logf)
    gcum = cums[0:L, :]
    glast = cums[L:2 * L, :]
    q_dec = qh * jnp.exp(gcum)
    k_dec = kh * jnp.exp(-gcum)
    k_end = kh * jnp.exp(glast - gcum)
    decay = jnp.exp(glast)

    row = lax.broadcasted_iota(I32, (L, L), 0)
    col = lax.broadcasted_iota(I32, (L, L), 1)
    causal = (row // CHUNK == col // CHUNK) & (col <= row)

    o_heads = []
    for hh in range(HG_HEADS):
        cs = slice(hh * HG_KD, (hh + 1) * HG_KD)
        qd, kd, ke, vh, dc = q_dec[:, cs], k_dec[:, cs], k_end[:, cs], vv[:, cs], decay[:, cs]
        a = _bdot_nt(qd, kd)
        a = jnp.where(causal, a, 0.0)
        o_intra = _bdot(a, vh)
        st = st_ref[hh]
        pieces = []
        for n in range(L // CHUNK):
            rs = slice(n * CHUNK, (n + 1) * CHUNK)
            pieces.append(_bdot_nt(qd[rs], st))
            st = dc[n * CHUNK:n * CHUNK + 1, :] * st + _bdot(vh[rs].T, ke[rs])
        st_ref[hh] = st
        o_heads.append(o_intra + jnp.concatenate(pieces, axis=0))
    o = jnp.concatenate(o_heads, axis=1)
    oms = _dot_sel_right(o * o, hmean_ref[...], pieces=2)
    o = o * lax.rsqrt(oms + EPS) * hgain_ref[...]
    o = o * _silu(gg)

    mix = jnp.concatenate([conv_out, o], axis=1)
    x1 = x + _bdot(mix, wout_ref[...])
    x1_ref[0] = x1
    h2 = _rms(x1, nffn_ref[...])
    h2_ref[0] = h2.astype(BF16)
    q2 = _bdot(h2, wq_ref[...])
    for hp in range(2 * PEER_HEADS):
        sc_ref[hp] = _bdot_nt(keys_ref[hp], q2[:, hp * HALF_Q:(hp + 1) * HALF_Q])


def _mixer_params(norm_mix, w_in, conv_w, conv_gain, lb_logits, hg_gain, w_out, norm_ffn, wq, keys):
    L = SEQ_BLOCK
    r = np.arange(L)
    same = (r[:, None] // CHUNK) == (r[None, :] // CHUNK)
    cum = np.concatenate([same & (r[None, :] <= r[:, None]), same], axis=0).astype(np.float32)
    c = np.arange(D_CONV)
    gmean = ((c[:, None] // CONV_GROUP) == (c[None, :] // CONV_GROUP)).astype(np.float32) / CONV_GROUP
    hmean = ((c[:, None] // HG_VD) == (c[None, :] // HG_VD)).astype(np.float32) / HG_VD
    return (norm_mix.reshape(1, D_MODEL), w_in.astype(BF16), conv_w, conv_gain.reshape(1, D_CONV), lb_logits,
            jnp.tile(hg_gain.reshape(1, HG_VD), (1, HG_HEADS)), w_out.astype(BF16), norm_ffn.reshape(1, D_MODEL),
            wq.astype(BF16), keys.reshape(2 * PEER_HEADS, N_KEYS, HALF_Q).astype(BF16),
            jnp.asarray(cum, BF16), jnp.asarray(gmean, BF16), jnp.asarray(hmean, BF16))


def _mixer_call(x, params, b0, nbatch):
    S = x.shape[1]
    L = SEQ_BLOCK
    nsb = S // L
    const = lambda a: pl.BlockSpec(a.shape, lambda b, s: (0,) * a.ndim)
    return pl.pallas_call(
        _mixer_kernel,
        grid=(nbatch, nsb),
        in_specs=[pl.BlockSpec((1, L, D_MODEL), lambda b, s: (b + b0, s, 0))] + [const(a) for a in params],
        out_specs=[
            pl.BlockSpec((1, L, D_MODEL), lambda b, s: (b, s, 0)),
            pl.BlockSpec((1, L, D_MODEL), lambda b, s: (b, s, 0)),
            pl.BlockSpec((2 * PEER_HEADS, N_KEYS, L), lambda b, s: (0, 0, b * nsb + s)),
        ],
        out_shape=[
            jax.ShapeDtypeStruct((nbatch, S, D_MODEL), F32),
            jax.ShapeDtypeStruct((nbatch, S, D_MODEL), BF16),
            jax.ShapeDtypeStruct((2 * PEER_HEADS, N_KEYS, nbatch * S), F32),
        ],
        scratch_shapes=[
            pltpu.VMEM((L + 2 * SUBLANES, D_CONV), F32),
            pltpu.VMEM((HG_HEADS, HG_VD, HG_KD), F32),
        ],
        compiler_params=pltpu.CompilerParams(
            dimension_semantics=("arbitrary", "arbitrary"), vmem_limit_bytes=VMEM_LIMIT_MIXER),
        name="mixer",
    )(x, *params)


def _extract_topk(vals, ids, n, sentinel):
    tops, sel = [], []
    for _ in range(n):
        m = jnp.max(vals, axis=0, keepdims=True)
        i = jnp.min(jnp.where(vals == m, ids, sentinel), axis=0, keepdims=True)
        tops.append(m)
        sel.append(i)
        vals = jnp.where(ids == i, -jnp.inf, vals)
    return jnp.concatenate(tops, axis=0), jnp.concatenate(sel, axis=0)


def _topk_head(hd, sc_ref, gate_ref, exp_ref):
    TB = TOK_BLOCK
    K = PEER_TOPK
    iota_keys = lax.broadcasted_iota(I32, (N_KEYS, TB), 0).astype(F32)
    sub = lax.broadcasted_iota(I32, (SUBLANES, TB), 0)
    subf = sub.astype(F32)
    s1, i1t = _extract_topk(sc_ref[2 * hd], iota_keys, K, N_KEYS)
    s2, i2t = _extract_topk(sc_ref[2 * hd + 1], iota_keys, K, N_KEYS)
    vals, ids = [], []
    for b0 in range(0, K, SUBLANES):
        vals.append(s1[0:1, :] + s2[b0:b0 + SUBLANES, :])
        ids.append(subf + b0)
    for a in range(1, SUBLANES):
        v = s1[a:a + 1, :] + s2[0:SUBLANES, :]
        vals.append(jnp.where(sub < K // (a + 1), v, -jnp.inf))
        ids.append(subf + a * K)
    vals.append(s1[SUBLANES:K, :] + s2[0:1, :])
    ids.append((subf + SUBLANES) * K)
    c_score, c_idx = _extract_topk(jnp.concatenate(vals, axis=0), jnp.concatenate(ids, axis=0), K, K * K)
    c_idx = c_idx.astype(I32)
    i1t = i1t.astype(I32)
    i2t = i2t.astype(I32)
    a_sel = lax.shift_right_logical(c_idx, 4)
    b_sel = c_idx & (K - 1)
    i1 = jnp.zeros((K, TB), I32)
    i2 = jnp.zeros((K, TB), I32)
    for a in range(K):
        i1 = i1 + jnp.where(a_sel == a, i1t[a:a + 1, :], 0)
        i2 = i2 + jnp.where(b_sel == a, i2t[a:a + 1, :], 0)
    e = jnp.exp(c_score - jnp.max(c_score, axis=0, keepdims=True))
    row0 = pl.multiple_of(hd * K, K)
    gate_ref[0, pl.ds(row0, K), :] = e / jnp.sum(e, axis=0, keepdims=True)
    exp_ref[pl.ds(row0, K), :] = (i1 * N_KEYS + i2) * ROW_WORDS


def _pack_offsets(exp_ref):
    half = N_PAIRS // 2
    return exp_ref[0:half, :] | (exp_ref[half:N_PAIRS, :] << 16)


def _topk_kernel(sc_ref, idx_ref, gate_ref, exp_ref):
    def head(hd, c):
        _topk_head(hd, sc_ref, gate_ref, exp_ref)
        return c

    lax.fori_loop(0, PEER_HEADS, head, 0)
    idx_ref[0] = _pack_offsets(exp_ref)


def _topk_call(scores_t):
    nblk = scores_t.shape[-1] // TOK_BLOCK
    return pl.pallas_call(
        _topk_kernel,
        grid=(nblk,),
        in_specs=[pl.BlockSpec((2 * PEER_HEADS, N_KEYS, TOK_BLOCK), lambda i: (0, 0, i))],
        out_specs=[
            pl.BlockSpec((1, N_PAIRS // 2, TOK_BLOCK), lambda i: (i, 0, 0)),
            pl.BlockSpec((1, N_PAIRS, TOK_BLOCK), lambda i: (i, 0, 0)),
        ],
        out_shape=[
            jax.ShapeDtypeStruct((nblk, N_PAIRS // 2, TOK_BLOCK), I32),
            jax.ShapeDtypeStruct((nblk, N_PAIRS, TOK_BLOCK), F32),
        ],
        scratch_shapes=[pltpu.VMEM((N_PAIRS, TOK_BLOCK), I32)],
        compiler_params=pltpu.CompilerParams(dimension_semantics=("arbitrary",)),
        name="topk",
    )(scores_t)


def _pack_kernel(tab_ref, out_ref, *, one_row_per_entry):
    rows = tab_ref.shape[0]
    bits = lambda v: lax.bitcast_convert_type(v.astype(BF16).astype(F32), jnp.uint32)
    for i in range(ROW_WORDS):
        lo = bits(tab_ref[:, (2 * i) * LANES:(2 * i + 1) * LANES]) >> 16
        hi = bits(tab_ref[:, (2 * i + 1) * LANES:(2 * i + 2) * LANES]) & jnp.uint32(0xFFFF0000)
        words = lax.bitcast_convert_type(lo | hi, I32)
        if one_row_per_entry:
            out_ref[:, i * LANES:(i + 1) * LANES] = words
        else:
            out_ref[pl.ds(i, rows, stride=ROW_WORDS), :] = words


def _pack_table(tab, one_row_per_entry):
    n = tab.shape[0]
    if one_row_per_entry:
        out_block, out_shape = (PACK_ROWS, ROW_WORDS * LANES), (n, ROW_WORDS * LANES)
    else:
        out_block, out_shape = (PACK_ROWS * ROW_WORDS, LANES), (n * ROW_WORDS, LANES)
    return pl.pallas_call(
        functools.partial(_pack_kernel, one_row_per_entry=one_row_per_entry),
        grid=(n // PACK_ROWS,),
        in_specs=[pl.BlockSpec((PACK_ROWS, D_MODEL), lambda i: (i, 0))],
        out_specs=pl.BlockSpec(out_block, lambda i: (i, 0)),
        out_shape=jax.ShapeDtypeStruct(out_shape, I32),
        compiler_params=pltpu.CompilerParams(dimension_semantics=("arbitrary",)),
        name="pack_table",
    )(tab)


def _load_table(tab_hbm, tab_ref, sem):
    @pl.when(pl.program_id(0) == 0)
    def _():
        copy = pltpu.make_async_copy(tab_hbm, tab_ref, sem)
        copy.start()
        copy.wait()


def _pair_offsets(idx_ref, w, j):
    word = idx_ref[w * TOK_BLOCK + j]
    lo = pl.multiple_of(word & 0xFFFF, ROW_WORDS)
    hi = pl.multiple_of(lax.shift_right_logical(word, 16), ROW_WORDS)
    return ((w, lo), (w + N_PAIRS // 2, hi))


def _gather_rows(idx_ref, tab_ref, g_ref, j):
    for w in range(N_PAIRS // 2):
        for k, off in _pair_offsets(idx_ref, w, j):
            g_ref[pl.ds(k * ROW_WORDS, ROW_WORDS), :] = tab_ref[pl.ds(off, ROW_WORDS), :]


def _two_token_step(consume, gather, g0_ref, g1_ref, j):
    consume(g0_ref, j)
    consume(g1_ref, j + 1)
    gather(g0_ref, jnp.minimum(j + 2, TOK_BLOCK - 1))
    gather(g1_ref, jnp.minimum(j + 3, TOK_BLOCK - 1))


def _expert_u_kernel(idx_ref, tab_hbm, h2_ref, gate_ref, gsum_ref, sc_ref,
                     w_ref, idx_next_ref, gate_next_ref,
                     tab_ref, tab_sem, g0_ref, g1_ref, xs_ref, araw_ref, exp_ref):
    TB = TOK_BLOCK
    nchunk = D_MODEL // LANES
    group = 2 * LANES // nchunk
    pitch = ARAW_PITCH
    _load_table(tab_hbm, tab_ref, tab_sem)

    for c in range(nchunk):
        xs_ref[pl.ds(c, TB, stride=nchunk), :] = h2_ref[:, c * LANES:(c + 1) * LANES].astype(F32)

    def dots(g_ref, j):
        xj = xs_ref[pl.ds(pl.multiple_of(j * nchunk, nchunk), nchunk), :].astype(BF16)
        for g in range(N_PAIRS // group):
            wgt = pltpu.bitcast(g_ref[pl.ds(g * LANES, LANES), :], BF16)
            r = lax.dot_general(xj, wgt, NT_DIMS, preferred_element_type=F32)
            for half in range(2):
                cc = 2 * g + half
                araw_ref[pl.ds(cc * nchunk * pitch + j, nchunk, stride=pitch), :] = r[:, half * LANES:(half + 1) * LANES]

    gather = functools.partial(_gather_rows, idx_ref, tab_ref)
    tokens_per_head = TB // PEER_HEADS
    gather(g0_ref, 0)
    gather(g1_ref, 1)

    def head_step(hd, c):
        _topk_head(hd, sc_ref, gate_next_ref, exp_ref)
        for s in range(tokens_per_head // 2):
            _two_token_step(dots, gather, g0_ref, g1_ref, hd * tokens_per_head + 2 * s)
        return c

    lax.fori_loop(0, PEER_HEADS, head_step, 0)
    idx_next_ref[0] = _pack_offsets(exp_ref)

    lane_chunk = lax.broadcasted_iota(I32, (TB, LANES), 1) % nchunk
    a = jnp.zeros((TB, N_PAIRS), F32)
    for cc in range(nchunk):
        part = jnp.zeros((TB, LANES), F32)
        for c in range(nchunk):
            part = part + jnp.where(lane_chunk == c, araw_ref[pl.ds((cc * nchunk + c) * pitch, TB), :], 0.0)
        a = a + _dot_sel_right(part, gsum_ref[cc * LANES:(cc + 1) * LANES, :])
    w_ref[...] = _gated_gelu(a, gate_ref[0])


def _gated_gelu(a, gates):
    return gates.T * (0.5 * a * (1.0 + lax.erf(a * (2.0 ** -0.5))))


def _gated_gelu_kernel(a_ref, gate_ref, w_ref):
    w_ref[...] = _gated_gelu(a_ref[...], gate_ref[0])


def _gated_gelu_call(a, gates):
    nblk = gates.shape[0]
    return pl.pallas_call(
        _gated_gelu_kernel,
        grid=(nblk,),
        in_specs=[pl.BlockSpec((TOK_BLOCK, N_PAIRS), lambda i: (i, 0)),
                  pl.BlockSpec((1, N_PAIRS, TOK_BLOCK), lambda i: (i, 0, 0))],
        out_specs=pl.BlockSpec((TOK_BLOCK, N_PAIRS), lambda i: (i, 0)),
        out_shape=jax.ShapeDtypeStruct(a.shape, F32),
        compiler_params=pltpu.CompilerParams(dimension_semantics=("arbitrary",)),
        name="gated_gelu",
    )(a, gates)


def _pair_expand():
    pair_of_col = np.arange(D_MODEL) // SUBLANES
    return (np.arange(N_PAIRS)[:, None] == pair_of_col[None, :]).astype(np.float32)


def _expert_u_call(idx_words, tab, h2, gates, scores_next):
    nchunk = D_MODEL // LANES
    nblk = h2.shape[0] // TOK_BLOCK
    return pl.pallas_call(
        _expert_u_kernel,
        grid=(nblk,),
        in_specs=[
            pl.BlockSpec((N_PAIRS // 2 * TOK_BLOCK,), lambda i: (i,), memory_space=pltpu.SMEM),
            pl.BlockSpec(memory_space=pl.ANY),
            pl.BlockSpec((TOK_BLOCK, D_MODEL), lambda i: (i, 0)),
            pl.BlockSpec((1, N_PAIRS, TOK_BLOCK), lambda i: (i, 0, 0)),
            pl.BlockSpec((D_MODEL, N_PAIRS), lambda i: (0, 0)),
            pl.BlockSpec((2 * PEER_HEADS, N_KEYS, TOK_BLOCK), lambda i: (0, 0, i)),
        ],
        out_specs=[
            pl.BlockSpec((TOK_BLOCK, N_PAIRS), lambda i: (i, 0)),
            pl.BlockSpec((1, N_PAIRS // 2, TOK_BLOCK), lambda i: (i, 0, 0)),
            pl.BlockSpec((1, N_PAIRS, TOK_BLOCK), lambda i: (i, 0, 0)),
        ],
        out_shape=[
            jax.ShapeDtypeStruct((nblk * TOK_BLOCK, N_PAIRS), F32),
            jax.ShapeDtypeStruct((nblk, N_PAIRS // 2, TOK_BLOCK), I32),
            jax.ShapeDtypeStruct((nblk, N_PAIRS, TOK_BLOCK), F32),
        ],
        scratch_shapes=[
            pltpu.VMEM(tab.shape, I32),
            pltpu.SemaphoreType.DMA,
            pltpu.VMEM((N_PAIRS * ROW_WORDS, LANES), I32),
            pltpu.VMEM((N_PAIRS * ROW_WORDS, LANES), I32),
            pltpu.VMEM((TOK_BLOCK * nchunk, LANES), F32),
            pltpu.VMEM((nchunk * nchunk * ARAW_PITCH, LANES), F32),
            pltpu.VMEM((N_PAIRS, TOK_BLOCK), I32),
        ],
        compiler_params=pltpu.CompilerParams(
            dimension_semantics=("arbitrary",), vmem_limit_bytes=VMEM_LIMIT_EXPERT),
        name="expert_u",
    )(idx_words, tab, h2, gates, jnp.asarray(_pair_expand().T, BF16), scores_next)


def _sc_rows_call(table_words, ids, vec, dot):
    ts = ids.shape[0]
    row_len = table_words.shape[1]
    vec_len = vec.shape[1]
    out_len = N_PAIRS if dot else D_MODEL
    workers = SC_CORES * SC_SUBCORES
    tpw = ts // workers
    nbatch = N_PAIRS // SC_ROWS
    groups = LANES // SC_LANES
    mesh = plsc.VectorSubcoreMesh(core_axis_name="c", subcore_axis_name="s")
    params = pltpu.CompilerParams()
    if "needs_layout_passes" in pltpu.CompilerParams.__dataclass_fields__:
        params = dataclasses.replace(params, needs_layout_passes=False)

    @functools.partial(
        pl.kernel, mesh=mesh, compiler_params=params,
        out_type=jax.ShapeDtypeStruct((ts, out_len), F32),
        scratch_types=[
            pltpu.VMEM((N_PAIRS,), I32), pltpu.VMEM((N_PAIRS,), I32),
            pltpu.VMEM((vec_len,), F32), pltpu.VMEM((vec_len,), F32),
            pltpu.VMEM((SC_ROWS, row_len), I32), pltpu.VMEM((SC_ROWS, row_len), I32),
            pltpu.VMEM((out_len,), F32),
            pltpu.SemaphoreType.DMA, pltpu.SemaphoreType.DMA, pltpu.SemaphoreType.DMA, pltpu.SemaphoreType.DMA,
        ],
    )
    def sc_kernel(table_hbm, ids_hbm, vec_hbm, out_hbm, ids0, ids1, vec0, vec1, rows0, rows1, out_v,
                  sem0, sem1, sem_ids, sem_vec):
        worker = lax.axis_index("s") * SC_CORES + lax.axis_index("c")
        base = worker * tpw
        ids_v, vec_v, rows_v, sems = (ids0, ids1), (vec0, vec1), (rows0, rows1), (sem0, sem1)

        def gather(slot, b, buf):
            src = table_hbm.at[ids_v[slot].at[pl.ds(b * SC_ROWS, SC_ROWS)]]
            return pltpu.make_async_copy(src, rows_v[buf], sems[buf])

        def fetch(slot, tok):
            return (pltpu.make_async_copy(ids_hbm.at[tok], ids_v[slot], sem_ids),
                    pltpu.make_async_copy(vec_hbm.at[tok], vec_v[slot], sem_vec))

        def halves(buf, r, off):
            word = rows_v[buf][r, pl.ds(off, SC_LANES)]
            return (lax.bitcast_convert_type(word << 16, F32),
                    lax.bitcast_convert_type(word & jnp.int32(-65536), F32))

        def feature_offsets(g):
            f_lo = pl.multiple_of((2 * (g // groups)) * LANES + (g % groups) * SC_LANES, SC_LANES)
            return pl.multiple_of(g * SC_LANES, SC_LANES), f_lo, pl.multiple_of(f_lo + LANES, SC_LANES)

        def weighted_sum(slot, b, buf):
            wvs = [vec_v[slot][pl.ds(b * SC_ROWS + q * SC_LANES, SC_LANES)] for q in range(SC_ROWS // SC_LANES)]
            wrs = [jnp.broadcast_to(wvs[r // SC_LANES][r % SC_LANES], (SC_LANES,)) for r in range(SC_ROWS)]

            def group(g, c):
                off, f_lo, f_hi = feature_offsets(g)
                if b == 0:
                    acc_lo = jnp.zeros((SC_LANES,), F32)
                    acc_hi = jnp.zeros((SC_LANES,), F32)
                else:
                    acc_lo = out_v[pl.ds(f_lo, SC_LANES)]
                    acc_hi = out_v[pl.ds(f_hi, SC_LANES)]
                for r in range(SC_ROWS):
                    lo, hi = halves(buf, r, off)
                    acc_lo = acc_lo + wrs[r] * lo
                    acc_hi = acc_hi + wrs[r] * hi
                out_v[pl.ds(f_lo, SC_LANES)] = acc_lo
                out_v[pl.ds(f_hi, SC_LANES)] = acc_hi
                return c

            lax.fori_loop(0, row_len // SC_LANES, group, 0)

        def row_dots(slot, b, buf):
            def group(g, accs):
                off, f_lo, f_hi = feature_offsets(g)
                x_lo = vec_v[slot][pl.ds(f_lo, SC_LANES)]
                x_hi = vec_v[slot][pl.ds(f_hi, SC_LANES)]
                new = []
                for r in range(SC_ROWS):
                    lo, hi = halves(buf, r, off)
                    new.append(accs[r] + lo * x_lo + hi * x_hi)
                return tuple(new)

            zero = jnp.zeros((SC_LANES,), F32)
            accs = lax.fori_loop(0, row_len // SC_LANES, group, (zero,) * SC_ROWS)
            lane = lax.broadcasted_iota(I32, (SC_LANES,), 0)
            for q in range(SC_ROWS // SC_LANES):
                res = zero
                for r in range(SC_LANES):
                    res = jnp.where(lane == r, jnp.sum(accs[q * SC_LANES + r]), res)
                out_v[pl.ds(b * SC_ROWS + q * SC_LANES, SC_LANES)] = res

        accumulate = row_dots if dot else weighted_sum

        for copy in fetch(0, base):
            copy.start()
        for copy in fetch(0, base):
            copy.wait()
        gather(0, 0, 0).start()

        def token_pair(p, c):
            t0 = base + 2 * p
            t_next = base + jnp.minimum(2 * p + 2, tpw - 1)
            for q in range(2 * nbatch):
                slot, b, buf = q // nbatch, q % nbatch, q % 2
                if q == 0:
                    for copy in fetch(1, t0 + 1):
                        copy.start()
                if q == nbatch:
                    for copy in fetch(0, t_next):
                        copy.start()
                if q == nbatch - 1:
                    for copy in fetch(1, t0 + 1):
                        copy.wait()
                if q == 2 * nbatch - 1:
                    for copy in fetch(0, t_next):
                        copy.wait()
                nq = (q + 1) % (2 * nbatch)
                gather(nq // nbatch, nq % nbatch, nq % 2).start()
                gather(slot, b, buf).wait()
                accumulate(slot, b, buf)
                if b == nbatch - 1:
                    pltpu.sync_copy(out_v, out_hbm.at[t0 + slot])
            return c

        lax.fori_loop(0, tpw // 2, token_pair, 0)
        gather(0, 0, 0).wait()

    return sc_kernel(table_words, ids, vec)


def _residual_norm_kernel(*refs):
    x1_ref, y_ref, nf_ref, out_ref = refs[-4:]
    out_ref[...] = _rms(x1_ref[...] + y_ref[...], nf_ref[...])


def _residual_norm_call(out_prev, x1, y, norm_f, chunk, total_tokens):
    nblk = y.shape[0] // NORM_BLOCK
    here = pl.BlockSpec((NORM_BLOCK, D_MODEL), lambda i: (i + chunk * nblk, 0))
    local = pl.BlockSpec((NORM_BLOCK, D_MODEL), lambda i: (i, 0))
    in_specs = [local, local, pl.BlockSpec((1, D_MODEL), lambda i: (0, 0))]
    args = (x1, y, norm_f.reshape(1, D_MODEL))
    aliases = {}
    if out_prev is not None:
        in_specs = [pl.BlockSpec(memory_space=pl.ANY)] + in_specs
        args = (out_prev,) + args
        aliases = {0: 0}
    return pl.pallas_call(
        _residual_norm_kernel,
        grid=(nblk,),
        in_specs=in_specs,
        out_specs=here,
        out_shape=jax.ShapeDtypeStruct((total_tokens, D_MODEL), F32),
        input_output_aliases=aliases,
        compiler_params=pltpu.CompilerParams(dimension_semantics=("arbitrary",)),
        name="residual_norm",
    )(*args)


def _expert_ids(idx_words):
    lo = (idx_words & 0xFFFF) // ROW_WORDS
    hi = lax.shift_right_logical(idx_words, 16) // ROW_WORDS
    ids = jnp.concatenate([lo, hi], axis=1)
    return ids.transpose(0, 2, 1).reshape(-1, N_PAIRS)


def kernel(x, norm_mix, w_in, conv_w, conv_gain, hg_lb_logits, hg_gain, w_out, norm_ffn, peer_wq, peer_keys,
           peer_u, peer_v, norm_f):
    B, S, D = x.shape
    depth = norm_mix.shape[0]
    assert depth == 1 and D == D_MODEL and S % SEQ_BLOCK == 0 and B % N_CHUNKS == 0 and S % TOK_BLOCK == 0
    T = B * S
    bc = B // N_CHUNKS
    nb = bc * S // TOK_BLOCK
    params = _mixer_params(norm_mix[0], w_in[0], conv_w[0], conv_gain[0], hg_lb_logits, hg_gain[0],
                           w_out[0], norm_ffn[0], peer_wq[0], peer_keys[0])
    tab_u = _pack_table(peer_u[0], one_row_per_entry=False)
    tab_u_rows = _pack_table(peer_u[0], one_row_per_entry=True)
    tab_v = _pack_table(peer_v[0], one_row_per_entry=True)
    mixed = [_mixer_call(x, params, 0, bc)]
    idx_k, gates_k = None, None
    out = None
    head = None
    for k in range(N_CHUNKS):
        if k + 1 < N_CHUNKS:
            mixed.append(_mixer_call(x, params, (k + 1) * bc, bc))
        x1_k, h2_k, scores_k = mixed[k]
        h2_k = h2_k.reshape(bc * S, D)
        x1_k = x1_k.reshape(bc * S, D)
        scores_next = mixed[min(k + 1, N_CHUNKS - 1)][2]
        if k == 0:
            idx_k, gates_k = _topk_call(scores_k)
            ids_0 = _expert_ids(idx_k)
            head = (_sc_rows_call(tab_u_rows, ids_0, h2_k.astype(F32), dot=True), gates_k, ids_0, x1_k)
            idx_k, gates_k = _topk_call(scores_next)
            continue
        ids_k = _expert_ids(idx_k)
        w_k, idx_k, gates_k = _expert_u_call(idx_k.reshape(-1), tab_u, h2_k, gates_k, scores_next)
        if head is not None:
            a_0, gates_0, ids_0, x1_0 = head
            y_0 = _sc_rows_call(tab_v, ids_0, _gated_gelu_call(a_0, gates_0), dot=False)
            out = _residual_norm_call(out, x1_0, y_0, norm_f, 0, T)
            head = None
        y_k = _sc_rows_call(tab_v, ids_k, w_k, dot=False)
        out = _residual_norm_call(out, x1_k, y_k, norm_f, k, T)
    return out.reshape(B, S, D)
```
